```python
import math
import jax
import jax.numpy as jnp
from jax import lax
import numpy as np

D_MODEL = 1024
BATCH = 4
SEQ = 4096
DEPTH = 4
DEC_BATCH = 128
DEC_SEQ = 1
PAST_LEN = 8192
PAGE_SIZE = 128

N_MIXERS = 3
N_A = (DEPTH + 2) // N_MIXERS
N_B = (DEPTH + 1) // N_MIXERS
N_C = DEPTH // N_MIXERS

RMS_EPS = 1e-6
ROPE_THETA = 10000.0
Q_BLOCK = 128
NEG_INF = -1e30

MLA_HEADS = 8
MLA_Q_LORA = 384
MLA_KV_LORA = 256
MLA_NOPE = 128
MLA_ROPE = 64
MLA_V = 128
MLA_SCALE = (MLA_NOPE + MLA_ROPE) ** -0.5

DIFF_HEADS = 8
DIFF_KV_HEADS = 4
DIFF_REP = DIFF_HEADS // DIFF_KV_HEADS
DIFF_HEAD_DIM = D_MODEL // DIFF_HEADS // 2
DIFF_SCALE = DIFF_HEAD_DIM ** -0.5

SSD_D_INNER = 2 * D_MODEL
SSD_HEAD_DIM = 64
SSD_HEADS = SSD_D_INNER // SSD_HEAD_DIM
SSD_GROUPS = 4
SSD_STATE = 128
SSD_CONV = 4
SSD_CHUNK = 128
SSD_CONV_DIM = SSD_D_INNER + 2 * SSD_GROUPS * SSD_STATE
SSD_IN_DIM = 2 * SSD_D_INNER + 2 * SSD_GROUPS * SSD_STATE + SSD_HEADS

MEM_TOKENS = 256
XA_HEADS = 4
XA_HEAD_DIM = D_MODEL // XA_HEADS
XA_SCALE = XA_HEAD_DIM ** -0.5

FFN_HIDDEN = -(-(8 * D_MODEL) // (3 * 256)) * 256

kernel_name = 'hybrid_mla_diff_ssd_decoder_step'


def rmsnorm(x, g):
    xf = x.astype(jnp.float32)
    y = xf * lax.rsqrt(jnp.mean(xf * xf, axis=-1, keepdims=True) + RMS_EPS)
    return (y * g.astype(jnp.float32)).astype(x.dtype)


def rope(x, pos):
    d = x.shape[-1]
    half = d // 2
    inv = ROPE_THETA ** (-jnp.arange(half, dtype=jnp.float32) * 2.0 / d)
    ang = pos.astype(jnp.float32)[:, None] * inv[None, :]
    cos = jnp.cos(ang)[None, :, None, :]
    sin = jnp.sin(ang)[None, :, None, :]
    xf = x.astype(jnp.float32)
    x1, x2 = xf[..., :half], xf[..., half:]
    return jnp.concatenate([x1 * cos - x2 * sin, x2 * cos + x1 * sin], axis=-1).astype(x.dtype)


def masked_softmax(s, qpos, kpos):
    allowed = kpos[None, :] <= qpos[:, None]
    return jax.nn.softmax(jnp.where(allowed, s, NEG_INF), axis=-1)


def weighted_values(eq, p, values):
    out, off = None, 0
    for v in values:
        n = v.shape[1]
        term = jnp.einsum(eq, p[..., off:off + n], v)
        out = term if out is None else out + term
        off += n
    return out


def map_query_blocks(fn, qs, pos):
    n_blocks = pos.shape[0] // Q_BLOCK

    def to_blocks(a):
        return jnp.moveaxis(a.reshape(a.shape[0], n_blocks, Q_BLOCK, *a.shape[2:]), 1, 0)

    out = lax.map(lambda args: fn(*args[0], args[1]),
                  (tuple(to_blocks(a) for a in qs), pos.reshape(n_blocks, Q_BLOCK)))
    out = jnp.moveaxis(out, 0, 1)
    return out.reshape(out.shape[0], n_blocks * Q_BLOCK, *out.shape[3:])


def gather_pages(pool, j, page_table):
    g = pool[j, page_table]
    return g.reshape(g.shape[0], g.shape[1] * g.shape[2], *g.shape[3:])


def mla_project(h, pos, wq_a, q_norm, wq_b, wkv_a, kv_norm, w_uk):
    B, T, _ = h.shape
    q = (rmsnorm(h @ wq_a, q_norm) @ wq_b).reshape(B, T, MLA_HEADS, MLA_NOPE + MLA_ROPE)
    q_pe = rope(q[..., MLA_NOPE:], pos)
    q_lat = jnp.einsum('bthd,chd->bthc', q[..., :MLA_NOPE], w_uk)
    kv = h @ wkv_a
    ckv = rmsnorm(kv[..., :MLA_KV_LORA], kv_norm)
    kpe = rope(kv[..., None, MLA_KV_LORA:], pos)[:, :, 0]
    return q_lat, q_pe, ckv, kpe


def mla_attend(q_lat, q_pe, segs, qpos):
    s = jnp.concatenate([jnp.einsum('bqhc,bkc->bhqk', q_lat, c) + jnp.einsum('bqhr,bkr->bhqk', q_pe, r)
                         for c, r, _ in segs], axis=-1)
    kpos = jnp.concatenate([kp for _, _, kp in segs])
    p = masked_softmax(s.astype(jnp.float32) * MLA_SCALE, qpos, kpos).astype(q_lat.dtype)
    return weighted_values('bhqk,bkc->bqhc', p, [c for c, _, _ in segs])


def mla_output(o_lat, w_uv, wo):
    o = jnp.einsum('bthc,chd->bthd', o_lat, w_uv)
    return o.reshape(o.shape[0], o.shape[1], -1) @ wo


def diff_project(h, pos, wq, wk, wv):
    B, T, _ = h.shape
    q = rope((h @ wq).reshape(B, T, DIFF_HEADS * 2, DIFF_HEAD_DIM), pos)
    k = rope((h @ wk).reshape(B, T, DIFF_KV_HEADS * 2, DIFF_HEAD_DIM), pos)
    q = q.reshape(B, T, DIFF_HEADS, 2, DIFF_HEAD_DIM)
    k = k.reshape(B, T, DIFF_KV_HEADS, 2, DIFF_HEAD_DIM)
    v = (h @ wv).reshape(B, T, DIFF_KV_HEADS, 2 * DIFF_HEAD_DIM)
    return q, k, v


def diff_lambda(lq1, lk1, lq2, lk2, lambda_init):
    f = lambda a: a.astype(jnp.float32)
    return jnp.exp(jnp.sum(f(lq1) * f(lk1))) - jnp.exp(jnp.sum(f(lq2) * f(lk2))) + lambda_init


def diff_attend(q, segs, lam, qpos):
    B, Tq = q.shape[:2]
    qg = q.reshape(B, Tq, DIFF_KV_HEADS, DIFF_REP, 2, DIFF_HEAD_DIM)
    s = jnp.concatenate([jnp.einsum('bqgrcd,bkgcd->bgrcqk', qg, k) for k, _, _ in segs], axis=-1)
    kpos = jnp.concatenate([kp for _, _, kp in segs])
    p = masked_softmax(s.astype(jnp.float32) * DIFF_SCALE, qpos, kpos)
    a = (p[:, :, :, 0] - lam * p[:, :, :, 1]).astype(q.dtype)
    o = weighted_values('bgrqk,bkgd->bqgrd', a, [v for _, v, _ in segs])
    return o.reshape(B, Tq, DIFF_HEADS, 2 * DIFF_HEAD_DIM)


def diff_output(o, subln, wo, lambda_init):
    o = rmsnorm(o, subln) * (1.0 - lambda_init)
    return o.reshape(o.shape[0], o.shape[1], -1) @ wo


def segsum(a):
    T = a.shape[-1]
    ar = jnp.broadcast_to(a[..., None], a.shape + (T,))
    ar = jnp.where(jnp.tril(jnp.ones((T, T), bool), -1), ar, 0.0)
    cs = jnp.cumsum(ar, axis=-2)
    return jnp.where(jnp.tril(jnp.ones((T, T), bool)), cs, -jnp.inf)


def ssd_chunked(x, dt, A, Bm, Cm):
    b, l, h, p = x.shape
    nc = l // SSD_CHUNK
    rep = h // Bm.shape[2]
    Bh = jnp.repeat(Bm, rep, axis=2).reshape(b, nc, SSD_CHUNK, h, -1)
    Ch = jnp.repeat(Cm, rep, axis=2).reshape(b, nc, SSD_CHUNK, h, -1)
    xdt = (x * dt[..., None]).reshape(b, nc, SSD_CHUNK, h, p)
    a = jnp.moveaxis((dt * A).reshape(b, nc, SSD_CHUNK, h), 3, 1)
    a_cs = jnp.cumsum(a, axis=-1)
    L = jnp.exp(segsum(a))
    y_diag = jnp.einsum('bcqhn,bcshn,bhcqs,bcshp->bcqhp', Ch, Bh, L, xdt)
    decay = jnp.exp(a_cs[..., -1:] - a_cs)
    states = jnp.einsum('bcshn,bhcs,bcshp->bchpn', Bh, decay, xdt)
    states = jnp.concatenate([jnp.zeros_like(states[:, :1]), states], axis=1)
    chunk_decay = jnp.exp(segsum(jnp.pad(a_cs[..., -1], ((0, 0), (0, 0), (1, 0)))))
    states = jnp.einsum('bhzc,bchpn->bzhpn', chunk_decay, states)
    y_off = jnp.einsum('bcqhn,bchpn,bhcq->bcqhp', Ch, states[:, :-1], jnp.exp(a_cs))
    return (y_diag + y_off).reshape(b, l, h, p), states[:, -1]


def ssd_recurrent(x, dt, A, Bm, Cm, h0):
    rep = x.shape[2] // Bm.shape[2]
    Bh = jnp.repeat(Bm, rep, axis=2)
    Ch = jnp.repeat(Cm, rep, axis=2)

    def step(state, inp):
        x_t, dt_t, b_t, c_t = inp
        state = state * jnp.exp(dt_t * A)[..., None, None] + jnp.einsum('bhp,bhn->bhpn', x_t * dt_t[..., None], b_t)
        return state, jnp.einsum('bhpn,bhn->bhp', state, c_t)

    hT, ys = lax.scan(step, h0, tuple(jnp.moveaxis(a, 1, 0) for a in (x, dt, Bh, Ch)))
    return jnp.moveaxis(ys, 0, 1), hT


def causal_dwconv(xpad, w, bias):
    K = w.shape[0]
    T = xpad.shape[1] - (K - 1)
    return sum(xpad[:, k:k + T] * w[k] for k in range(K)) + bias


def ssd_mixer(h, conv_buf, ssm_state, w_in, conv_w, conv_b, dt_bias, A_log, D_skip, norm_w, w_out, chunked):
    B, T, _ = h.shape
    f32 = jnp.float32
    zxbcdt = h @ w_in
    z = zxbcdt[..., :SSD_D_INNER]
    xbc = zxbcdt[..., SSD_D_INNER:SSD_D_INNER + SSD_CONV_DIM]
    dt = zxbcdt[..., SSD_D_INNER + SSD_CONV_DIM:]
    xpad = jnp.concatenate([conv_buf.astype(xbc.dtype), xbc], axis=1)
    new_buf = xpad[:, -(SSD_CONV - 1):]
    xbc = jax.nn.silu(causal_dwconv(xpad, conv_w, conv_b))
    gn = SSD_GROUPS * SSD_STATE
    xs = xbc[..., :SSD_D_INNER].reshape(B, T, SSD_HEADS, SSD_HEAD_DIM).astype(f32)
    Bm = xbc[..., SSD_D_INNER:SSD_D_INNER + gn].reshape(B, T, SSD_GROUPS, SSD_STATE).astype(f32)
    Cm = xbc[..., SSD_D_INNER + gn:].reshape(B, T, SSD_GROUPS, SSD_STATE).astype(f32)
    dt = jax.nn.softplus(dt.astype(f32) + dt_bias.astype(f32))
    A = -jnp.exp(A_log.astype(f32))
    if chunked:
        y, new_state = ssd_chunked(xs, dt, A, Bm, Cm)
    else:
        y, new_state = ssd_recurrent(xs, dt, A, Bm, Cm, ssm_state.astype(f32))
    y = y + D_skip.astype(f32)[:, None] * xs
    yg = (y.reshape(B, T, SSD_D_INNER) * jax.nn.silu(z.astype(f32))).reshape(B, T, SSD_GROUPS, -1)
    yg = yg * lax.rsqrt(jnp.mean(yg * yg, axis=-1, keepdims=True) + RMS_EPS)
    y = yg.reshape(B, T, SSD_D_INNER) * norm_w.astype(f32)
    return y.astype(h.dtype) @ w_out, new_state.astype(h.dtype), new_buf


def mem_kv(mem, g, wk, wv):
    m = rmsnorm(mem, g)
    B, M, _ = m.shape
    return (m @ wk).reshape(B, M, XA_HEADS, XA_HEAD_DIM), (m @ wv).reshape(B, M, XA_HEADS, XA_HEAD_DIM)


def cross_attend(h, k, v, wq, wo):
    B, T, _ = h.shape
    q = (h @ wq).reshape(B, T, XA_HEADS, XA_HEAD_DIM)
    s = jnp.einsum('bqhd,bkhd->bhqk', q, k).astype(jnp.float32) * XA_SCALE
    p = jax.nn.softmax(s, axis=-1).astype(v.dtype)
    return jnp.einsum('bhqk,bkhd->bqhd', p, v).reshape(B, T, -1) @ wo


def swiglu(h, wg, wu, wd):
    return (jax.nn.silu(h @ wg) * (h @ wu)) @ wd


def setup_inputs(seed: int = 0) -> dict:
    key = jax.random.key(seed)
    ks = iter(jax.random.split(key, 64))
    f32 = jnp.float32

    def nrm(shape, scale=1.0):
        return scale * jax.random.normal(next(ks), shape, f32)

    def gain(shape):
        return 1.0 + nrm(shape, 0.02)

    n_pages = PAST_LEN // PAGE_SIZE
    n_used = DEC_BATCH * n_pages
    n_pool = n_used + n_used // 4
    page_table = jax.random.permutation(next(ks), n_pool)[:n_used].reshape(DEC_BATCH, n_pages).astype(jnp.int32)
    dt0 = jnp.exp(jax.random.uniform(next(ks), (N_C, SSD_HEADS), f32, math.log(1e-3), math.log(1e-1)))
    dt_bias = dt0 + jnp.log(-jnp.expm1(-dt0))
    A_log = jnp.log(jax.random.uniform(next(ks), (N_C, SSD_HEADS), f32, 1.0, 16.0))
    D = D_MODEL
    XW = XA_HEADS * XA_HEAD_DIM
    return {
        'x_prompt': nrm((BATCH, SEQ, D)),
        'x_sample': nrm((DEC_BATCH, DEC_SEQ, D)),
        'cache_mla_ckv': nrm((N_A, n_pool, PAGE_SIZE, MLA_KV_LORA)),
        'cache_mla_kpe': nrm((N_A, n_pool, PAGE_SIZE, MLA_ROPE)),
        'cache_diff_k': nrm((N_B, n_pool, PAGE_SIZE, DIFF_KV_HEADS, 2, DIFF_HEAD_DIM)),
        'cache_diff_v': nrm((N_B, n_pool, PAGE_SIZE, DIFF_KV_HEADS, 2 * DIFF_HEAD_DIM)),
        'state_ssm': nrm((N_C, DEC_BATCH, SSD_HEADS, SSD_HEAD_DIM, SSD_STATE), 0.5),
        'state_conv': nrm((N_C, DEC_BATCH, SSD_CONV - 1, SSD_CONV_DIM)),
        'cache_mem_k': nrm((DEPTH, DEC_BATCH, MEM_TOKENS, XA_HEADS, XA_HEAD_DIM)),
        'cache_mem_v': nrm((DEPTH, DEC_BATCH, MEM_TOKENS, XA_HEADS, XA_HEAD_DIM)),
        'page_table': page_table,
        'mem_prompt': nrm((BATCH, MEM_TOKENS, D)),
        'norm_mix': gain((DEPTH, D)),
        'norm_xa': gain((DEPTH, D)),
        'norm_ffn': gain((DEPTH, D)),
        'norm_final': gain((D,)),
        'xa_mem_norm': gain((DEPTH, D)),
        'xa_wq': nrm((DEPTH, D, XW), D ** -0.5),
        'xa_wk': nrm((DEPTH, D, XW), D ** -0.5),
        'xa_wv': nrm((DEPTH, D, XW), D ** -0.5),
        'xa_wo': nrm((DEPTH, XW, D), XW ** -0.5),
        'ffn_w_gate': nrm((DEPTH, D, FFN_HIDDEN), D ** -0.5),
        'ffn_w_up': nrm((DEPTH, D, FFN_HIDDEN), D ** -0.5),
        'ffn_w_down': nrm((DEPTH, FFN_HIDDEN, D), FFN_HIDDEN ** -0.5),
        'mla_wq_a': nrm((N_A, D, MLA_Q_LORA), D ** -0.5),
        'mla_q_norm': gain((N_A, MLA_Q_LORA)),
        'mla_wq_b': nrm((N_A, MLA_Q_LORA, MLA_HEADS * (MLA_NOPE + MLA_ROPE)), MLA_Q_LORA ** -0.5),
        'mla_wkv_a': nrm((N_A, D, MLA_KV_LORA + MLA_ROPE), D ** -0.5),
        'mla_kv_norm': gain((N_A, MLA_KV_LORA)),
        'mla_w_uk': nrm((N_A, MLA_KV_LORA, MLA_HEADS, MLA_NOPE), MLA_KV_LORA ** -0.5),
        'mla_w_uv': nrm((N_A, MLA_KV_LORA, MLA_HEADS, MLA_V), MLA_KV_LORA ** -0.5),
        'mla_wo': nrm((N_A, MLA_HEADS * MLA_V, D), (MLA_HEADS * MLA_V) ** -0.5),
        'diff_wq': nrm((N_B, D, DIFF_HEADS * 2 * DIFF_HEAD_DIM), D ** -0.5),
        'diff_wk': nrm((N_B, D, DIFF_KV_HEADS * 2 * DIFF_HEAD_DIM), D ** -0.5),
        'diff_wv': nrm((N_B, D, DIFF_KV_HEADS * 2 * DIFF_HEAD_DIM), D ** -0.5),
        'diff_lambda_q1': nrm((N_B, DIFF_HEAD_DIM), 0.1),
        'diff_lambda_k1': nrm((N_B, DIFF_HEAD_DIM), 0.1),
        'diff_lambda_q2': nrm((N_B, DIFF_HEAD_DIM), 0.1),
        'diff_lambda_k2': nrm((N_B, DIFF_HEAD_DIM), 0.1),
        'diff_subln': gain((N_B, 2 * DIFF_HEAD_DIM)),
        'diff_wo': nrm((N_B, DIFF_HEADS * 2 * DIFF_HEAD_DIM, D), (DIFF_HEADS * 2 * DIFF_HEAD_DIM) ** -0.5),
        'ssd_w_in': nrm((N_C, D, SSD_IN_DIM), D ** -0.5),
        'ssd_conv_w': nrm((N_C, SSD_CONV, SSD_CONV_DIM), SSD_CONV ** -0.5),
        'ssd_conv_b': nrm((N_C, SSD_CONV_DIM), 0.02),
        'ssd_dt_bias': dt_bias,
        'ssd_A_log': A_log,
        'ssd_D': 1.0 + nrm((N_C, SSD_HEADS), 0.1),
        'ssd_norm': gain((N_C, SSD_D_INNER)),
        'ssd_w_out': nrm((N_C, SSD_D_INNER, D), SSD_D_INNER ** -0.5),
    }


def reference(x_prompt, x_sample, cache_mla_ckv, cache_mla_kpe, cache_diff_k, cache_diff_v, state_ssm, state_conv,
              cache_mem_k, cache_mem_v, page_table, mem_prompt,
              norm_mix, norm_xa, norm_ffn, norm_final, xa_mem_norm, xa_wq, xa_wk, xa_wv, xa_wo,
              ffn_w_gate, ffn_w_up, ffn_w_down,
              mla_wq_a, mla_q_norm, mla_wq_b, mla_wkv_a, mla_kv_norm, mla_w_uk, mla_w_uv, mla_wo,
              diff_wq, diff_wk, diff_wv, diff_lambda_q1, diff_lambda_k1, diff_lambda_q2, diff_lambda_k2, diff_subln, diff_wo,
              ssd_w_in, ssd_conv_w, ssd_conv_b, ssd_dt_bias, ssd_A_log, ssd_D, ssd_norm, ssd_w_out):
    pos_p = jnp.arange(x_prompt.shape[1], dtype=jnp.int32)
    pos_s = PAST_LEN + jnp.arange(x_sample.shape[1], dtype=jnp.int32)

    def run(x, pos, prompt):
        new = {k: [] for k in ('mla_ckv', 'mla_kpe', 'diff_k', 'diff_v', 'ssm', 'conv', 'mem_k', 'mem_v')}
        for i in range(DEPTH):
            kind, j = i % N_MIXERS, i // N_MIXERS
            h = rmsnorm(x, norm_mix[i])
            if kind == 0:
                q_lat, q_pe, ckv, kpe = mla_project(h, pos, mla_wq_a[j], mla_q_norm[j], mla_wq_b[j],
                                                    mla_wkv_a[j], mla_kv_norm[j], mla_w_uk[j])
                if prompt:
                    o_lat = map_query_blocks(lambda ql, qp, qpos: mla_attend(ql, qp, [(ckv, kpe, pos)], qpos),
                                             (q_lat, q_pe), pos)
                else:
                    c_past = gather_pages(cache_mla_ckv, j, page_table)
                    r_past = gather_pages(cache_mla_kpe, j, page_table)
                    kpos_past = jnp.arange(c_past.shape[1], dtype=jnp.int32)
                    o_lat = mla_attend(q_lat, q_pe, [(c_past, r_past, kpos_past), (ckv, kpe, pos)], pos)
                x = x + mla_output(o_lat, mla_w_uv[j], mla_wo[j])
                new['mla_ckv'].append(ckv)
                new['mla_kpe'].append(kpe)
            elif kind == 1:
                q, k, v = diff_project(h, pos, diff_wq[j], diff_wk[j], diff_wv[j])
                lam_init = 0.8 - 0.6 * math.exp(-0.3 * i)
                lam = diff_lambda(diff_lambda_q1[j], diff_lambda_k1[j], diff_lambda_q2[j], diff_lambda_k2[j], lam_init)
                if prompt:
                    o = map_query_blocks(lambda qb, qpos: diff_attend(qb, [(k, v, pos)], lam, qpos), (q,), pos)
                else:
                    k_past = gather_pages(cache_diff_k, j, page_table)
                    v_past = gather_pages(cache_diff_v, j, page_table)
                    kpos_past = jnp.arange(k_past.shape[1], dtype=jnp.int32)
                    o = diff_attend(q, [(k_past, v_past, kpos_past), (k, v, pos)], lam, pos)
                x = x + diff_output(o, diff_subln[j], diff_wo[j], lam_init)
                new['diff_k'].append(k)
                new['diff_v'].append(v)
            else:
                if prompt:
                    buf = jnp.zeros((x.shape[0], SSD_CONV - 1, SSD_CONV_DIM), x.dtype)
                    st = None
                else:
                    buf, st = state_conv[j], state_ssm[j]
                out, new_st, new_buf = ssd_mixer(h, buf, st, ssd_w_in[j], ssd_conv_w[j], ssd_conv_b[j], ssd_dt_bias[j],
                                                 ssd_A_log[j], ssd_D[j], ssd_norm[j], ssd_w_out[j], prompt)
                x = x + out
                new['ssm'].append(new_st)
                new['conv'].append(new_buf)
            if prompt:
                mk, mv = mem_kv(mem_prompt, xa_mem_norm[i], xa_wk[i], xa_wv[i])
                new['mem_k'].append(mk)
                new['mem_v'].append(mv)
            else:
                mk, mv = cache_mem_k[i], cache_mem_v[i]
            x = x + cross_attend(rmsnorm(x, norm_xa[i]), mk, mv, xa_wq[i], xa_wo[i])
            x = x + swiglu(rmsnorm(x, norm_ffn[i]), ffn_w_gate[i], ffn_w_up[i], ffn_w_down[i])
        return rmsnorm(x, norm_final), {k: jnp.stack(v) for k, v in new.items() if v}

    y_prompt, new_p = run(x_prompt, pos_p, True)
    y_sample, new_s = run(x_sample, pos_s, False)
    return (y_prompt, y_sample,
            new_p['mla_ckv'], new_p['mla_kpe'], new_p['diff_k'], new_p['diff_v'], new_p['ssm'], new_p['conv'],
            new_p['mem_k'], new_p['mem_v'],
            new_s['mla_ckv'], new_s['mla_kpe'], new_s['diff_k'], new_s['diff_v'], new_s['ssm'], new_s['conv'])
```

```python
import functools
import math

import jax
import jax.numpy as jnp
from jax import lax
from jax.experimental import pallas as pl
from jax.experimental.pallas import tpu as pltpu

F32 = jnp.float32
BF16 = jnp.bfloat16

RMS_EPS = 1e-6
ROPE_THETA = 10000.0
NEG_INF = -1e30
PAST_LEN = 8192
PAGE_SIZE = 128

MLA_HEADS = 8
MLA_NOPE = 128
MLA_ROPE = 64
MLA_KV_LORA = 256
MLA_V = 128
MLA_QK = 384
MLA_SCALE = (MLA_NOPE + MLA_ROPE) ** -0.5

DIFF_HEADS = 8
DIFF_GROUPS = 4
DIFF_HEAD_DIM = 64
DIFF_SCALE = DIFF_HEAD_DIM ** -0.5

SSD_D_INNER = 2048
SSD_HEADS = 32
SSD_HEAD_DIM = 64
SSD_GROUPS = 4
SSD_STATE = 128
SSD_CONV = 4
SSD_CONV_DIM = SSD_D_INNER + 2 * SSD_GROUPS * SSD_STATE
SSD_CHUNK = 128

XA_HEADS = 4
XA_HEAD_DIM = 256
XA_SCALE = XA_HEAD_DIM ** -0.5

LANES = 128
VMEM_LIMIT = 48 * 1024 * 1024

NT_DIMS = (((1,), (1,)), ((), ()))
TN_DIMS = (((0,), (0,)), ((), ()))


def _params(sem):
    return pltpu.CompilerParams(dimension_semantics=sem, vmem_limit_bytes=VMEM_LIMIT)


def _rms(x, g):
    return x * lax.rsqrt(jnp.mean(x * x, axis=-1, keepdims=True) + RMS_EPS) * g


def _silu(x):
    return x * jax.nn.sigmoid(x)


def _dot(a, b):
    return jnp.dot(a, b, preferred_element_type=F32)


def _dot_nt(a, b):
    return lax.dot_general(a, b, NT_DIMS, preferred_element_type=F32)


def _dot_exact(a, b, dims=(((1,), (0,)), ((), ()))):
    return lax.dot_general(a, b, dims, precision=lax.Precision.HIGHEST, preferred_element_type=F32)


def _linear_kernel(*refs, has_norm, has_res):
    refs = list(refs)
    x_ref = refs.pop(0)
    g_ref = refs.pop(0) if has_norm else None
    w_ref = refs.pop(0)
    r_ref = refs.pop(0) if has_res else None
    o_ref, h_scr = refs

    @pl.when(pl.program_id(1) == 0)
    def _():
        x = x_ref[...].astype(F32)
        if has_norm:
            x = _rms(x, g_ref[...])
        h_scr[...] = x.astype(BF16)

    acc = _dot(h_scr[...], w_ref[...])
    if has_res:
        acc = acc + r_ref[...]
    o_ref[...] = acc.astype(o_ref.dtype)


def _pick_tn(n):
    for tn in (512, 384, 256, 128):
        if n % tn == 0:
            return tn
    return n


def _linear(x, w, gain=None, res=None, out_dtype=F32, x_col=0, name="linear"):
    m = x.shape[0]
    k, n = w.shape
    tm = min(m, 512)
    tn = _pick_tn(n)
    in_specs = [pl.BlockSpec((tm, k), lambda i, j: (i, x_col))]
    args = [x]
    if gain is not None:
        in_specs.append(pl.BlockSpec((1, k), lambda i, j: (0, 0)))
        args.append(gain.reshape(1, k).astype(F32))
    in_specs.append(pl.BlockSpec((k, tn), lambda i, j: (0, j)))
    args.append(w)
    if res is not None:
        in_specs.append(pl.BlockSpec((tm, tn), lambda i, j: (i, j)))
        args.append(res)
    return pl.pallas_call(
        functools.partial(_linear_kernel, has_norm=gain is not None, has_res=res is not None),
        out_shape=jax.ShapeDtypeStruct((m, n), out_dtype),
        grid=(m // tm, n // tn),
        in_specs=in_specs,
        out_specs=pl.BlockSpec((tm, tn), lambda i, j: (i, j)),
        scratch_shapes=[pltpu.VMEM((tm, k), BF16)],
        compiler_params=_params(("parallel", "arbitrary")),
        name=name,
    )(*args)


def _ffn_kernel(x_ref, g_ref, wg_ref, wu_ref, wd_ref, o_ref, h_scr, acc_scr):
    c = pl.program_id(1)

    @pl.when(c == 0)
    def _():
        h_scr[...] = _rms(x_ref[...], g_ref[...]).astype(BF16)
        acc_scr[...] = jnp.zeros_like(acc_scr)

    h = h_scr[...]
    a = _silu(_dot(h, wg_ref[...])) * _dot(h, wu_ref[...])
    acc_scr[...] += _dot(a.astype(BF16), wd_ref[...])

    @pl.when(c == pl.num_programs(1) - 1)
    def _():
        o_ref[...] = x_ref[...] + acc_scr[...]


def _ffn(x, gain, wg, wu, wd):
    m, d = x.shape
    hid = wg.shape[1]
    tm = min(m, 1024)
    tc = 256
    return pl.pallas_call(
        _ffn_kernel,
        out_shape=jax.ShapeDtypeStruct((m, d), F32),
        grid=(m // tm, hid // tc),
        in_specs=[
            pl.BlockSpec((tm, d), lambda i, c: (i, 0)),
            pl.BlockSpec((1, d), lambda i, c: (0, 0)),
            pl.BlockSpec((d, tc), lambda i, c: (0, c)),
            pl.BlockSpec((d, tc), lambda i, c: (0, c)),
            pl.BlockSpec((tc, d), lambda i, c: (c, 0)),
        ],
        out_specs=pl.BlockSpec((tm, d), lambda i, c: (i, 0)),
        scratch_shapes=[pltpu.VMEM((tm, d), BF16), pltpu.VMEM((tm, d), F32)],
        compiler_params=_params(("parallel", "arbitrary")),
        name="ffn",
    )(x, gain.reshape(1, d), wg, wu, wd)


def _norm_kernel(x_ref, g_ref, o_ref):
    o_ref[...] = _rms(x_ref[...], g_ref[...])


def _final_norm(x, gain):
    m, d = x.shape
    tm = min(m, 1024)
    return pl.pallas_call(
        _norm_kernel,
        out_shape=jax.ShapeDtypeStruct((m, d), F32),
        grid=(m // tm,),
        in_specs=[pl.BlockSpec((tm, d), lambda i: (i, 0)), pl.BlockSpec((1, d), lambda i: (0, 0))],
        out_specs=pl.BlockSpec((tm, d), lambda i: (i, 0)),
        compiler_params=_params(("parallel",)),
        name="final_norm",
    )(x, gain.reshape(1, d))


def _rope(x, cos, sin):
    n = x.shape[1]
    reps = n // LANES
    if reps > 1:
        cos = jnp.concatenate([cos] * reps, axis=1)
        sin = jnp.concatenate([sin] * reps, axis=1)
    lane = lax.broadcasted_iota(jnp.int32, x.shape, 1)
    first_half = (lane % 64) < 32
    partner = jnp.where(first_half, pltpu.roll(x, n - 32, 1), pltpu.roll(x, 32, 1))
    return x * cos + partner * sin


def _rope_tables(pos):
    inv = ROPE_THETA ** (-jnp.arange(32, dtype=F32) * 2.0 / 64)
    ang = pos.astype(F32)[:, None] * inv[None, :]
    c, s = jnp.cos(ang), jnp.sin(ang)
    return jnp.concatenate([c, c, c, c], axis=1), jnp.concatenate([-s, s, -s, s], axis=1)


def _mla_prep_kernel(q_ref, kv_ref, cos_ref, sin_ref, wuk_ref, kvg_ref,
                     qcat_ref, kcat_ref, ckv_ref, kpe_ref):
    cos, sin = cos_ref[...], sin_ref[...]
    q = q_ref[...] * MLA_SCALE
    q_pe = _rope(q[:, 1024:], cos, sin).astype(BF16)
    q_nope = q[:, :1024].astype(BF16)
    for h in range(MLA_HEADS):
        q_lat = _dot(q_nope[:, h * 128:(h + 1) * 128], wuk_ref[h])
        qcat_ref[h, :, 0:256] = q_lat.astype(BF16)
        qcat_ref[h, :, 256:384] = q_pe[:, h * 128:(h + 1) * 128]
    kv = kv_ref[...]
    ckv = _rms(kv[:, :256], kvg_ref[...])
    kpe = _rope(kv[:, 256:], cos, sin)
    ckv_ref[...] = ckv
    kpe_ref[...] = kpe[:, :64]
    kcat_ref[:, 0:256] = ckv.astype(BF16)
    kcat_ref[:, 256:384] = kpe.astype(BF16)


def _mla_prep(q, qkv_a, cos, sin, wuk, kv_gain):
    m = q.shape[0]
    tm = min(m, 512)
    nt = cos.shape[0] // tm
    return pl.pallas_call(
        _mla_prep_kernel,
        out_shape=(
            jax.ShapeDtypeStruct((MLA_HEADS, m, MLA_QK), BF16),
            jax.ShapeDtypeStruct((m, MLA_QK), BF16),
            jax.ShapeDtypeStruct((m, MLA_KV_LORA), F32),
            jax.ShapeDtypeStruct((m, MLA_ROPE), F32),
        ),
        grid=(m // tm,),
        in_specs=[
            pl.BlockSpec((tm, 2048), lambda i: (i, 0)),
            pl.BlockSpec((tm, MLA_QK), lambda i: (i, 1)),
            pl.BlockSpec((tm, LANES), lambda i: (i % nt, 0)),
            pl.BlockSpec((tm, LANES), lambda i: (i % nt, 0)),
            pl.BlockSpec((MLA_HEADS, 128, 256), lambda i: (0, 0, 0)),
            pl.BlockSpec((1, 256), lambda i: (0, 0)),
        ],
        out_specs=(
            pl.BlockSpec((MLA_HEADS, tm, MLA_QK), lambda i: (0, i, 0)),
            pl.BlockSpec((tm, MLA_QK), lambda i: (i, 0)),
            pl.BlockSpec((tm, MLA_KV_LORA), lambda i: (i, 0)),
            pl.BlockSpec((tm, MLA_ROPE), lambda i: (i, 0)),
        ),
        compiler_params=_params(("parallel",)),
        name="mla_prep",
    )(q, qkv_a, cos, sin, wuk, kv_gain.reshape(1, 256))


def _flash_mla_kernel(q_ref, k_ref, o_ref, m_scr, l_scr, acc_scr, *, tq, tk):
    qi, ki = pl.program_id(1), pl.program_id(2)
    last = ((qi + 1) * tq - 1) // tk

    @pl.when(ki == 0)
    def _():
        m_scr[...] = jnp.full_like(m_scr, NEG_INF)
        l_scr[...] = jnp.zeros_like(l_scr)
        acc_scr[...] = jnp.zeros_like(acc_scr)

    @pl.when(ki <= last)
    def _():
        q = q_ref[...].reshape(MLA_HEADS * tq, MLA_QK)
        k = k_ref[...]
        s = _dot_nt(q, k).reshape(MLA_HEADS, tq, tk)
        qpos = qi * tq + lax.broadcasted_iota(jnp.int32, (tq, tk), 0)
        kpos = ki * tk + lax.broadcasted_iota(jnp.int32, (tq, tk), 1)
        s = jnp.where((kpos <= qpos)[None], s, NEG_INF).reshape(MLA_HEADS * tq, tk)
        m_old = m_scr[...]
        m_new = jnp.maximum(m_old, jnp.max(s, axis=-1, keepdims=True))
        alpha = jnp.exp(m_old - m_new)
        p = jnp.exp(s - m_new)
        l_scr[...] = alpha * l_scr[...] + jnp.sum(p, axis=-1, keepdims=True)
        acc_scr[...] = alpha * acc_scr[...] + _dot(p.astype(BF16), k[:, :MLA_KV_LORA])
        m_scr[...] = m_new

    @pl.when(ki == last)
    def _():
        o = acc_scr[...] / l_scr[...]
        for h in range(MLA_HEADS):
            o_ref[:, h * 256:(h + 1) * 256] = o[h * tq:(h + 1) * tq].astype(o_ref.dtype)


def _flash_mla(qcat, kcat, batch, seq, tq=256, tk=512):
    nq, nk = seq // tq, seq // tk

    def k_map(b, qi, ki):
        return (b * nk + jnp.minimum(ki, ((qi + 1) * tq - 1) // tk), 0)

    return pl.pallas_call(
        functools.partial(_flash_mla_kernel, tq=tq, tk=tk),
        out_shape=jax.ShapeDtypeStruct((batch * seq, MLA_HEADS * MLA_KV_LORA), BF16),
        grid=(batch, nq, nk),
        in_specs=[
            pl.BlockSpec((MLA_HEADS, tq, MLA_QK), lambda b, qi, ki: (0, b * nq + qi, 0)),
            pl.BlockSpec((tk, MLA_QK), k_map),
        ],
        out_specs=pl.BlockSpec((tq, MLA_HEADS * MLA_KV_LORA), lambda b, qi, ki: (b * nq + qi, 0)),
        scratch_shapes=[
            pltpu.VMEM((MLA_HEADS * tq, 1), F32),
            pltpu.VMEM((MLA_HEADS * tq, 1), F32),
            pltpu.VMEM((MLA_HEADS * tq, MLA_KV_LORA), F32),
        ],
        compiler_params=_params(("parallel", "parallel", "arbitrary")),
        name="flash_mla",
    )(qcat, kcat)


def _mla_out_kernel(o_ref, wuv_ref, wo_ref, x_ref, y_ref):
    o = o_ref[...].astype(BF16)
    heads = [_dot(o[:, h * 256:(h + 1) * 256], wuv_ref[h]).astype(BF16) for h in range(MLA_HEADS)]
    y_ref[...] = x_ref[...] + _dot(jnp.concatenate(heads, axis=1), wo_ref[...])


def _mla_out(o_lat, wuv, wo, x):
    m, d = x.shape
    tm = min(m, 512)
    return pl.pallas_call(
        _mla_out_kernel,
        out_shape=jax.ShapeDtypeStruct((m, d), F32),
        grid=(m // tm,),
        in_specs=[
            pl.BlockSpec((tm, 2048), lambda i: (i, 0)),
            pl.BlockSpec((MLA_HEADS, 256, 128), lambda i: (0, 0, 0)),
            pl.BlockSpec((1024, d), lambda i: (0, 0)),
            pl.BlockSpec((tm, d), lambda i: (i, 0)),
        ],
        out_specs=pl.BlockSpec((tm, d), lambda i: (i, 0)),
        compiler_params=_params(("parallel",)),
        name="mla_out",
    )(o_lat, wuv, wo, x)


def _decode_mla_kernel(pt_ref, q_ref, kn_ref, ckv_hbm, kpe_hbm, o_ref,
                       cbuf, pbuf, kb_scr, pb_scr, sem, *, layer, n_pages):
    b = pl.program_id(0)
    slot = b % 2

    def copies(tok, sl, p):
        page = pt_ref[tok * n_pages + p]
        return (pltpu.make_async_copy(ckv_hbm.at[layer, page], cbuf.at[sl, p], sem.at[0, sl]),
                pltpu.make_async_copy(kpe_hbm.at[layer, page], pbuf.at[sl, p], sem.at[1, sl]))

    def issue(tok, sl):
        def body(p, carry):
            for cp in copies(tok, sl, p):
                cp.start()
            return carry
        lax.fori_loop(0, n_pages, body, 0)

    @pl.when(b == 0)
    def _():
        issue(0, 0)

    @pl.when(b + 1 < pl.num_programs(0))
    def _():
        issue(b + 1, 1 - slot)

    def wait_body(p, carry):
        for cp in copies(b, slot, p):
            cp.wait()
        return carry
    lax.fori_loop(0, n_pages, wait_body, 0)

    n_keys = n_pages * PAGE_SIZE
    kb_scr[...] = cbuf[slot].reshape(n_keys, MLA_KV_LORA).astype(BF16)
    pb_scr[...] = pbuf[slot].reshape(n_keys, MLA_ROPE).astype(BF16)

    q = q_ref[0]
    kn = kn_ref[0].astype(F32)
    s_self = jnp.sum(q.astype(F32) * kn, axis=-1, keepdims=True)
    s = _dot_nt(q[:, :256], kb_scr[...]) + _dot_nt(q[:, 256:320], pb_scr[...])
    m = jnp.maximum(jnp.max(s, axis=-1, keepdims=True), s_self)
    p = jnp.exp(s - m)
    p_self = jnp.exp(s_self - m)
    l = jnp.sum(p, axis=-1, keepdims=True) + p_self
    acc = _dot(p.astype(BF16), kb_scr[...]) + p_self * kn[:, :256]
    o_ref[0] = acc / l


def _decode_mla(page_table, q_tok, k_new, cache_ckv, cache_kpe, layer):
    nb, n_pages = page_table.shape
    n_keys = n_pages * PAGE_SIZE
    return pl.pallas_call(
        functools.partial(_decode_mla_kernel, layer=layer, n_pages=n_pages),
        out_shape=jax.ShapeDtypeStruct((nb, MLA_HEADS, MLA_KV_LORA), F32),
        grid_spec=pltpu.PrefetchScalarGridSpec(
            num_scalar_prefetch=1,
            grid=(nb,),
            in_specs=[
                pl.BlockSpec((1, MLA_HEADS, MLA_QK), lambda b, pt: (b, 0, 0)),
                pl.BlockSpec((1, 1, MLA_QK), lambda b, pt: (b, 0, 0)),
                pl.BlockSpec(memory_space=pl.ANY),
                pl.BlockSpec(memory_space=pl.ANY),
            ],
            out_specs=pl.BlockSpec((1, MLA_HEADS, MLA_KV_LORA), lambda b, pt: (b, 0, 0)),
            scratch_shapes=[
                pltpu.VMEM((2, n_pages, PAGE_SIZE, MLA_KV_LORA), F32),
                pltpu.VMEM((2, n_pages, PAGE_SIZE, MLA_ROPE), F32),
                pltpu.VMEM((n_keys, MLA_KV_LORA), BF16),
                pltpu.VMEM((n_keys, MLA_ROPE), BF16),
                pltpu.SemaphoreType.DMA((2, 2)),
            ],
        ),
        compiler_params=_params(("arbitrary",)),
        name="decode_mla",
    )(page_table.reshape(-1), q_tok, k_new.reshape(nb, 1, MLA_QK), cache_ckv, cache_kpe)


def _diff_prep_kernel(qkv_ref, cos_ref, sin_ref, q_ref, k32_ref, kb_ref, v32_ref, vb_ref):
    qkv = qkv_ref[...]
    qk = _rope(qkv[:, :1536], cos_ref[...], sin_ref[...])
    q_ref[...] = (qk[:, :1024] * DIFF_SCALE).astype(BF16)
    k = qk[:, 1024:]
    v = qkv[:, 1536:]
    k32_ref[...] = k
    kb_ref[...] = k.astype(BF16)
    v32_ref[...] = v
    vb_ref[...] = v.astype(BF16)


def _diff_prep(qkv, cos, sin):
    m = qkv.shape[0]
    tm = min(m, 512)
    nt = cos.shape[0] // tm
    row = lambda i: (i, 0)
    return pl.pallas_call(
        _diff_prep_kernel,
        out_shape=(
            jax.ShapeDtypeStruct((m, 1024), BF16),
            jax.ShapeDtypeStruct((m, 512), F32),
            jax.ShapeDtypeStruct((m, 512), BF16),
            jax.ShapeDtypeStruct((m, 512), F32),
            jax.ShapeDtypeStruct((m, 512), BF16),
        ),
        grid=(m // tm,),
        in_specs=[
            pl.BlockSpec((tm, 2048), row),
            pl.BlockSpec((tm, LANES), lambda i: (i % nt, 0)),
            pl.BlockSpec((tm, LANES), lambda i: (i % nt, 0)),
        ],
        out_specs=(
            pl.BlockSpec((tm, 1024), row),
            pl.BlockSpec((tm, 512), row),
            pl.BlockSpec((tm, 512), row),
            pl.BlockSpec((tm, 512), row),
            pl.BlockSpec((tm, 512), row),
        ),
        compiler_params=_params(("parallel",)),
        name="diff_prep",
    )(qkv, cos, sin)


def _diff_lambda(lq1_ref, lk1_ref, lq2_ref, lk2_ref, lam_init):
    e1 = jnp.exp(jnp.sum(lq1_ref[...] * lk1_ref[...], axis=-1, keepdims=True))
    e2 = jnp.exp(jnp.sum(lq2_ref[...] * lk2_ref[...], axis=-1, keepdims=True))
    return e1 - e2 + lam_init


def _flash_diff_kernel(lq1_ref, lk1_ref, lq2_ref, lk2_ref, sub_ref, q_ref, k_ref, v_ref, o_ref,
                       m_scr, l_scr, acc_scr, *, tq, tk, lam_init):
    qi, ki = pl.program_id(2), pl.program_id(3)
    last = ((qi + 1) * tq - 1) // tk

    @pl.when(ki == 0)
    def _():
        m_scr[...] = jnp.full_like(m_scr, NEG_INF)
        l_scr[...] = jnp.zeros_like(l_scr)
        acc_scr[...] = jnp.zeros_like(acc_scr)

    @pl.when(ki <= last)
    def _():
        q = q_ref[...]
        comp0 = lax.broadcasted_iota(jnp.int32, (tq, LANES), 1) < DIFF_HEAD_DIM
        zero = jnp.zeros((tq, LANES), BF16)
        r0, r1 = q[:, :LANES], q[:, LANES:]
        qs = jnp.concatenate([jnp.where(comp0, r0, zero), jnp.where(comp0, zero, r0),
                              jnp.where(comp0, r1, zero), jnp.where(comp0, zero, r1)], axis=0)
        s = _dot_nt(qs, k_ref[...]).reshape(4, tq, tk)
        qpos = qi * tq + lax.broadcasted_iota(jnp.int32, (tq, tk), 0)
        kpos = ki * tk + lax.broadcasted_iota(jnp.int32, (tq, tk), 1)
        s = jnp.where((kpos <= qpos)[None], s, NEG_INF).reshape(4 * tq, tk)
        m_old = m_scr[...]
        m_new = jnp.maximum(m_old, jnp.max(s, axis=-1, keepdims=True))
        alpha = jnp.exp(m_old - m_new)
        p = jnp.exp(s - m_new)
        l_scr[...] = alpha * l_scr[...] + jnp.sum(p, axis=-1, keepdims=True)
        acc_scr[...] = alpha * acc_scr[...] + _dot(p.astype(BF16), v_ref[...])
        m_scr[...] = m_new

    @pl.when(ki == last)
    def _():
        lam = _diff_lambda(lq1_ref, lk1_ref, lq2_ref, lk2_ref, lam_init)
        o = (acc_scr[...] / l_scr[...]).reshape(4, tq, LANES)
        g = sub_ref[...]
        d0 = _rms(o[0] - lam * o[1], g) * (1.0 - lam_init)
        d1 = _rms(o[2] - lam * o[3], g) * (1.0 - lam_init)
        o_ref[:, :LANES] = d0.astype(o_ref.dtype)
        o_ref[:, LANES:] = d1.astype(o_ref.dtype)


def _flash_diff(q, k, v, lams, subln, lam_init, batch, seq, tq=256, tk=512):
    nq, nk = seq // tq, seq // tk

    def kv_map(b, g, qi, ki):
        return (b * nk + jnp.minimum(ki, ((qi + 1) * tq - 1) // tk), g)

    vec = pl.BlockSpec((1, DIFF_HEAD_DIM), lambda b, g, qi, ki: (0, 0))
    return pl.pallas_call(
        functools.partial(_flash_diff_kernel, tq=tq, tk=tk, lam_init=lam_init),
        out_shape=jax.ShapeDtypeStruct((batch * seq, 1024), BF16),
        grid=(batch, DIFF_GROUPS, nq, nk),
        in_specs=[
            vec, vec, vec, vec,
            pl.BlockSpec((1, LANES), lambda b, g, qi, ki: (0, 0)),
            pl.BlockSpec((tq, 256), lambda b, g, qi, ki: (b * nq + qi, g)),
            pl.BlockSpec((tk, LANES), kv_map),
            pl.BlockSpec((tk, LANES), kv_map),
        ],
        out_specs=pl.BlockSpec((tq, 256), lambda b, g, qi, ki: (b * nq + qi, g)),
        scratch_shapes=[
            pltpu.VMEM((4 * tq, 1), F32),
            pltpu.VMEM((4 * tq, 1), F32),
            pltpu.VMEM((4 * tq, LANES), F32),
        ],
        compiler_params=_params(("parallel", "parallel", "parallel", "arbitrary")),
        name="flash_diff",
    )(*[a.reshape(1, DIFF_HEAD_DIM) for a in lams], subln.reshape(1, LANES), q, k, v)


def _decode_diff_kernel(pt_ref, lq1_ref, lk1_ref, lq2_ref, lk2_ref, sub_ref, q_ref, kn_ref, vn_ref,
                        k_hbm, v_hbm, o_ref, kbuf, vbuf, sem, m_scr, l_scr, acc_scr,
                        *, layer, n_pages, unit, lam_init):
    b, u = pl.program_id(0), pl.program_id(1)
    n_units = n_pages // unit
    step = b * n_units + u
    slot = step % 2

    def copies(st, sl, p):
        tok, un = st // n_units, st % n_units
        page = pt_ref[tok * n_pages + un * unit + p]
        return (pltpu.make_async_copy(k_hbm.at[layer, page], kbuf.at[sl, p], sem.at[0, sl]),
                pltpu.make_async_copy(v_hbm.at[layer, page], vbuf.at[sl, p], sem.at[1, sl]))

    def issue(st, sl):
        def body(p, carry):
            for cp in copies(st, sl, p):
                cp.start()
            return carry
        lax.fori_loop(0, unit, body, 0)

    @pl.when(step == 0)
    def _():
        issue(0, 0)

    @pl.when(step + 1 < pl.num_programs(0) * n_units)
    def _():
        issue(step + 1, 1 - slot)

    def wait_body(p, carry):
        for cp in copies(step, slot, p):
            cp.wait()
        return carry
    lax.fori_loop(0, unit, wait_body, 0)

    q = q_ref[0]

    @pl.when(u == 0)
    def _():
        m_scr[...] = jnp.sum(q.astype(F32) * kn_ref[0].astype(F32), axis=-1, keepdims=True)
        l_scr[...] = jnp.ones_like(l_scr)
        acc_scr[...] = jnp.broadcast_to(vn_ref[0].astype(F32), acc_scr.shape)

    n_keys = unit * PAGE_SIZE
    kc = kbuf[slot].reshape(n_keys, 512).astype(BF16)
    vc = vbuf[slot].reshape(n_keys, 512).astype(BF16)
    s = _dot_nt(q, kc)
    m_old = m_scr[...]
    m_new = jnp.maximum(m_old, jnp.max(s, axis=-1, keepdims=True))
    alpha = jnp.exp(m_old - m_new)
    p = jnp.exp(s - m_new)
    l_scr[...] = alpha * l_scr[...] + jnp.sum(p, axis=-1, keepdims=True)
    acc_scr[...] = alpha * acc_scr[...] + _dot(p.astype(BF16), vc)
    m_scr[...] = m_new

    @pl.when(u == n_units - 1)
    def _():
        lam = _diff_lambda(lq1_ref, lk1_ref, lq2_ref, lk2_ref, lam_init)
        on = acc_scr[...] / l_scr[...]
        d = on[0:8] - lam * on[8:16]
        row_g = lax.broadcasted_iota(jnp.int32, (8, 512), 0) % DIFF_GROUPS
        lane_g = lax.broadcasted_iota(jnp.int32, (8, 512), 1) // LANES
        d = jnp.where(row_g == lane_g, d, 0.0)
        d = d[:, 0:128] + d[:, 128:256] + d[:, 256:384] + d[:, 384:512]
        o_ref[0] = _rms(d, sub_ref[...]) * (1.0 - lam_init)


def _decode_diff(page_table, q_rows, k_new, v_new, cache_k, cache_v, lams, subln, lam_init, layer, unit=16):
    nb, n_pages = page_table.shape
    vec = pl.BlockSpec((1, DIFF_HEAD_DIM), lambda b, u, pt: (0, 0))
    return pl.pallas_call(
        functools.partial(_decode_diff_kernel, layer=layer, n_pages=n_pages, unit=unit, lam_init=lam_init),
        out_shape=jax.ShapeDtypeStruct((nb, 8, LANES), F32),
        grid_spec=pltpu.PrefetchScalarGridSpec(
            num_scalar_prefetch=1,
            grid=(nb, n_pages // unit),
            in_specs=[
                vec, vec, vec, vec,
                pl.BlockSpec((1, LANES), lambda b, u, pt: (0, 0)),
                pl.BlockSpec((1, 16, 512), lambda b, u, pt: (b, 0, 0)),
                pl.BlockSpec((1, 1, 512), lambda b, u, pt: (b, 0, 0)),
                pl.BlockSpec((1, 1, 512), lambda b, u, pt: (b, 0, 0)),
                pl.BlockSpec(memory_space=pl.ANY),
                pl.BlockSpec(memory_space=pl.ANY),
            ],
            out_specs=pl.BlockSpec((1, 8, LANES), lambda b, u, pt: (b, 0, 0)),
            scratch_shapes=[
                pltpu.VMEM((2, unit, PAGE_SIZE, 512), F32),
                pltpu.VMEM((2, unit, PAGE_SIZE, 512), F32),
                pltpu.SemaphoreType.DMA((2, 2)),
                pltpu.VMEM((16, 1), F32),
                pltpu.VMEM((16, 1), F32),
                pltpu.VMEM((16, 512), F32),
            ],
        ),
        compiler_params=_params(("arbitrary", "arbitrary")),
        name="decode_diff",
    )(page_table.reshape(-1), *[a.reshape(1, DIFF_HEAD_DIM) for a in lams], subln.reshape(1, LANES),
      q_rows, k_new.reshape(nb, 1, 512), v_new.reshape(nb, 1, 512), cache_k, cache_v)


def _ssd_gate_norm(y, xs, z, d_ref, nw_ref):
    yg = (y + d_ref[...] * xs) * _silu(z)
    w = SSD_D_INNER // SSD_GROUPS
    parts = []
    for g in range(SSD_GROUPS):
        part = yg[:, g * w:(g + 1) * w]
        parts.append(part * lax.rsqrt(jnp.mean(part * part, axis=-1, keepdims=True) + RMS_EPS))
    return jnp.concatenate(parts, axis=1) * nw_ref[...]


def _ssd_prompt_kernel(z_ref, xbc_ref, dt_ref, cw_ref, cb_ref, dtb_ref, a_ref, e_ref, d_ref, nw_ref,
                       yn_ref, st_ref, buf_ref, xp_scr, s_scr, *, chunk):
    c = pl.program_id(1)

    @pl.when(c == 0)
    def _():
        xp_scr[0:8, :] = jnp.zeros((8, SSD_CONV_DIM), F32)
        s_scr[...] = jnp.zeros_like(s_scr)

    xp_scr[8:8 + chunk, :] = xbc_ref[...]
    conv = cb_ref[...] + sum(xp_scr[5 + k:5 + k + chunk, :] * cw_ref[k:k + 1, :] for k in range(SSD_CONV))
    u = _silu(conv)
    xs = u[:, :SSD_D_INNER]
    bm = u[:, SSD_D_INNER:SSD_D_INNER + 512]
    cm = u[:, SSD_D_INNER + 512:]

    dt = jax.nn.softplus(dt_ref[...] + dtb_ref[...])
    a = dt * a_ref[...]
    row = lax.broadcasted_iota(jnp.int32, (chunk, chunk), 0)
    col = lax.broadcasted_iota(jnp.int32, (chunk, chunk), 1)
    causal = col <= row
    a_cs = _dot_exact(causal.astype(F32), a)
    a_cs_t = a_cs.T
    a_last = a_cs[chunk - 1:chunk, :]
    ex = _dot_exact(jnp.concatenate([dt, jnp.exp(a_cs), jnp.exp(a_last - a_cs)], axis=0), e_ref[...])
    dtx, ecs, dend = ex[:chunk], ex[chunk:2 * chunk], ex[2 * chunk:]
    xdt = xs * dtx
    xdt_b = xdt.astype(BF16)
    xdtd_b = (xdt * dend).astype(BF16)
    etot = ecs[chunk - 1:chunk, :]
    low_head = lax.broadcasted_iota(jnp.int32, (chunk, LANES), 1) < SSD_HEAD_DIM

    ys = []
    for g in range(SSD_GROUPS):
        bg = bm[:, g * 128:(g + 1) * 128]
        cg = cm[:, g * 128:(g + 1) * 128].astype(BF16)
        cb = _dot_nt(cg, bg.astype(BF16))
        bg_t = bg.T.astype(BF16)
        for pi in range(4):
            pair = g * 4 + pi
            sl = slice(pair * LANES, (pair + 1) * LANES)
            ms = []
            for h in (2 * pair, 2 * pair + 1):
                seg = a_cs[:, h:h + 1] - a_cs_t[h:h + 1, :]
                ms.append((cb * jnp.exp(jnp.where(causal, seg, NEG_INF))).astype(BF16))
            y_diag = jnp.where(low_head, _dot(ms[0], xdt_b[:, sl]), _dot(ms[1], xdt_b[:, sl]))
            st = s_scr[:, sl]
            y_off = _dot(cg, st.astype(BF16)) * ecs[:, sl]
            s_scr[:, sl] = etot[:, sl] * st + _dot(bg_t, xdtd_b[:, sl])
            ys.append(y_diag + y_off)

    y = jnp.concatenate(ys, axis=1)
    yn_ref[...] = _ssd_gate_norm(y, xs, z_ref[...], d_ref, nw_ref).astype(yn_ref.dtype)
    xp_scr[0:8, :] = xp_scr[chunk:chunk + 8, :]

    @pl.when(c == pl.num_programs(1) - 1)
    def _():
        st_ref[0] = s_scr[...].T
        buf_ref[0] = xp_scr[chunk + 5:chunk + 8, :]


def _ssd_prompt(z, xbc, dt, cw, cb, dtb, a_neg, expand, d_exp, nw, batch, seq, chunk=SSD_CHUNK):
    nc = seq // chunk
    row = lambda b, c: (b * nc + c, 0)
    const = lambda b, c: (0, 0)
    return pl.pallas_call(
        functools.partial(_ssd_prompt_kernel, chunk=chunk),
        out_shape=(
            jax.ShapeDtypeStruct((batch * seq, SSD_D_INNER), BF16),
            jax.ShapeDtypeStruct((batch, SSD_D_INNER, SSD_STATE), F32),
            jax.ShapeDtypeStruct((batch, SSD_CONV - 1, SSD_CONV_DIM), F32),
        ),
        grid=(batch, nc),
        in_specs=[
            pl.BlockSpec((chunk, SSD_D_INNER), row),
            pl.BlockSpec((chunk, SSD_CONV_DIM), row),
            pl.BlockSpec((chunk, LANES), row),
            pl.BlockSpec((SSD_CONV, SSD_CONV_DIM), const),
            pl.BlockSpec((1, SSD_CONV_DIM), const),
            pl.BlockSpec((1, LANES), const),
            pl.BlockSpec((1, LANES), const),
            pl.BlockSpec((LANES, SSD_D_INNER), const),
            pl.BlockSpec((1, SSD_D_INNER), const),
            pl.BlockSpec((1, SSD_D_INNER), const),
        ],
        out_specs=(
            pl.BlockSpec((chunk, SSD_D_INNER), row),
            pl.BlockSpec((1, SSD_D_INNER, SSD_STATE), lambda b, c: (b, 0, 0)),
            pl.BlockSpec((1, SSD_CONV - 1, SSD_CONV_DIM), lambda b, c: (b, 0, 0)),
        ),
        scratch_shapes=[
            pltpu.VMEM((chunk + 8, SSD_CONV_DIM), F32),
            pltpu.VMEM((SSD_STATE, SSD_D_INNER), F32),
        ],
        compiler_params=_params(("parallel", "arbitrary")),
        name="ssd_prompt",
    )(z, xbc, dt, cw, cb, dtb, a_neg, expand, d_exp, nw)


def _ssd_decode_pre_kernel(xbc_ref, dt_ref, buf_ref, cw_ref, cb_ref, dtb_ref, a_ref, e_ref,
                           xs_ref, xdt_ref, dec_ref, b_ref, c_ref, nbuf_ref):
    xbc = xbc_ref[...]
    conv = cb_ref[...] + xbc * cw_ref[3:4, :]
    for k in range(SSD_CONV - 1):
        conv = conv + buf_ref[k] * cw_ref[k:k + 1, :]
    u = _silu(conv)
    xs = u[:, :SSD_D_INNER]
    dt = jax.nn.softplus(dt_ref[...] + dtb_ref[...])
    ex = _dot_exact(jnp.concatenate([dt, dt * a_ref[...]], axis=0), e_ref[...])
    t = xbc.shape[0]
    xs_ref[...] = xs
    xdt_ref[...] = xs * ex[:t]
    dec_ref[...] = jnp.exp(ex[t:])
    b_ref[...] = u[:, SSD_D_INNER:SSD_D_INNER + 512]
    c_ref[...] = u[:, SSD_D_INNER + 512:]
    nbuf_ref[0] = buf_ref[1]
    nbuf_ref[1] = buf_ref[2]
    nbuf_ref[2] = xbc


def _ssd_decode_pre(xbc, dt, buf, cw, cb, dtb, a_neg, expand):
    t = xbc.shape[0]
    wide = jax.ShapeDtypeStruct((t, SSD_D_INNER), F32)
    grp = jax.ShapeDtypeStruct((t, 512), F32)
    return pl.pallas_call(
        _ssd_decode_pre_kernel,
        out_shape=(wide, wide, wide, grp, grp, jax.ShapeDtypeStruct((SSD_CONV - 1, t, SSD_CONV_DIM), F32)),
        compiler_params=pltpu.CompilerParams(vmem_limit_bytes=VMEM_LIMIT),
        name="ssd_decode_pre",
    )(xbc, dt, buf, cw, cb, dtb, a_neg, expand)


def _ssd_decode_state_kernel(xdt_ref, dec_ref, b_ref, c_ref, st_ref, y_ref, nst_ref):
    xdt = xdt_ref[0]
    lane_g = lax.broadcasted_iota(jnp.int32, (8, SSD_D_INNER), 1) // 512
    row = lax.broadcasted_iota(jnp.int32, (8, SSD_D_INNER), 0)
    lhs = jnp.where(row == lane_g, xdt, 0.0) + jnp.where(row == 4, dec_ref[0], 0.0)
    brow = lax.broadcasted_iota(jnp.int32, (8, LANES), 0)
    bmat = jnp.zeros((8, LANES), F32)
    cmat = jnp.zeros((8, LANES), F32)
    for g in range(SSD_GROUPS):
        bmat = jnp.where(brow == g, b_ref[0][:, g * 128:(g + 1) * 128], bmat)
        cmat = jnp.where(brow == g, c_ref[0][:, g * 128:(g + 1) * 128], cmat)
    rhs = jnp.concatenate([bmat, (brow == 4).astype(F32)], axis=1)
    both = _dot_exact(lhs, rhs, TN_DIMS)
    new = st_ref[0] * both[:, LANES:] + both[:, :LANES]
    nst_ref[0] = new
    yg = _dot_nt(cmat.astype(BF16), new.astype(BF16))
    y_ref[0] = jnp.sum(jnp.where(row == lane_g, yg, 0.0), axis=0, keepdims=True)


def _ssd_decode_state(xdt, dec, bm, cm, state):
    t = xdt.shape[0]
    tok3 = lambda w: pl.BlockSpec((1, 1, w), lambda i: (i, 0, 0))
    st_spec = pl.BlockSpec((1, SSD_D_INNER, SSD_STATE), lambda i: (i, 0, 0))
    return pl.pallas_call(
        _ssd_decode_state_kernel,
        out_shape=(jax.ShapeDtypeStruct((t, 1, SSD_D_INNER), F32),
                   jax.ShapeDtypeStruct((t, SSD_D_INNER, SSD_STATE), F32)),
        grid=(t,),
        in_specs=[tok3(SSD_D_INNER), tok3(SSD_D_INNER), tok3(512), tok3(512), st_spec],
        out_specs=(tok3(SSD_D_INNER), st_spec),
        compiler_params=_params(("parallel",)),
        name="ssd_decode_state",
    )(xdt.reshape(t, 1, -1), dec.reshape(t, 1, -1), bm.reshape(t, 1, -1), cm.reshape(t, 1, -1), state)


def _ssd_decode_post_kernel(y_ref, xs_ref, z_ref, d_ref, nw_ref, o_ref):
    o_ref[...] = _ssd_gate_norm(y_ref[...], xs_ref[...], z_ref[...], d_ref, nw_ref).astype(o_ref.dtype)


def _ssd_decode_post(y, xs, z, d_exp, nw):
    return pl.pallas_call(
        _ssd_decode_post_kernel,
        out_shape=jax.ShapeDtypeStruct(y.shape, BF16),
        compiler_params=pltpu.CompilerParams(vmem_limit_bytes=VMEM_LIMIT),
        name="ssd_decode_post",
    )(y, xs, z, d_exp, nw)


def _xa_prompt_kernel(x_ref, g_ref, wq_ref, k_ref, v_ref, wo_ref, o_ref):
    x = x_ref[...]
    q = _dot(_rms(x, g_ref[...]).astype(BF16), wq_ref[...]).astype(BF16)
    k = k_ref[0].astype(BF16)
    v = v_ref[0].astype(BF16)
    heads = []
    for h in range(XA_HEADS):
        sl = slice(h * XA_HEAD_DIM, (h + 1) * XA_HEAD_DIM)
        s = _dot_nt(q[:, sl], k[:, sl]) * XA_SCALE
        p = jnp.exp(s - jnp.max(s, axis=-1, keepdims=True))
        o = _dot(p.astype(BF16), v[:, sl]) / jnp.sum(p, axis=-1, keepdims=True)
        heads.append(o.astype(BF16))
    o_ref[...] = x + _dot(jnp.concatenate(heads, axis=1), wo_ref[...])


def _xa_prompt(x, gain, wq, kv_mem, wo, batch, seq):
    m, d = x.shape
    tm = 512
    nt = seq // tm
    mem = kv_mem.shape[1]
    return pl.pallas_call(
        _xa_prompt_kernel,
        out_shape=jax.ShapeDtypeStruct((m, d), F32),
        grid=(m // tm,),
        in_specs=[
            pl.BlockSpec((tm, d), lambda i: (i, 0)),
            pl.BlockSpec((1, d), lambda i: (0, 0)),
            pl.BlockSpec((d, d), lambda i: (0, 0)),
            pl.BlockSpec((1, mem, d), lambda i: (i // nt, 0, 0)),
            pl.BlockSpec((1, mem, d), lambda i: (i // nt, 0, 1)),
            pl.BlockSpec((d, d), lambda i: (0, 0)),
        ],
        out_specs=pl.BlockSpec((tm, d), lambda i: (i, 0)),
        compiler_params=_params(("parallel",)),
        name="xa_prompt",
    )(x, gain.reshape(1, d), wq, kv_mem, kv_mem, wo)


def _xa_decode_kernel(q_ref, k_ref, v_ref, o_ref, *, tokens):
    d = q_ref.shape[2]
    row = lax.broadcasted_iota(jnp.int32, (8, d), 0)
    lane_h = lax.broadcasted_iota(jnp.int32, (8, d), 1) // XA_HEAD_DIM
    own = row == lane_h
    for t in range(tokens):
        q = jnp.where(own, q_ref[0, t:t + 1, :], 0.0).astype(BF16)
        s = _dot_nt(q, k_ref[0, t].astype(BF16)) * XA_SCALE
        p = jnp.exp(s - jnp.max(s, axis=-1, keepdims=True))
        o = _dot(p.astype(BF16), v_ref[0, t].astype(BF16)) / jnp.sum(p, axis=-1, keepdims=True)
        o_ref[0, t:t + 1, :] = jnp.sum(jnp.where(own, o, 0.0), axis=0, keepdims=True)


def _xa_decode(q, mem_k, mem_v, layer, tokens=4):
    t, d = q.shape
    mem = mem_k.shape[2]
    kv_spec = pl.BlockSpec((1, tokens, mem, d), lambda i: (layer, i, 0, 0))
    return pl.pallas_call(
        functools.partial(_xa_decode_kernel, tokens=tokens),
        out_shape=jax.ShapeDtypeStruct((t // tokens, tokens, d), F32),
        grid=(t // tokens,),
        in_specs=[pl.BlockSpec((1, tokens, d), lambda i: (i, 0, 0)), kv_spec, kv_spec],
        out_specs=pl.BlockSpec((1, tokens, d), lambda i: (i, 0, 0)),
        compiler_params=_params(("parallel",)),
        name="xa_decode",
    )(q.reshape(t // tokens, tokens, d), mem_k, mem_v).reshape(t, d)


def kernel(x_prompt, x_sample, cache_mla_ckv, cache_mla_kpe, cache_diff_k, cache_diff_v, state_ssm, state_conv, cache_mem_k, cache_mem_v, page_table, mem_prompt, norm_mix, norm_xa, norm_ffn, norm_final, xa_mem_norm, xa_wq, xa_wk, xa_wv, xa_wo, ffn_w_gate, ffn_w_up, ffn_w_down, mla_wq_a, mla_q_norm, mla_wq_b, mla_wkv_a, mla_kv_norm, mla_w_uk, mla_w_uv, mla_wo, diff_wq, diff_wk, diff_wv, diff_lambda_q1, diff_lambda_k1, diff_lambda_q2, diff_lambda_k2, diff_subln, diff_wo, ssd_w_in, ssd_conv_w, ssd_conv_b, ssd_dt_bias, ssd_A_log, ssd_D, ssd_norm, ssd_w_out):
    batch, seq, d = x_prompt.shape
    nb = x_sample.shape[0]
    depth = norm_mix.shape[0]
    n_mix = 3
    bf = lambda w: w.astype(BF16)

    n_a = mla_wq_a.shape[0]
    wkv_a = jnp.pad(mla_wkv_a, ((0, 0), (0, 0), (0, MLA_QK - mla_wkv_a.shape[2])))
    w_qkv_a = bf(jnp.concatenate([mla_wq_a, wkv_a], axis=2))
    wq_b = mla_wq_b.reshape(n_a, -1, MLA_HEADS, MLA_NOPE + MLA_ROPE)
    wq_b = bf(jnp.concatenate([
        wq_b[..., :MLA_NOPE].reshape(n_a, -1, MLA_HEADS * MLA_NOPE),
        jnp.pad(wq_b[..., MLA_NOPE:], ((0, 0), (0, 0), (0, 0), (0, 64))).reshape(n_a, -1, MLA_HEADS * 128),
    ], axis=2))
    w_uk = bf(jnp.transpose(mla_w_uk, (0, 2, 3, 1)))
    w_uv = bf(jnp.transpose(mla_w_uv, (0, 2, 1, 3)))
    w_mla_o = bf(mla_wo)
    w_diff_qkv = bf(jnp.concatenate([diff_wq, diff_wk, diff_wv], axis=2))
    w_diff_o = bf(diff_wo)
    w_ssd_z = bf(ssd_w_in[:, :, :SSD_D_INNER])
    w_ssd_xbc = bf(ssd_w_in[:, :, SSD_D_INNER:SSD_D_INNER + SSD_CONV_DIM])
    w_ssd_dt = bf(jnp.pad(ssd_w_in[:, :, SSD_D_INNER + SSD_CONV_DIM:], ((0, 0), (0, 0), (0, LANES - SSD_HEADS))))
    w_ssd_out = bf(ssd_w_out)
    pad_heads = lambda a: jnp.pad(a, ((0, 0), (0, LANES - SSD_HEADS)))
    ssd_dtb = pad_heads(ssd_dt_bias)
    ssd_a = pad_heads(-jnp.exp(ssd_A_log.astype(F32)))
    ssd_d_exp = jnp.repeat(ssd_D, SSD_HEAD_DIM, axis=1)
    expand = (jnp.arange(LANES)[:, None] == jnp.arange(SSD_D_INNER)[None, :] // SSD_HEAD_DIM).astype(F32)
    w_xa_q, w_xa_o = bf(xa_wq), bf(xa_wo)
    w_xa_kv = bf(jnp.concatenate([xa_wk, xa_wv], axis=2))
    w_gate, w_up, w_down = bf(ffn_w_gate), bf(ffn_w_up), bf(ffn_w_down)

    mem_k = cache_mem_k.reshape(depth, nb, cache_mem_k.shape[2], d)
    mem_v = cache_mem_v.reshape(depth, nb, cache_mem_v.shape[2], d)
    n_pool = cache_diff_k.shape[1]
    pool_k = cache_diff_k.reshape(-1, n_pool, PAGE_SIZE, 512)
    pool_v = cache_diff_v.reshape(-1, n_pool, PAGE_SIZE, 512)
    mem_flat = mem_prompt.reshape(-1, d)

    def run(x, pos_tab, prompt):
        m = x.shape[0]
        cos, sin = _rope_tables(pos_tab)
        new = {k: [] for k in ("mla_ckv", "mla_kpe", "diff_k", "diff_v", "ssm", "conv", "mem_k", "mem_v")}
        for i in range(depth):
            kind, j = i % n_mix, i // n_mix
            g_mix = norm_mix[i]
            if kind == 0:
                qkv_a = _linear(x, w_qkv_a[j], gain=g_mix, name="mla_qkv_a")
                q = _linear(qkv_a, wq_b[j], gain=mla_q_norm[j], name="mla_q_b")
                qcat, kcat, ckv, kpe = _mla_prep(q, qkv_a, cos, sin, w_uk[j], mla_kv_norm[j])
                if prompt:
                    o_lat = _flash_mla(qcat, kcat, batch, seq)
                else:
                    o_lat = _decode_mla(page_table, jnp.transpose(qcat, (1, 0, 2)), kcat,
                                        cache_mla_ckv, cache_mla_kpe, j).reshape(m, -1)
                x = _mla_out(o_lat, w_uv[j], w_mla_o[j], x)
                new["mla_ckv"].append(ckv)
                new["mla_kpe"].append(kpe)
            elif kind == 1:
                lam_init = 0.8 - 0.6 * math.exp(-0.3 * i)
                lams = (diff_lambda_q1[j], diff_lambda_k1[j], diff_lambda_q2[j], diff_lambda_k2[j])
                qkv = _linear(x, w_diff_qkv[j], gain=g_mix, name="diff_qkv")
                q_b, k32, k_b, v32, v_b = _diff_prep(qkv, cos, sin)
                if prompt:
                    o = _flash_diff(q_b, k_b, v_b, lams, diff_subln[j], lam_init, batch, seq)
                else:
                    q5 = q_b.reshape(m, DIFF_GROUPS, 2, 2, DIFF_HEAD_DIM)
                    eye_g = jnp.eye(DIFF_GROUPS, dtype=BF16)
                    eye_c = jnp.eye(2, dtype=BF16)
                    q_rows = jnp.einsum("tgrcd,gG,cC->tcrgGCd", q5, eye_g, eye_c).reshape(m, 16, 512)
                    o = _decode_diff(page_table, q_rows, k_b, v_b, pool_k, pool_v, lams, diff_subln[j],
                                     lam_init, j)
                    o = jnp.transpose(o.reshape(m, 2, DIFF_GROUPS, LANES), (0, 2, 1, 3)).reshape(m, -1)
                x = _linear(o, w_diff_o[j], res=x, name="diff_o")
                new["diff_k"].append(k32)
                new["diff_v"].append(v32)
            else:
                z = _linear(x, w_ssd_z[j], gain=g_mix, name="ssd_in_z")
                xbc = _linear(x, w_ssd_xbc[j], gain=g_mix, name="ssd_in_xbc")
                dt = _linear(x, w_ssd_dt[j], gain=g_mix, name="ssd_in_dt")
                cw, cb = ssd_conv_w[j], ssd_conv_b[j].reshape(1, -1)
                dtb, a_neg = ssd_dtb[j].reshape(1, -1), ssd_a[j].reshape(1, -1)
                d_exp, nw = ssd_d_exp[j].reshape(1, -1), ssd_norm[j].reshape(1, -1)
                if prompt:
                    yn, st, buf = _ssd_prompt(z, xbc, dt, cw, cb, dtb, a_neg, expand, d_exp, nw, batch, seq)
                else:
                    xs, xdt, dec, bm, cm, nbuf = _ssd_decode_pre(
                        xbc, dt, jnp.transpose(state_conv[j], (1, 0, 2)), cw, cb, dtb, a_neg, expand)
                    y, st = _ssd_decode_state(xdt, dec, bm, cm, state_ssm[j].reshape(m, SSD_D_INNER, SSD_STATE))
                    yn = _ssd_decode_post(y.reshape(m, -1), xs, z, d_exp, nw)
                    buf = jnp.transpose(nbuf, (1, 0, 2))
                x = _linear(yn, w_ssd_out[j], res=x, name="ssd_out")
                new["ssm"].append(st.reshape(-1, SSD_HEADS, SSD_HEAD_DIM, SSD_STATE))
                new["conv"].append(buf)
            if prompt:
                kv_mem = _linear(mem_flat, w_xa_kv[i], gain=xa_mem_norm[i], name="xa_mem_kv")
                new["mem_k"].append(kv_mem[:, :d])
                new["mem_v"].append(kv_mem[:, d:])
                x = _xa_prompt(x, norm_xa[i], w_xa_q[i], kv_mem.reshape(batch, -1, 2 * d), w_xa_o[i], batch, seq)
            else:
                q = _linear(x, w_xa_q[i], gain=norm_xa[i], name="xa_q")
                o = _xa_decode(q, mem_k, mem_v, i)
                x = _linear(o, w_xa_o[i], res=x, name="xa_o")
            x = _ffn(x, norm_ffn[i], w_gate[i], w_up[i], w_down[i])
        return _final_norm(x, norm_final), {k: jnp.stack(v) for k, v in new.items() if v}

    pos_p = jnp.arange(seq, dtype=jnp.int32)
    pos_s = jnp.full((nb,), PAST_LEN, jnp.int32)
    y_p, new_p = run(x_prompt.reshape(batch * seq, d), pos_p, True)
    y_s, new_s = run(x_sample.reshape(nb, d), pos_s, False)

    n_b = new_p["diff_k"].shape[0]
    n_c = new_p["ssm"].shape[0]
    mem_t = mem_prompt.shape[1]
    return (
        y_p.reshape(batch, seq, d),
        y_s.reshape(nb, 1, d),
        new_p["mla_ckv"].reshape(n_a, batch, seq, MLA_KV_LORA),
        new_p["mla_kpe"].reshape(n_a, batch, seq, MLA_ROPE),
        new_p["diff_k"].reshape(n_b, batch, seq, DIFF_GROUPS, 2, DIFF_HEAD_DIM),
        new_p["diff_v"].reshape(n_b, batch, seq, DIFF_GROUPS, 2 * DIFF_HEAD_DIM),
        new_p["ssm"].reshape(n_c, batch, SSD_HEADS, SSD_HEAD_DIM, SSD_STATE),
        new_p["conv"],
        new_p["mem_k"].reshape(depth, batch, mem_t, XA_HEADS, XA_HEAD_DIM),
        new_p["mem_v"].reshape(depth, batch, mem_t, XA_HEADS, XA_HEAD_DIM),
        new_s["mla_ckv"].reshape(n_a, nb, 1, MLA_KV_LORA),
        new_s["mla_kpe"].reshape(n_a, nb, 1, MLA_ROPE),
        new_s["diff_k"].reshape(n_b, nb, 1, DIFF_GROUPS, 2, DIFF_HEAD_DIM),
        new_s["diff_v"].reshape(n_b, nb, 1, DIFF_GROUPS, 2 * DIFF_HEAD_DIM),
        new_s["ssm"].reshape(n_c, nb, SSD_HEADS, SSD_HEAD_DIM, SSD_STATE),
        new_s["conv"],
    )
```

```python
import functools
import math

import jax
import jax.numpy as jnp
from jax import lax
from jax.experimental import pallas as pl
from jax.experimental.pallas import tpu as pltpu

F32 = jnp.float32
BF16 = jnp.bfloat16

RMS_EPS = 1e-6
ROPE_THETA = 10000.0
NEG_INF = -1e30
PAST_LEN = 8192
PAGE_SIZE = 128

MLA_HEADS = 8
MLA_NOPE = 128
MLA_ROPE = 64
MLA_KV_LORA = 256
MLA_V = 128
MLA_QK = 384
MLA_SCALE = (MLA_NOPE + MLA_ROPE) ** -0.5

DIFF_HEADS = 8
DIFF_GROUPS = 4
DIFF_HEAD_DIM = 64
DIFF_SCALE = DIFF_HEAD_DIM ** -0.5

SSD_D_INNER = 2048
SSD_HEADS = 32
SSD_HEAD_DIM = 64
SSD_GROUPS = 4
SSD_STATE = 128
SSD_CONV = 4
SSD_CONV_DIM = SSD_D_INNER + 2 * SSD_GROUPS * SSD_STATE
SSD_CHUNK = 128

XA_HEADS = 4
XA_HEAD_DIM = 256
XA_SCALE = XA_HEAD_DIM ** -0.5

LOG2E = math.log2(math.e)
SOFTMAX_ROWS = 32
LANES = 128
VMEM_LIMIT = 48 * 1024 * 1024

NT_DIMS = (((1,), (1,)), ((), ()))
TN_DIMS = (((0,), (0,)), ((), ()))


def _params(sem):
    return pltpu.CompilerParams(dimension_semantics=sem, vmem_limit_bytes=VMEM_LIMIT)


def _rms(x, g):
    return x * lax.rsqrt(jnp.mean(x * x, axis=-1, keepdims=True) + RMS_EPS) * g


def _silu(x):
    return x * jax.nn.sigmoid(x)


def _dot(a, b):
    return jnp.dot(a, b, preferred_element_type=F32)


def _dot_nt(a, b):
    return lax.dot_general(a, b, NT_DIMS, preferred_element_type=F32)


def _dot_exact(a, b, dims=(((1,), (0,)), ((), ()))):
    return lax.dot_general(a, b, dims, precision=lax.Precision.HIGHEST, preferred_element_type=F32)


def _linear_kernel(*refs, has_norm, has_res):
    refs = list(refs)
    x_ref = refs.pop(0)
    g_ref = refs.pop(0) if has_norm else None
    w_ref = refs.pop(0)
    r_ref = refs.pop(0) if has_res else None
    o_ref, h_scr = refs

    @pl.when(pl.program_id(1) == 0)
    def _():
        x = x_ref[...].astype(F32)
        if has_norm:
            x = _rms(x, g_ref[...])
        h_scr[...] = x.astype(BF16)

    acc = _dot(h_scr[...], w_ref[...])
    if has_res:
        acc = acc + r_ref[...]
    o_ref[...] = acc.astype(o_ref.dtype)


def _pick_tn(n):
    for tn in (512, 384, 256, 128):
        if n % tn == 0:
            return tn
    return n


def _linear(x, w, gain=None, res=None, out_dtype=F32, x_col=0, name="linear"):
    m = x.shape[0]
    k, n = w.shape
    tm = min(m, 512)
    tn = _pick_tn(n)
    in_specs = [pl.BlockSpec((tm, k), lambda i, j: (i, x_col))]
    args = [x]
    if gain is not None:
        in_specs.append(pl.BlockSpec((1, k), lambda i, j: (0, 0)))
        args.append(gain.reshape(1, k).astype(F32))
    in_specs.append(pl.BlockSpec((k, tn), lambda i, j: (0, j)))
    args.append(w)
    if res is not None:
        in_specs.append(pl.BlockSpec((tm, tn), lambda i, j: (i, j)))
        args.append(res)
    return pl.pallas_call(
        functools.partial(_linear_kernel, has_norm=gain is not None, has_res=res is not None),
        out_shape=jax.ShapeDtypeStruct((m, n), out_dtype),
        grid=(m // tm, n // tn),
        in_specs=in_specs,
        out_specs=pl.BlockSpec((tm, tn), lambda i, j: (i, j)),
        scratch_shapes=[pltpu.VMEM((tm, k), BF16)],
        compiler_params=_params(("parallel", "arbitrary")),
        name=name,
    )(*args)


def _ffn_kernel(x_ref, g_ref, wg_ref, wu_ref, wd_ref, o_ref, h_scr, acc_scr):
    c = pl.program_id(1)

    @pl.when(c == 0)
    def _():
        h_scr[...] = _rms(x_ref[...], g_ref[...]).astype(BF16)
        acc_scr[...] = jnp.zeros_like(acc_scr)

    h = h_scr[...]
    a = _silu(_dot(h, wg_ref[...])) * _dot(h, wu_ref[...])
    acc_scr[...] += _dot(a.astype(BF16), wd_ref[...])

    @pl.when(c == pl.num_programs(1) - 1)
    def _():
        o_ref[...] = x_ref[...] + acc_scr[...]


def _ffn(x, gain, wg, wu, wd):
    m, d = x.shape
    hid = wg.shape[1]
    tm = min(m, 1024)
    tc = 256
    return pl.pallas_call(
        _ffn_kernel,
        out_shape=jax.ShapeDtypeStruct((m, d), F32),
        grid=(m // tm, hid // tc),
        in_specs=[
            pl.BlockSpec((tm, d), lambda i, c: (i, 0)),
            pl.BlockSpec((1, d), lambda i, c: (0, 0)),
            pl.BlockSpec((d, tc), lambda i, c: (0, c)),
            pl.BlockSpec((d, tc), lambda i, c: (0, c)),
            pl.BlockSpec((tc, d), lambda i, c: (c, 0)),
        ],
        out_specs=pl.BlockSpec((tm, d), lambda i, c: (i, 0)),
        scratch_shapes=[pltpu.VMEM((tm, d), BF16), pltpu.VMEM((tm, d), F32)],
        compiler_params=_params(("parallel", "arbitrary")),
        name="ffn",
    )(x, gain.reshape(1, d), wg, wu, wd)


def _norm_kernel(x_ref, g_ref, o_ref):
    o_ref[...] = _rms(x_ref[...], g_ref[...])


def _final_norm(x, gain):
    m, d = x.shape
    tm = min(m, 1024)
    return pl.pallas_call(
        _norm_kernel,
        out_shape=jax.ShapeDtypeStruct((m, d), F32),
        grid=(m // tm,),
        in_specs=[pl.BlockSpec((tm, d), lambda i: (i, 0)), pl.BlockSpec((1, d), lambda i: (0, 0))],
        out_specs=pl.BlockSpec((tm, d), lambda i: (i, 0)),
        compiler_params=_params(("parallel",)),
        name="final_norm",
    )(x, gain.reshape(1, d))


def _rope(x, cos, sin):
    n = x.shape[1]
    reps = n // LANES
    if reps > 1:
        cos = jnp.concatenate([cos] * reps, axis=1)
        sin = jnp.concatenate([sin] * reps, axis=1)
    lane = lax.broadcasted_iota(jnp.int32, x.shape, 1)
    first_half = (lane % 64) < 32
    partner = jnp.where(first_half, pltpu.roll(x, n - 32, 1), pltpu.roll(x, 32, 1))
    return x * cos + partner * sin


def _rope_tables(pos):
    inv = ROPE_THETA ** (-jnp.arange(32, dtype=F32) * 2.0 / 64)
    ang = pos.astype(F32)[:, None] * inv[None, :]
    c, s = jnp.cos(ang), jnp.sin(ang)
    return jnp.concatenate([c, c, c, c], axis=1), jnp.concatenate([-s, s, -s, s], axis=1)


def _mla_prep_kernel(q_ref, kv_ref, cos_ref, sin_ref, wuk_ref, kvg_ref,
                     qcat_ref, kcat_ref, ckv_ref, kpe_ref):
    cos, sin = cos_ref[...], sin_ref[...]
    q = q_ref[...] * (MLA_SCALE * LOG2E)
    q_pe = _rope(q[:, 1024:], cos, sin).astype(BF16)
    q_nope = q[:, :1024].astype(BF16)
    for h in range(MLA_HEADS):
        q_lat = _dot(q_nope[:, h * 128:(h + 1) * 128], wuk_ref[h])
        qcat_ref[h, :, 0:256] = q_lat.astype(BF16)
        qcat_ref[h, :, 256:384] = q_pe[:, h * 128:(h + 1) * 128]
    kv = kv_ref[...]
    ckv = _rms(kv[:, :256], kvg_ref[...])
    kpe = _rope(kv[:, 256:], cos, sin)
    ckv_ref[...] = ckv
    kpe_ref[...] = kpe[:, :64]
    kcat_ref[:, 0:256] = ckv.astype(BF16)
    kcat_ref[:, 256:384] = kpe.astype(BF16)


def _mla_prep(q, qkv_a, cos, sin, wuk, kv_gain):
    m = q.shape[0]
    tm = min(m, 512)
    nt = cos.shape[0] // tm
    return pl.pallas_call(
        _mla_prep_kernel,
        out_shape=(
            jax.ShapeDtypeStruct((MLA_HEADS, m, MLA_QK), BF16),
            jax.ShapeDtypeStruct((m, MLA_QK), BF16),
            jax.ShapeDtypeStruct((m, MLA_KV_LORA), F32),
            jax.ShapeDtypeStruct((m, MLA_ROPE), F32),
        ),
        grid=(m // tm,),
        in_specs=[
            pl.BlockSpec((tm, 2048), lambda i: (i, 0)),
            pl.BlockSpec((tm, MLA_QK), lambda i: (i, 1)),
            pl.BlockSpec((tm, LANES), lambda i: (i % nt, 0)),
            pl.BlockSpec((tm, LANES), lambda i: (i % nt, 0)),
            pl.BlockSpec((MLA_HEADS, 128, 256), lambda i: (0, 0, 0)),
            pl.BlockSpec((1, 256), lambda i: (0, 0)),
        ],
        out_specs=(
            pl.BlockSpec((MLA_HEADS, tm, MLA_QK), lambda i: (0, i, 0)),
            pl.BlockSpec((tm, MLA_QK), lambda i: (i, 0)),
            pl.BlockSpec((tm, MLA_KV_LORA), lambda i: (i, 0)),
            pl.BlockSpec((tm, MLA_ROPE), lambda i: (i, 0)),
        ),
        compiler_params=_params(("parallel",)),
        name="mla_prep",
    )(q, qkv_a, cos, sin, wuk, kv_gain.reshape(1, 256))


def _tree(op, xs):
    xs = list(xs)
    while len(xs) > 1:
        xs = [op(xs[i], xs[i + 1]) if i + 1 < len(xs) else xs[i] for i in range(0, len(xs), 2)]
    return xs[0]


def _online_softmax(s, row0, m_scr, l_scr, acc_scr, masked):
    rows_n, tk = s.shape
    ch = SOFTMAX_ROWS
    acc_tiles = acc_scr.shape[1] // LANES
    out = []
    for c in range(rows_n // ch):
        rows = pl.ds(row0 + c * ch, ch)
        sc = s[c * ch:(c + 1) * ch, :]
        if masked:
            qpos = c * ch + lax.broadcasted_iota(jnp.int32, (ch, tk), 0)
            kpos = lax.broadcasted_iota(jnp.int32, (ch, tk), 1)
            sc = jnp.where(kpos <= qpos, sc, NEG_INF)
        tiles = [sc[:, t * LANES:(t + 1) * LANES] for t in range(tk // LANES)]
        m_prev = m_scr[rows, :]
        m_new = jnp.maximum(m_prev, jnp.max(_tree(jnp.maximum, tiles), axis=1, keepdims=True))
        alpha = jnp.exp2(m_prev - m_new)
        ps = [jnp.exp2(t - m_new) for t in tiles]
        if l_scr is not None:
            l_scr[rows, :] = alpha * l_scr[rows, :] + jnp.sum(_tree(jnp.add, ps), axis=1, keepdims=True)
        m_scr[rows, :] = m_new
        acc_scr[rows, :] = acc_scr[rows, :] * jnp.concatenate([alpha] * acc_tiles, axis=1)
        out.append(jnp.concatenate(ps, axis=1).astype(BF16))
    return jnp.concatenate(out, axis=0)


def _flash_mla_kernel(q_ref, k_ref, o_ref, m_scr, l_scr, acc_scr, *, tile):
    qi, ki = pl.program_id(1), pl.program_id(2)

    @pl.when(ki == 0)
    def _():
        m_scr[...] = jnp.full_like(m_scr, NEG_INF)
        l_scr[...] = jnp.zeros_like(l_scr)
        acc_scr[...] = jnp.zeros_like(acc_scr)

    def all_heads(masked):
        def head(h, carry):
            row0 = pl.multiple_of(h * tile, tile)
            k = k_ref[...]
            p = _online_softmax(_dot_nt(q_ref[h], k), row0, m_scr, l_scr, acc_scr, masked)
            acc_scr[pl.ds(row0, tile), :] += _dot(p, k[:, :MLA_KV_LORA])
            return carry
        lax.fori_loop(0, MLA_HEADS, head, 0, unroll=2)

    @pl.when(ki < qi)
    def _():
        all_heads(False)

    @pl.when(ki == qi)
    def _():
        all_heads(True)
        for h in range(MLA_HEADS):
            rows = slice(h * tile, (h + 1) * tile)
            l = l_scr[rows, :]
            o = acc_scr[rows, :] / jnp.concatenate([l, l], axis=1)
            o_ref[:, h * 256:(h + 1) * 256] = o.astype(o_ref.dtype)


def _flash_mla(qcat, kcat, batch, seq, tile=512):
    nt = seq // tile
    return pl.pallas_call(
        functools.partial(_flash_mla_kernel, tile=tile),
        out_shape=jax.ShapeDtypeStruct((batch * seq, MLA_HEADS * MLA_KV_LORA), BF16),
        grid=(batch, nt, nt),
        in_specs=[
            pl.BlockSpec((MLA_HEADS, tile, MLA_QK), lambda b, qi, ki: (0, b * nt + qi, 0)),
            pl.BlockSpec((tile, MLA_QK), lambda b, qi, ki: (b * nt + jnp.minimum(ki, qi), 0)),
        ],
        out_specs=pl.BlockSpec((tile, MLA_HEADS * MLA_KV_LORA), lambda b, qi, ki: (b * nt + qi, 0)),
        scratch_shapes=[
            pltpu.VMEM((MLA_HEADS * tile, LANES), F32),
            pltpu.VMEM((MLA_HEADS * tile, LANES), F32),
            pltpu.VMEM((MLA_HEADS * tile, MLA_KV_LORA), F32),
        ],
        compiler_params=_params(("parallel", "parallel", "arbitrary")),
        name="flash_mla",
    )(qcat, kcat)


def _mla_out_kernel(o_ref, wuv_ref, wo_ref, x_ref, y_ref):
    o = o_ref[...].astype(BF16)
    heads = [_dot(o[:, h * 256:(h + 1) * 256], wuv_ref[h]).astype(BF16) for h in range(MLA_HEADS)]
    y_ref[...] = x_ref[...] + _dot(jnp.concatenate(heads, axis=1), wo_ref[...])


def _mla_out(o_lat, wuv, wo, x):
    m, d = x.shape
    tm = min(m, 512)
    return pl.pallas_call(
        _mla_out_kernel,
        out_shape=jax.ShapeDtypeStruct((m, d), F32),
        grid=(m // tm,),
        in_specs=[
            pl.BlockSpec((tm, 2048), lambda i: (i, 0)),
            pl.BlockSpec((MLA_HEADS, 256, 128), lambda i: (0, 0, 0)),
            pl.BlockSpec((1024, d), lambda i: (0, 0)),
            pl.BlockSpec((tm, d), lambda i: (i, 0)),
        ],
        out_specs=pl.BlockSpec((tm, d), lambda i: (i, 0)),
        compiler_params=_params(("parallel",)),
        name="mla_out",
    )(o_lat, wuv, wo, x)


def _decode_mla_kernel(pt_ref, q_ref, kn_ref, ckv_hbm, kpe_hbm, o_ref,
                       cbuf, pbuf, kb_scr, sem, *, layer, n_pages):
    b = pl.program_id(0)
    slot = b % 2

    def copies(tok, sl, p):
        page = pt_ref[tok * n_pages + p]
        return (pltpu.make_async_copy(ckv_hbm.at[layer, page], cbuf.at[sl, p], sem.at[0, sl]),
                pltpu.make_async_copy(kpe_hbm.at[layer, page], pbuf.at[sl, p], sem.at[1, sl]))

    def issue(tok, sl):
        def body(p, carry):
            for cp in copies(tok, sl, p):
                cp.start()
            return carry
        lax.fori_loop(0, n_pages, body, 0)

    @pl.when(b == 0)
    def _():
        issue(0, 0)

    @pl.when(b + 1 < pl.num_programs(0))
    def _():
        issue(b + 1, 1 - slot)

    def wait_body(p, carry):
        for cp in copies(b, slot, p):
            cp.wait()
        return carry
    lax.fori_loop(0, n_pages, wait_body, 0)

    n_keys = n_pages * PAGE_SIZE
    kb_scr[...] = cbuf[slot].reshape(n_keys, MLA_KV_LORA).astype(BF16)

    q = q_ref[0]
    kn = kn_ref[0].astype(F32)
    s_self = jnp.sum(q.astype(F32) * kn, axis=-1, keepdims=True)
    q_pe = q[:, 256:320]
    s_pe = jnp.concatenate([_dot(q_pe, pbuf[slot, p].astype(BF16)) for p in range(n_pages)], axis=1)
    s = _dot_nt(q[:, :256], kb_scr[...]) + s_pe
    m = jnp.maximum(jnp.max(s, axis=-1, keepdims=True), s_self)
    p = jnp.exp2(s - m)
    p_self = jnp.exp2(s_self - m)
    l = jnp.sum(p, axis=-1, keepdims=True) + p_self
    acc = _dot(p.astype(BF16), kb_scr[...]) + p_self * kn[:, :256]
    o_ref[0] = acc / l


def _decode_mla(page_table, q_tok, k_new, cache_ckv, cache_kpe, layer):
    nb, n_pages = page_table.shape
    n_keys = n_pages * PAGE_SIZE
    return pl.pallas_call(
        functools.partial(_decode_mla_kernel, layer=layer, n_pages=n_pages),
        out_shape=jax.ShapeDtypeStruct((nb, MLA_HEADS, MLA_KV_LORA), F32),
        grid_spec=pltpu.PrefetchScalarGridSpec(
            num_scalar_prefetch=1,
            grid=(nb,),
            in_specs=[
                pl.BlockSpec((1, MLA_HEADS, MLA_QK), lambda b, pt: (b, 0, 0)),
                pl.BlockSpec((1, 1, MLA_QK), lambda b, pt: (b, 0, 0)),
                pl.BlockSpec(memory_space=pl.ANY),
                pl.BlockSpec(memory_space=pl.ANY),
            ],
            out_specs=pl.BlockSpec((1, MLA_HEADS, MLA_KV_LORA), lambda b, pt: (b, 0, 0)),
            scratch_shapes=[
                pltpu.VMEM((2, n_pages, PAGE_SIZE, MLA_KV_LORA), F32),
                pltpu.VMEM((2, n_pages, MLA_ROPE, PAGE_SIZE), F32),
                pltpu.VMEM((n_keys, MLA_KV_LORA), BF16),
                pltpu.SemaphoreType.DMA((2, 2)),
            ],
        ),
        compiler_params=_params(("arbitrary",)),
        name="decode_mla",
    )(page_table.reshape(-1), q_tok, k_new.reshape(nb, 1, MLA_QK), cache_ckv, cache_kpe)


def _diff_prep_kernel(qkv_ref, cos_ref, sin_ref, q_ref, k32_ref, kb_ref, v32_ref, vb_ref):
    qkv = qkv_ref[...]
    qk = _rope(qkv[:, :1536], cos_ref[...], sin_ref[...])
    q_ref[...] = (qk[:, :1024] * (DIFF_SCALE * LOG2E)).astype(BF16)
    k = qk[:, 1024:]
    v = qkv[:, 1536:]
    k32_ref[...] = k
    kb_ref[...] = k.astype(BF16)
    v32_ref[...] = v
    vb_ref[...] = v.astype(BF16)


def _diff_prep(qkv, cos, sin):
    m = qkv.shape[0]
    tm = min(m, 512)
    nt = cos.shape[0] // tm
    row = lambda i: (i, 0)
    return pl.pallas_call(
        _diff_prep_kernel,
        out_shape=(
            jax.ShapeDtypeStruct((m, 1024), BF16),
            jax.ShapeDtypeStruct((m, 512), F32),
            jax.ShapeDtypeStruct((m, 512), BF16),
            jax.ShapeDtypeStruct((m, 512), F32),
            jax.ShapeDtypeStruct((m, 512), BF16),
        ),
        grid=(m // tm,),
        in_specs=[
            pl.BlockSpec((tm, 2048), row),
            pl.BlockSpec((tm, LANES), lambda i: (i % nt, 0)),
            pl.BlockSpec((tm, LANES), lambda i: (i % nt, 0)),
        ],
        out_specs=(
            pl.BlockSpec((tm, 1024), row),
            pl.BlockSpec((tm, 512), row),
            pl.BlockSpec((tm, 512), row),
            pl.BlockSpec((tm, 512), row),
            pl.BlockSpec((tm, 512), row),
        ),
        compiler_params=_params(("parallel",)),
        name="diff_prep",
    )(qkv, cos, sin)


def _diff_lambda(lq1_ref, lk1_ref, lq2_ref, lk2_ref, lam_init):
    e1 = jnp.exp(jnp.sum(lq1_ref[...] * lk1_ref[...], axis=-1, keepdims=True))
    e2 = jnp.exp(jnp.sum(lq2_ref[...] * lk2_ref[...], axis=-1, keepdims=True))
    return e1 - e2 + lam_init


def _flash_diff_kernel(lq1_ref, lk1_ref, lq2_ref, lk2_ref, sub_ref, q_ref, k_ref, v_ref, o_ref,
                       m_scr, acc_scr, *, tile, lam_init):
    qi, ki = pl.program_id(2), pl.program_id(3)

    @pl.when(ki == 0)
    def _():
        m_scr[...] = jnp.full_like(m_scr, NEG_INF)
        acc_scr[...] = jnp.zeros_like(acc_scr)

    def all_rows(masked):
        q = q_ref[...]
        k = k_ref[...]
        v_ext = jnp.concatenate([v_ref[...], jnp.ones((tile, LANES), BF16)], axis=1)
        comp0 = lax.broadcasted_iota(jnp.int32, (tile, LANES), 1) < DIFF_HEAD_DIM
        zero = jnp.zeros((tile, LANES), BF16)
        for r in range(2):
            head = q[:, r * LANES:(r + 1) * LANES]
            for c in range(2):
                qc = jnp.where(comp0, head, zero) if c == 0 else jnp.where(comp0, zero, head)
                row0 = (2 * r + c) * tile
                p = _online_softmax(_dot_nt(qc, k), row0, m_scr, None, acc_scr, masked)
                acc_scr[row0:row0 + tile, :] += _dot(p, v_ext)

    @pl.when(ki < qi)
    def _():
        all_rows(False)

    @pl.when(ki == qi)
    def _():
        all_rows(True)
        lam = _diff_lambda(lq1_ref, lk1_ref, lq2_ref, lk2_ref, lam_init)
        g = sub_ref[...]
        for r in range(2):
            o = []
            for c in range(2):
                rows = slice((2 * r + c) * tile, (2 * r + c + 1) * tile)
                o.append(acc_scr[rows, :LANES] / acc_scr[rows, LANES:])
            d = _rms(o[0] - lam * o[1], g) * (1.0 - lam_init)
            o_ref[:, r * LANES:(r + 1) * LANES] = d.astype(o_ref.dtype)


def _flash_diff(q, k, v, lams, subln, lam_init, batch, seq, tile=512):
    nt = seq // tile

    def kv_map(b, g, qi, ki):
        return (b * nt + jnp.minimum(ki, qi), g)

    vec = pl.BlockSpec((1, DIFF_HEAD_DIM), lambda b, g, qi, ki: (0, 0))
    return pl.pallas_call(
        functools.partial(_flash_diff_kernel, tile=tile, lam_init=lam_init),
        out_shape=jax.ShapeDtypeStruct((batch * seq, 1024), BF16),
        grid=(batch, DIFF_GROUPS, nt, nt),
        in_specs=[
            vec, vec, vec, vec,
            pl.BlockSpec((1, LANES), lambda b, g, qi, ki: (0, 0)),
            pl.BlockSpec((tile, 256), lambda b, g, qi, ki: (b * nt + qi, g)),
            pl.BlockSpec((tile, LANES), kv_map),
            pl.BlockSpec((tile, LANES), kv_map),
        ],
        out_specs=pl.BlockSpec((tile, 256), lambda b, g, qi, ki: (b * nt + qi, g)),
        scratch_shapes=[
            pltpu.VMEM((4 * tile, LANES), F32),
            pltpu.VMEM((4 * tile, 2 * LANES), F32),
        ],
        compiler_params=_params(("parallel", "parallel", "parallel", "arbitrary")),
        name="flash_diff",
    )(*[a.reshape(1, DIFF_HEAD_DIM) for a in lams], subln.reshape(1, LANES), q, k, v)


def _decode_diff_kernel(pt_ref, lq1_ref, lk1_ref, lq2_ref, lk2_ref, sub_ref, q_ref, kn_ref, vn_ref, e_ref,
                        k_hbm, v_hbm, o_ref, kbuf, vbuf, sem, m_scr, l_scr, acc_scr,
                        *, layer, n_pages, unit, lam_init):
    b, u = pl.program_id(0), pl.program_id(1)
    n_units = n_pages // unit
    step = b * n_units + u
    slot = step % 2

    def copies(st, sl, p):
        tok, un = st // n_units, st % n_units
        page = pt_ref[tok * n_pages + un * unit + p]
        return (pltpu.make_async_copy(k_hbm.at[layer, page], kbuf.at[sl, p], sem.at[0, sl]),
                pltpu.make_async_copy(v_hbm.at[layer, page], vbuf.at[sl, p], sem.at[1, sl]))

    def issue(st, sl):
        def body(p, carry):
            for cp in copies(st, sl, p):
                cp.start()
            return carry
        lax.fori_loop(0, unit, body, 0)

    @pl.when(step == 0)
    def _():
        issue(0, 0)

    @pl.when(step + 1 < pl.num_programs(0) * n_units)
    def _():
        issue(step + 1, 1 - slot)

    def wait_body(p, carry):
        for cp in copies(step, slot, p):
            cp.wait()
        return carry
    lax.fori_loop(0, unit, wait_body, 0)

    q = q_ref[0]

    @pl.when(u == 0)
    def _():
        m_scr[...] = jnp.sum(q.astype(F32) * kn_ref[0].astype(F32), axis=-1, keepdims=True)
        l_scr[...] = jnp.ones_like(l_scr)
        acc_scr[...] = vn_ref[0]

    s = jnp.concatenate([_dot(q, kbuf[slot, pg].astype(BF16)) for pg in range(unit)], axis=1)
    m_old = m_scr[...]
    m_new = jnp.maximum(m_old, jnp.max(s, axis=-1, keepdims=True))
    alpha = jnp.exp2(m_old - m_new)
    p = jnp.exp2(s - m_new)
    l_scr[...] = alpha * l_scr[...] + jnp.sum(p, axis=-1, keepdims=True)
    m_scr[...] = m_new
    pb = p.astype(BF16)
    own = (lax.broadcasted_iota(jnp.int32, (16, 512), 0) % DIFF_GROUPS
           == lax.broadcasted_iota(jnp.int32, (16, 512), 1) % DIFF_GROUPS)
    pv = jnp.zeros((16, LANES), F32)
    for pg in range(unit):
        spread = _dot(pb[:, pg * PAGE_SIZE:(pg + 1) * PAGE_SIZE], e_ref[...])
        spread = jnp.where(own, spread, 0.0).astype(BF16)
        pv = pv + _dot(spread, vbuf[slot, pg].astype(BF16))
    acc_scr[...] = alpha * acc_scr[...] + pv

    @pl.when(u == n_units - 1)
    def _():
        lam = _diff_lambda(lq1_ref, lk1_ref, lq2_ref, lk2_ref, lam_init)
        on = acc_scr[...] / l_scr[...]
        d = on[0:8] - lam * on[8:16]
        o_ref[0] = _rms(d, sub_ref[...]) * (1.0 - lam_init)


def _decode_diff(page_table, q_rows, k_new, v_rows, cache_kt, cache_v, lams, subln, lam_init, layer, unit=16):
    nb, n_pages = page_table.shape
    spread = (jnp.arange(PAGE_SIZE)[:, None] == jnp.arange(512)[None, :] // DIFF_GROUPS).astype(BF16)
    vec = pl.BlockSpec((1, DIFF_HEAD_DIM), lambda b, u, pt: (0, 0))
    return pl.pallas_call(
        functools.partial(_decode_diff_kernel, layer=layer, n_pages=n_pages, unit=unit, lam_init=lam_init),
        out_shape=jax.ShapeDtypeStruct((nb, 8, LANES), F32),
        grid_spec=pltpu.PrefetchScalarGridSpec(
            num_scalar_prefetch=1,
            grid=(nb, n_pages // unit),
            in_specs=[
                vec, vec, vec, vec,
                pl.BlockSpec((1, LANES), lambda b, u, pt: (0, 0)),
                pl.BlockSpec((1, 16, 512), lambda b, u, pt: (b, 0, 0)),
                pl.BlockSpec((1, 1, 512), lambda b, u, pt: (b, 0, 0)),
                pl.BlockSpec((1, 16, LANES), lambda b, u, pt: (b, 0, 0)),
                pl.BlockSpec((PAGE_SIZE, 512), lambda b, u, pt: (0, 0)),
                pl.BlockSpec(memory_space=pl.ANY),
                pl.BlockSpec(memory_space=pl.ANY),
            ],
            out_specs=pl.BlockSpec((1, 8, LANES), lambda b, u, pt: (b, 0, 0)),
            scratch_shapes=[
                pltpu.VMEM((2, unit, 512, PAGE_SIZE), F32),
                pltpu.VMEM((2, unit, 512, LANES), F32),
                pltpu.SemaphoreType.DMA((2, 2)),
                pltpu.VMEM((16, 1), F32),
                pltpu.VMEM((16, 1), F32),
                pltpu.VMEM((16, LANES), F32),
            ],
        ),
        compiler_params=_params(("arbitrary", "arbitrary")),
        name="decode_diff",
    )(page_table.reshape(-1), *[a.reshape(1, DIFF_HEAD_DIM) for a in lams], subln.reshape(1, LANES),
      q_rows, k_new.reshape(nb, 1, 512), v_rows, spread, cache_kt, cache_v)


def _ssd_gate_norm(y, xs, z, d_ref, nw_ref):
    yg = (y + d_ref[...] * xs) * _silu(z)
    w = SSD_D_INNER // SSD_GROUPS
    parts = []
    for g in range(SSD_GROUPS):
        part = yg[:, g * w:(g + 1) * w]
        parts.append(part * lax.rsqrt(jnp.mean(part * part, axis=-1, keepdims=True) + RMS_EPS))
    return jnp.concatenate(parts, axis=1) * nw_ref[...]


def _ssd_prompt_kernel(z_ref, xbc_ref, dt_ref, cw_ref, cb_ref, dtb_ref, a_ref, e_ref, d_ref, nw_ref,
                       yn_ref, st_ref, buf_ref, xp_scr, s_scr, *, chunk):
    c = pl.program_id(1)

    @pl.when(c == 0)
    def _():
        xp_scr[0:8, :] = jnp.zeros((8, SSD_CONV_DIM), F32)
        s_scr[...] = jnp.zeros_like(s_scr)

    xp_scr[8:8 + chunk, :] = xbc_ref[...]
    conv = cb_ref[...] + sum(xp_scr[5 + k:5 + k + chunk, :] * cw_ref[k:k + 1, :] for k in range(SSD_CONV))
    u = _silu(conv)
    xs = u[:, :SSD_D_INNER]
    bm = u[:, SSD_D_INNER:SSD_D_INNER + 512]
    cm = u[:, SSD_D_INNER + 512:]

    dt = jax.nn.softplus(dt_ref[...] + dtb_ref[...])
    a = dt * a_ref[...]
    row = lax.broadcasted_iota(jnp.int32, (chunk, chunk), 0)
    col = lax.broadcasted_iota(jnp.int32, (chunk, chunk), 1)
    causal = col <= row
    a_cs = _dot_exact(causal.astype(F32), a)
    a_cs_t = a_cs.T
    a_last = a_cs[chunk - 1:chunk, :]
    ex = _dot_exact(jnp.concatenate([dt, jnp.exp(a_cs), jnp.exp(a_last - a_cs)], axis=0), e_ref[...])
    dtx, ecs, dend = ex[:chunk], ex[chunk:2 * chunk], ex[2 * chunk:]
    xdt = xs * dtx
    xdt_b = xdt.astype(BF16)
    xdtd_b = (xdt * dend).astype(BF16)
    etot = ecs[chunk - 1:chunk, :]
    low_head = lax.broadcasted_iota(jnp.int32, (chunk, LANES), 1) < SSD_HEAD_DIM

    ys = []
    for g in range(SSD_GROUPS):
        bg = bm[:, g * 128:(g + 1) * 128]
        cg = cm[:, g * 128:(g + 1) * 128].astype(BF16)
        cb = _dot_nt(cg, bg.astype(BF16))
        bg_t = bg.T.astype(BF16)
        for pi in range(4):
            pair = g * 4 + pi
            sl = slice(pair * LANES, (pair + 1) * LANES)
            ms = []
            for h in (2 * pair, 2 * pair + 1):
                seg = a_cs[:, h:h + 1] - a_cs_t[h:h + 1, :]
                ms.append((cb * jnp.exp(jnp.where(causal, seg, NEG_INF))).astype(BF16))
            y_diag = jnp.where(low_head, _dot(ms[0], xdt_b[:, sl]), _dot(ms[1], xdt_b[:, sl]))
            st = s_scr[:, sl]
            y_off = _dot(cg, st.astype(BF16)) * ecs[:, sl]
            s_scr[:, sl] = etot[:, sl] * st + _dot(bg_t, xdtd_b[:, sl])
            ys.append(y_diag + y_off)

    y = jnp.concatenate(ys, axis=1)
    yn_ref[...] = _ssd_gate_norm(y, xs, z_ref[...], d_ref, nw_ref).astype(yn_ref.dtype)
    xp_scr[0:8, :] = xp_scr[chunk:chunk + 8, :]

    @pl.when(c == pl.num_programs(1) - 1)
    def _():
        st_ref[0] = s_scr[...].T
        buf_ref[0] = xp_scr[chunk + 5:chunk + 8, :]


def _ssd_prompt(z, xbc, dt, cw, cb, dtb, a_neg, expand, d_exp, nw, batch, seq, chunk=SSD_CHUNK):
    nc = seq // chunk
    row = lambda b, c: (b * nc + c, 0)
    const = lambda b, c: (0, 0)
    return pl.pallas_call(
        functools.partial(_ssd_prompt_kernel, chunk=chunk),
        out_shape=(
            jax.ShapeDtypeStruct((batch * seq, SSD_D_INNER), BF16),
            jax.ShapeDtypeStruct((batch, SSD_D_INNER, SSD_STATE), F32),
            jax.ShapeDtypeStruct((batch, SSD_CONV - 1, SSD_CONV_DIM), F32),
        ),
        grid=(batch, nc),
        in_specs=[
            pl.BlockSpec((chunk, SSD_D_INNER), row),
            pl.BlockSpec((chunk, SSD_CONV_DIM), row),
            pl.BlockSpec((chunk, LANES), row),
            pl.BlockSpec((SSD_CONV, SSD_CONV_DIM), const),
            pl.BlockSpec((1, SSD_CONV_DIM), const),
            pl.BlockSpec((1, LANES), const),
            pl.BlockSpec((1, LANES), const),
            pl.BlockSpec((LANES, SSD_D_INNER), const),
            pl.BlockSpec((1, SSD_D_INNER), const),
            pl.BlockSpec((1, SSD_D_INNER), const),
        ],
        out_specs=(
            pl.BlockSpec((chunk, SSD_D_INNER), row),
            pl.BlockSpec((1, SSD_D_INNER, SSD_STATE), lambda b, c: (b, 0, 0)),
            pl.BlockSpec((1, SSD_CONV - 1, SSD_CONV_DIM), lambda b, c: (b, 0, 0)),
        ),
        scratch_shapes=[
            pltpu.VMEM((chunk + 8, SSD_CONV_DIM), F32),
            pltpu.VMEM((SSD_STATE, SSD_D_INNER), F32),
        ],
        compiler_params=_params(("parallel", "arbitrary")),
        name="ssd_prompt",
    )(z, xbc, dt, cw, cb, dtb, a_neg, expand, d_exp, nw)


def _ssd_decode_pre_kernel(xbc_ref, dt_ref, buf_ref, cw_ref, cb_ref, dtb_ref, a_ref, e_ref,
                           xs_ref, xdt_ref, dec_ref, b_ref, c_ref, nbuf_ref):
    xbc = xbc_ref[...]
    conv = cb_ref[...] + xbc * cw_ref[3:4, :]
    for k in range(SSD_CONV - 1):
        conv = conv + buf_ref[k] * cw_ref[k:k + 1, :]
    u = _silu(conv)
    xs = u[:, :SSD_D_INNER]
    dt = jax.nn.softplus(dt_ref[...] + dtb_ref[...])
    ex = _dot_exact(jnp.concatenate([dt, dt * a_ref[...]], axis=0), e_ref[...])
    t = xbc.shape[0]
    xs_ref[...] = xs
    xdt_ref[...] = xs * ex[:t]
    dec_ref[...] = jnp.exp(ex[t:])
    b_ref[...] = u[:, SSD_D_INNER:SSD_D_INNER + 512]
    c_ref[...] = u[:, SSD_D_INNER + 512:]
    nbuf_ref[0] = buf_ref[1]
    nbuf_ref[1] = buf_ref[2]
    nbuf_ref[2] = xbc


def _ssd_decode_pre(xbc, dt, buf, cw, cb, dtb, a_neg, expand):
    t = xbc.shape[0]
    wide = jax.ShapeDtypeStruct((t, SSD_D_INNER), F32)
    grp = jax.ShapeDtypeStruct((t, 512), F32)
    return pl.pallas_call(
        _ssd_decode_pre_kernel,
        out_shape=(wide, wide, wide, grp, grp, jax.ShapeDtypeStruct((SSD_CONV - 1, t, SSD_CONV_DIM), F32)),
        compiler_params=pltpu.CompilerParams(vmem_limit_bytes=VMEM_LIMIT),
        name="ssd_decode_pre",
    )(xbc, dt, buf, cw, cb, dtb, a_neg, expand)


def _ssd_decode_state_kernel(xdt_ref, dec_ref, b_ref, c_ref, st_ref, y_ref, nst_ref):
    xdt = xdt_ref[0]
    lane_g = lax.broadcasted_iota(jnp.int32, (8, SSD_D_INNER), 1) // 512
    row = lax.broadcasted_iota(jnp.int32, (8, SSD_D_INNER), 0)
    lhs = jnp.where(row == lane_g, xdt, 0.0) + jnp.where(row == 4, dec_ref[0], 0.0)
    brow = lax.broadcasted_iota(jnp.int32, (8, LANES), 0)
    bmat = jnp.zeros((8, LANES), F32)
    cmat = jnp.zeros((8, LANES), F32)
    for g in range(SSD_GROUPS):
        bmat = jnp.where(brow == g, b_ref[0][:, g * 128:(g + 1) * 128], bmat)
        cmat = jnp.where(brow == g, c_ref[0][:, g * 128:(g + 1) * 128], cmat)
    rhs = jnp.concatenate([bmat, (brow == 4).astype(F32)], axis=1)
    both = _dot_exact(lhs, rhs, TN_DIMS)
    new = st_ref[0] * both[:, LANES:] + both[:, :LANES]
    nst_ref[0] = new
    yg = _dot_nt(cmat.astype(BF16), new.astype(BF16))
    y_ref[0] = jnp.sum(jnp.where(row == lane_g, yg, 0.0), axis=0, keepdims=True)


def _ssd_decode_state(xdt, dec, bm, cm, state):
    t = xdt.shape[0]
    tok3 = lambda w: pl.BlockSpec((1, 1, w), lambda i: (i, 0, 0))
    st_spec = pl.BlockSpec((1, SSD_D_INNER, SSD_STATE), lambda i: (i, 0, 0))
    return pl.pallas_call(
        _ssd_decode_state_kernel,
        out_shape=(jax.ShapeDtypeStruct((t, 1, SSD_D_INNER), F32),
                   jax.ShapeDtypeStruct((t, SSD_D_INNER, SSD_STATE), F32)),
        grid=(t,),
        in_specs=[tok3(SSD_D_INNER), tok3(SSD_D_INNER), tok3(512), tok3(512), st_spec],
        out_specs=(tok3(SSD_D_INNER), st_spec),
        compiler_params=_params(("parallel",)),
        name="ssd_decode_state",
    )(xdt.reshape(t, 1, -1), dec.reshape(t, 1, -1), bm.reshape(t, 1, -1), cm.reshape(t, 1, -1), state)


def _ssd_decode_post_kernel(y_ref, xs_ref, z_ref, d_ref, nw_ref, o_ref):
    o_ref[...] = _ssd_gate_norm(y_ref[...], xs_ref[...], z_ref[...], d_ref, nw_ref).astype(o_ref.dtype)


def _ssd_decode_post(y, xs, z, d_exp, nw):
    return pl.pallas_call(
        _ssd_decode_post_kernel,
        out_shape=jax.ShapeDtypeStruct(y.shape, BF16),
        compiler_params=pltpu.CompilerParams(vmem_limit_bytes=VMEM_LIMIT),
        name="ssd_decode_post",
    )(y, xs, z, d_exp, nw)


def _xa_prompt_kernel(x_ref, g_ref, wq_ref, k_ref, v_ref, wo_ref, o_ref):
    x = x_ref[...]
    q = _dot(_rms(x, g_ref[...]).astype(BF16), wq_ref[...]).astype(BF16)
    k = k_ref[0].astype(BF16)
    v = v_ref[0].astype(BF16)
    heads = []
    for h in range(XA_HEADS):
        sl = slice(h * XA_HEAD_DIM, (h + 1) * XA_HEAD_DIM)
        s = _dot_nt(q[:, sl], k[:, sl]) * XA_SCALE
        p = jnp.exp(s - jnp.max(s, axis=-1, keepdims=True))
        o = _dot(p.astype(BF16), v[:, sl]) / jnp.sum(p, axis=-1, keepdims=True)
        heads.append(o.astype(BF16))
    o_ref[...] = x + _dot(jnp.concatenate(heads, axis=1), wo_ref[...])


def _xa_prompt(x, gain, wq, kv_mem, wo, batch, seq):
    m, d = x.shape
    tm = 512
    nt = seq // tm
    mem = kv_mem.shape[1]
    return pl.pallas_call(
        _xa_prompt_kernel,
        out_shape=jax.ShapeDtypeStruct((m, d), F32),
        grid=(m // tm,),
        in_specs=[
            pl.BlockSpec((tm, d), lambda i: (i, 0)),
            pl.BlockSpec((1, d), lambda i: (0, 0)),
            pl.BlockSpec((d, d), lambda i: (0, 0)),
            pl.BlockSpec((1, mem, d), lambda i: (i // nt, 0, 0)),
            pl.BlockSpec((1, mem, d), lambda i: (i // nt, 0, 1)),
            pl.BlockSpec((d, d), lambda i: (0, 0)),
        ],
        out_specs=pl.BlockSpec((tm, d), lambda i: (i, 0)),
        compiler_params=_params(("parallel",)),
        name="xa_prompt",
    )(x, gain.reshape(1, d), wq, kv_mem, kv_mem, wo)


def _xa_decode_kernel(q_ref, k_ref, v_ref, o_ref, *, tokens):
    n = k_ref.shape[2]
    lane = lax.broadcasted_iota(jnp.int32, (8, n), 1)
    own = lane % 8 == lax.broadcasted_iota(jnp.int32, (8, n), 0)
    first_half = lane % 8 < XA_HEADS
    strides = [8 << i for i in range((n // 8).bit_length() - 1)]

    def over_tokens(op, x):
        for sh in strides:
            x = op(x, pltpu.roll(x, sh, 1))
        return x

    for t in range(tokens):
        g = _dot_nt(q_ref[0, t].astype(BF16), k_ref[0, t].astype(BF16))
        part = jnp.broadcast_to(jnp.sum(jnp.where(own, g, 0.0), axis=0, keepdims=True), (8, n))
        s = (part + pltpu.roll(part, n - XA_HEADS, 1)) * XA_SCALE
        p = jnp.exp(s - over_tokens(jnp.maximum, s))
        p = p / over_tokens(jnp.add, p)
        p = jnp.where(first_half, p, pltpu.roll(p, XA_HEADS, 1))
        o_ref[0, t] = _dot(jnp.where(own, p, 0.0).astype(BF16), v_ref[0, t].astype(BF16))


def _xa_decode(q, mem_k, mem_v, layer, tokens=4):
    t, d = q.shape
    rows = mem_k.shape[2]
    half = XA_HEAD_DIM // 2
    q8 = jnp.transpose(q.reshape(t, XA_HEADS, 2, half), (0, 2, 1, 3)).reshape(t // tokens, tokens, 8, half)
    kv_spec = pl.BlockSpec((1, tokens, rows, half), lambda i: (layer, i, 0, 0))
    tok_spec = pl.BlockSpec((1, tokens, 8, half), lambda i: (i, 0, 0, 0))
    o8 = pl.pallas_call(
        functools.partial(_xa_decode_kernel, tokens=tokens),
        out_shape=jax.ShapeDtypeStruct((t // tokens, tokens, 8, half), F32),
        grid=(t // tokens,),
        in_specs=[tok_spec, kv_spec, kv_spec],
        out_specs=tok_spec,
        compiler_params=_params(("parallel",)),
        name="xa_decode",
    )(q8, mem_k, mem_v)
    return jnp.transpose(o8.reshape(t, 2, XA_HEADS, half), (0, 2, 1, 3)).reshape(t, d)


def kernel(x_prompt, x_sample, cache_mla_ckv, cache_mla_kpe, cache_diff_k, cache_diff_v, state_ssm, state_conv, cache_mem_k, cache_mem_v, page_table, mem_prompt, norm_mix, norm_xa, norm_ffn, norm_final, xa_mem_norm, xa_wq, xa_wk, xa_wv, xa_wo, ffn_w_gate, ffn_w_up, ffn_w_down, mla_wq_a, mla_q_norm, mla_wq_b, mla_wkv_a, mla_kv_norm, mla_w_uk, mla_w_uv, mla_wo, diff_wq, diff_wk, diff_wv, diff_lambda_q1, diff_lambda_k1, diff_lambda_q2, diff_lambda_k2, diff_subln, diff_wo, ssd_w_in, ssd_conv_w, ssd_conv_b, ssd_dt_bias, ssd_A_log, ssd_D, ssd_norm, ssd_w_out):
    batch, seq, d = x_prompt.shape
    nb = x_sample.shape[0]
    depth = norm_mix.shape[0]
    n_mix = 3
    bf = lambda w: w.astype(BF16)

    n_a = mla_wq_a.shape[0]
    wkv_a = jnp.pad(mla_wkv_a, ((0, 0), (0, 0), (0, MLA_QK - mla_wkv_a.shape[2])))
    w_qkv_a = bf(jnp.concatenate([mla_wq_a, wkv_a], axis=2))
    wq_b = mla_wq_b.reshape(n_a, -1, MLA_HEADS, MLA_NOPE + MLA_ROPE)
    wq_b = bf(jnp.concatenate([
        wq_b[..., :MLA_NOPE].reshape(n_a, -1, MLA_HEADS * MLA_NOPE),
        jnp.pad(wq_b[..., MLA_NOPE:], ((0, 0), (0, 0), (0, 0), (0, 64))).reshape(n_a, -1, MLA_HEADS * 128),
    ], axis=2))
    w_uk = bf(jnp.transpose(mla_w_uk, (0, 2, 3, 1)))
    w_uv = bf(jnp.transpose(mla_w_uv, (0, 2, 1, 3)))
    w_mla_o = bf(mla_wo)
    w_diff_qkv = bf(jnp.concatenate([diff_wq, diff_wk, diff_wv], axis=2))
    w_diff_o = bf(diff_wo)
    w_ssd_z = bf(ssd_w_in[:, :, :SSD_D_INNER])
    w_ssd_xbc = bf(ssd_w_in[:, :, SSD_D_INNER:SSD_D_INNER + SSD_CONV_DIM])
    w_ssd_dt = bf(jnp.pad(ssd_w_in[:, :, SSD_D_INNER + SSD_CONV_DIM:], ((0, 0), (0, 0), (0, LANES - SSD_HEADS))))
    w_ssd_out = bf(ssd_w_out)
    pad_heads = lambda a: jnp.pad(a, ((0, 0), (0, LANES - SSD_HEADS)))
    ssd_dtb = pad_heads(ssd_dt_bias)
    ssd_a = pad_heads(-jnp.exp(ssd_A_log.astype(F32)))
    ssd_d_exp = jnp.repeat(ssd_D, SSD_HEAD_DIM, axis=1)
    expand = (jnp.arange(LANES)[:, None] == jnp.arange(SSD_D_INNER)[None, :] // SSD_HEAD_DIM).astype(F32)
    w_xa_q, w_xa_o = bf(xa_wq), bf(xa_wo)
    w_xa_kv = bf(jnp.concatenate([xa_wk, xa_wv], axis=2))
    w_gate, w_up, w_down = bf(ffn_w_gate), bf(ffn_w_up), bf(ffn_w_down)

    def mem_view(c):
        mt = c.shape[2]
        c = c.reshape(depth, nb, mt, XA_HEADS, 2, XA_HEAD_DIM // 2)
        return jnp.transpose(c, (0, 1, 2, 4, 3, 5)).reshape(depth, nb, mt * 8, XA_HEAD_DIM // 2)

    mem_k, mem_v = mem_view(cache_mem_k), mem_view(cache_mem_v)
    n_pool = cache_diff_k.shape[1]
    pool_kpe_t = jnp.swapaxes(cache_mla_kpe, 2, 3)
    pool_kt = jnp.transpose(cache_diff_k, (0, 1, 3, 4, 5, 2)).reshape(-1, n_pool, 512, PAGE_SIZE)
    pool_v = cache_diff_v.reshape(-1, n_pool, PAGE_SIZE * DIFF_GROUPS, 2 * DIFF_HEAD_DIM)
    mem_flat = mem_prompt.reshape(-1, d)

    def run(x, pos_tab, prompt):
        m = x.shape[0]
        cos, sin = _rope_tables(pos_tab)
        new = {k: [] for k in ("mla_ckv", "mla_kpe", "diff_k", "diff_v", "ssm", "conv", "mem_k", "mem_v")}
        for i in range(depth):
            kind, j = i % n_mix, i // n_mix
            g_mix = norm_mix[i]
            if kind == 0:
                qkv_a = _linear(x, w_qkv_a[j], gain=g_mix, name="mla_qkv_a")
                q = _linear(qkv_a, wq_b[j], gain=mla_q_norm[j], name="mla_q_b")
                qcat, kcat, ckv, kpe = _mla_prep(q, qkv_a, cos, sin, w_uk[j], mla_kv_norm[j])
                if prompt:
                    o_lat = _flash_mla(qcat, kcat, batch, seq)
                else:
                    o_lat = _decode_mla(page_table, jnp.transpose(qcat, (1, 0, 2)), kcat,
                                        cache_mla_ckv, pool_kpe_t, j).reshape(m, -1)
                x = _mla_out(o_lat, w_uv[j], w_mla_o[j], x)
                new["mla_ckv"].append(ckv)
                new["mla_kpe"].append(kpe)
            elif kind == 1:
                lam_init = 0.8 - 0.6 * math.exp(-0.3 * i)
                lams = (diff_lambda_q1[j], diff_lambda_k1[j], diff_lambda_q2[j], diff_lambda_k2[j])
                qkv = _linear(x, w_diff_qkv[j], gain=g_mix, name="diff_qkv")
                q_b, k32, k_b, v32, v_b = _diff_prep(qkv, cos, sin)
                if prompt:
                    o = _flash_diff(q_b, k_b, v_b, lams, diff_subln[j], lam_init, batch, seq)
                else:
                    q5 = q_b.reshape(m, DIFF_GROUPS, 2, 2, DIFF_HEAD_DIM)
                    eye_g = jnp.eye(DIFF_GROUPS, dtype=BF16)
                    eye_c = jnp.eye(2, dtype=BF16)
                    q_rows = jnp.einsum("tgrcd,gG,cC->tcrgGCd", q5, eye_g, eye_c).reshape(m, 16, 512)
                    v_rows = jnp.tile(v_b.astype(F32).reshape(m, DIFF_GROUPS, LANES), (1, 4, 1))
                    o = _decode_diff(page_table, q_rows, k_b, v_rows, pool_kt, pool_v, lams, diff_subln[j],
                                     lam_init, j)
                    o = jnp.transpose(o.reshape(m, 2, DIFF_GROUPS, LANES), (0, 2, 1, 3)).reshape(m, -1)
                x = _linear(o, w_diff_o[j], res=x, name="diff_o")
                new["diff_k"].append(k32)
                new["diff_v"].append(v32)
            else:
                z = _linear(x, w_ssd_z[j], gain=g_mix, name="ssd_in_z")
                xbc = _linear(x, w_ssd_xbc[j], gain=g_mix, name="ssd_in_xbc")
                dt = _linear(x, w_ssd_dt[j], gain=g_mix, name="ssd_in_dt")
                cw, cb = ssd_conv_w[j], ssd_conv_b[j].reshape(1, -1)
                dtb, a_neg = ssd_dtb[j].reshape(1, -1), ssd_a[j].reshape(1, -1)
                d_exp, nw = ssd_d_exp[j].reshape(1, -1), ssd_norm[j].reshape(1, -1)
                if prompt:
                    yn, st, buf = _ssd_prompt(z, xbc, dt, cw, cb, dtb, a_neg, expand, d_exp, nw, batch, seq)
                else:
                    xs, xdt, dec, bm, cm, nbuf = _ssd_decode_pre(
                        xbc, dt, jnp.transpose(state_conv[j], (1, 0, 2)), cw, cb, dtb, a_neg, expand)
                    y, st = _ssd_decode_state(xdt, dec, bm, cm, state_ssm[j].reshape(m, SSD_D_INNER, SSD_STATE))
                    yn = _ssd_decode_post(y.reshape(m, -1), xs, z, d_exp, nw)
                    buf = jnp.transpose(nbuf, (1, 0, 2))
                x = _linear(yn, w_ssd_out[j], res=x, name="ssd_out")
                new["ssm"].append(st.reshape(-1, SSD_HEADS, SSD_HEAD_DIM, SSD_STATE))
                new["conv"].append(buf)
            if prompt:
                kv_mem = _linear(mem_flat, w_xa_kv[i], gain=xa_mem_norm[i], name="xa_mem_kv")
                new["mem_k"].append(kv_mem[:, :d])
                new["mem_v"].append(kv_mem[:, d:])
                x = _xa_prompt(x, norm_xa[i], w_xa_q[i], kv_mem.reshape(batch, -1, 2 * d), w_xa_o[i], batch, seq)
            else:
                q = _linear(x, w_xa_q[i], gain=norm_xa[i], name="xa_q")
                o = _xa_decode(q, mem_k, mem_v, i)
                x = _linear(o, w_xa_o[i], res=x, name="xa_o")
            x = _ffn(x, norm_ffn[i], w_gate[i], w_up[i], w_down[i])
        return _final_norm(x, norm_final), {k: jnp.stack(v) for k, v in new.items() if v}

    pos_p = jnp.arange(seq, dtype=jnp.int32)
    pos_s = jnp.full((nb,), PAST_LEN, jnp.int32)
    y_p, new_p = run(x_prompt.reshape(batch * seq, d), pos_p, True)
    y_s, new_s = run(x_sample.reshape(nb, d), pos_s, False)

    n_b = new_p["diff_k"].shape[0]
    n_c = new_p["ssm"].shape[0]
    mem_t = mem_prompt.shape[1]
    return (
        y_p.reshape(batch, seq, d),
        y_s.reshape(nb, 1, d),
        new_p["mla_ckv"].reshape(n_a, batch, seq, MLA_KV_LORA),
        new_p["mla_kpe"].reshape(n_a, batch, seq, MLA_ROPE),
        new_p["diff_k"].reshape(n_b, batch, seq, DIFF_GROUPS, 2, DIFF_HEAD_DIM),
        new_p["diff_v"].reshape(n_b, batch, seq, DIFF_GROUPS, 2 * DIFF_HEAD_DIM),
        new_p["ssm"].reshape(n_c, batch, SSD_HEADS, SSD_HEAD_DIM, SSD_STATE),
        new_p["conv"],
        new_p["mem_k"].reshape(depth, batch, mem_t, XA_HEADS, XA_HEAD_DIM),
        new_p["mem_v"].reshape(depth, batch, mem_t, XA_HEADS, XA_HEAD_DIM),
        new_s["mla_ckv"].reshape(n_a, nb, 1, MLA_KV_LORA),
        new_s["mla_kpe"].reshape(n_a, nb, 1, MLA_ROPE),
        new_s["diff_k"].reshape(n_b, nb, 1, DIFF_GROUPS, 2, DIFF_HEAD_DIM),
        new_s["diff_v"].reshape(n_b, nb, 1, DIFF_GROUPS, 2 * DIFF_HEAD_DIM),
        new_s["ssm"].reshape(n_c, nb, SSD_HEADS, SSD_HEAD_DIM, SSD_STATE),
        new_s["conv"],
    )
```

```python
import functools
import math

import jax
import jax.numpy as jnp
from jax import lax
from jax.experimental import pallas as pl
from jax.experimental.pallas import tpu as pltpu

F32 = jnp.float32
BF16 = jnp.bfloat16

RMS_EPS = 1e-6
ROPE_THETA = 10000.0
NEG_INF = -1e30
PAST_LEN = 8192
PAGE_SIZE = 128

MLA_HEADS = 8
MLA_NOPE = 128
MLA_ROPE = 64
MLA_KV_LORA = 256
MLA_V = 128
MLA_QK = 384
MLA_SCALE = (MLA_NOPE + MLA_ROPE) ** -0.5

DIFF_HEADS = 8
DIFF_GROUPS = 4
DIFF_HEAD_DIM = 64
DIFF_SCALE = DIFF_HEAD_DIM ** -0.5

SSD_D_INNER = 2048
SSD_HEADS = 32
SSD_HEAD_DIM = 64
SSD_GROUPS = 4
SSD_STATE = 128
SSD_CONV = 4
SSD_CONV_DIM = SSD_D_INNER + 2 * SSD_GROUPS * SSD_STATE
SSD_CHUNK = 128

XA_HEADS = 4
XA_HEAD_DIM = 256
XA_SCALE = XA_HEAD_DIM ** -0.5

LOG2E = math.log2(math.e)
SOFTMAX_ROWS = 32
LANES = 128
VMEM_LIMIT = 48 * 1024 * 1024

NT_DIMS = (((1,), (1,)), ((), ()))
TN_DIMS = (((0,), (0,)), ((), ()))


def _params(sem):
    return pltpu.CompilerParams(dimension_semantics=sem, vmem_limit_bytes=VMEM_LIMIT)


def _rms(x, g):
    return x * lax.rsqrt(jnp.mean(x * x, axis=-1, keepdims=True) + RMS_EPS) * g


def _silu(x):
    return x * jax.nn.sigmoid(x)


def _dot(a, b):
    return jnp.dot(a, b, preferred_element_type=F32)


def _dot_nt(a, b):
    return lax.dot_general(a, b, NT_DIMS, preferred_element_type=F32)


def _dot_exact(a, b, dims=(((1,), (0,)), ((), ()))):
    return lax.dot_general(a, b, dims, precision=lax.Precision.HIGHEST, preferred_element_type=F32)


def _linear_kernel(*refs, has_norm, has_res):
    refs = list(refs)
    x_ref = refs.pop(0)
    g_ref = refs.pop(0) if has_norm else None
    w_ref = refs.pop(0)
    r_ref = refs.pop(0) if has_res else None
    o_ref, h_scr = refs

    @pl.when(pl.program_id(1) == 0)
    def _():
        x = x_ref[...].astype(F32)
        if has_norm:
            x = _rms(x, g_ref[...])
        h_scr[...] = x.astype(BF16)

    acc = _dot(h_scr[...], w_ref[...])
    if has_res:
        acc = acc + r_ref[...]
    o_ref[...] = acc.astype(o_ref.dtype)


def _pick_tn(n):
    for tn in (1024, 768, 512, 384, 256, 128):
        if n % tn == 0:
            return tn
    return n


def _linear(x, w, gain=None, res=None, out_dtype=F32, x_col=0, name="linear"):
    m = x.shape[0]
    k, n = w.shape
    tm = min(m, 1024)
    tn = _pick_tn(n)
    in_specs = [pl.BlockSpec((tm, k), lambda i, j: (i, x_col))]
    args = [x]
    if gain is not None:
        in_specs.append(pl.BlockSpec((1, k), lambda i, j: (0, 0)))
        args.append(gain.reshape(1, k).astype(F32))
    in_specs.append(pl.BlockSpec((k, tn), lambda i, j: (0, j)))
    args.append(w)
    if res is not None:
        in_specs.append(pl.BlockSpec((tm, tn), lambda i, j: (i, j)))
        args.append(res)
    return pl.pallas_call(
        functools.partial(_linear_kernel, has_norm=gain is not None, has_res=res is not None),
        out_shape=jax.ShapeDtypeStruct((m, n), out_dtype),
        grid=(m // tm, n // tn),
        in_specs=in_specs,
        out_specs=pl.BlockSpec((tm, tn), lambda i, j: (i, j)),
        scratch_shapes=[pltpu.VMEM((tm, k), BF16)],
        compiler_params=_params(("parallel", "arbitrary")),
        name=name,
    )(*args)


def _ffn_kernel(x_ref, g_ref, wg_ref, wu_ref, wd_ref, o_ref, h_scr, acc_scr):
    c = pl.program_id(1)

    @pl.when(c == 0)
    def _():
        h_scr[...] = _rms(x_ref[...], g_ref[...]).astype(BF16)
        acc_scr[...] = jnp.zeros_like(acc_scr)

    h = h_scr[...]
    a = _silu(_dot(h, wg_ref[...])) * _dot(h, wu_ref[...])
    acc_scr[...] += _dot(a.astype(BF16), wd_ref[...])

    @pl.when(c == pl.num_programs(1) - 1)
    def _():
        o_ref[...] = x_ref[...] + acc_scr[...]


def _ffn(x, gain, wg, wu, wd):
    m, d = x.shape
    hid = wg.shape[1]
    tm = min(m, 1024)
    tc = 256
    return pl.pallas_call(
        _ffn_kernel,
        out_shape=jax.ShapeDtypeStruct((m, d), F32),
        grid=(m // tm, hid // tc),
        in_specs=[
            pl.BlockSpec((tm, d), lambda i, c: (i, 0)),
            pl.BlockSpec((1, d), lambda i, c: (0, 0)),
            pl.BlockSpec((d, tc), lambda i, c: (0, c)),
            pl.BlockSpec((d, tc), lambda i, c: (0, c)),
            pl.BlockSpec((tc, d), lambda i, c: (c, 0)),
        ],
        out_specs=pl.BlockSpec((tm, d), lambda i, c: (i, 0)),
        scratch_shapes=[pltpu.VMEM((tm, d), BF16), pltpu.VMEM((tm, d), F32)],
        compiler_params=_params(("parallel", "arbitrary")),
        name="ffn",
    )(x, gain.reshape(1, d), wg, wu, wd)


def _norm_kernel(x_ref, g_ref, o_ref):
    o_ref[...] = _rms(x_ref[...], g_ref[...])


def _final_norm(x, gain):
    m, d = x.shape
    tm = min(m, 1024)
    return pl.pallas_call(
        _norm_kernel,
        out_shape=jax.ShapeDtypeStruct((m, d), F32),
        grid=(m // tm,),
        in_specs=[pl.BlockSpec((tm, d), lambda i: (i, 0)), pl.BlockSpec((1, d), lambda i: (0, 0))],
        out_specs=pl.BlockSpec((tm, d), lambda i: (i, 0)),
        compiler_params=_params(("parallel",)),
        name="final_norm",
    )(x, gain.reshape(1, d))


def _rope(x, cos, sin):
    n = x.shape[1]
    reps = n // LANES
    if reps > 1:
        cos = jnp.concatenate([cos] * reps, axis=1)
        sin = jnp.concatenate([sin] * reps, axis=1)
    lane = lax.broadcasted_iota(jnp.int32, x.shape, 1)
    first_half = (lane % 64) < 32
    partner = jnp.where(first_half, pltpu.roll(x, n - 32, 1), pltpu.roll(x, 32, 1))
    return x * cos + partner * sin


def _rope_tables(pos):
    inv = ROPE_THETA ** (-jnp.arange(32, dtype=F32) * 2.0 / 64)
    ang = pos.astype(F32)[:, None] * inv[None, :]
    c, s = jnp.cos(ang), jnp.sin(ang)
    return jnp.concatenate([c, c, c, c], axis=1), jnp.concatenate([-s, s, -s, s], axis=1)


def _mla_prep_kernel(q_ref, kv_ref, cos_ref, sin_ref, wuk_ref, kvg_ref,
                     qcat_ref, kcat_ref, ckv_ref, kpe_ref):
    cos, sin = cos_ref[...], sin_ref[...]
    q = q_ref[...] * (MLA_SCALE * LOG2E)
    q_pe = _rope(q[:, 1024:], cos, sin).astype(BF16)
    q_nope = q[:, :1024].astype(BF16)
    for h in range(MLA_HEADS):
        q_lat = _dot(q_nope[:, h * 128:(h + 1) * 128], wuk_ref[h])
        qcat_ref[h, :, 0:256] = q_lat.astype(BF16)
        qcat_ref[h, :, 256:384] = q_pe[:, h * 128:(h + 1) * 128]
    kv = kv_ref[...]
    ckv = _rms(kv[:, :256], kvg_ref[...])
    kpe = _rope(kv[:, 256:], cos, sin)
    ckv_ref[...] = ckv
    kpe_ref[...] = kpe[:, :64]
    kcat_ref[:, 0:256] = ckv.astype(BF16)
    kcat_ref[:, 256:384] = kpe.astype(BF16)


def _mla_prep(q, qkv_a, cos, sin, wuk, kv_gain):
    m = q.shape[0]
    tm = min(m, 512)
    nt = cos.shape[0] // tm
    return pl.pallas_call(
        _mla_prep_kernel,
        out_shape=(
            jax.ShapeDtypeStruct((MLA_HEADS, m, MLA_QK), BF16),
            jax.ShapeDtypeStruct((m, MLA_QK), BF16),
            jax.ShapeDtypeStruct((m, MLA_KV_LORA), F32),
            jax.ShapeDtypeStruct((m, MLA_ROPE), F32),
        ),
        grid=(m // tm,),
        in_specs=[
            pl.BlockSpec((tm, 2048), lambda i: (i, 0)),
            pl.BlockSpec((tm, MLA_QK), lambda i: (i, 1)),
            pl.BlockSpec((tm, LANES), lambda i: (i % nt, 0)),
            pl.BlockSpec((tm, LANES), lambda i: (i % nt, 0)),
            pl.BlockSpec((MLA_HEADS, 128, 256), lambda i: (0, 0, 0)),
            pl.BlockSpec((1, 256), lambda i: (0, 0)),
        ],
        out_specs=(
            pl.BlockSpec((MLA_HEADS, tm, MLA_QK), lambda i: (0, i, 0)),
            pl.BlockSpec((tm, MLA_QK), lambda i: (i, 0)),
            pl.BlockSpec((tm, MLA_KV_LORA), lambda i: (i, 0)),
            pl.BlockSpec((tm, MLA_ROPE), lambda i: (i, 0)),
        ),
        compiler_params=_params(("parallel",)),
        name="mla_prep",
    )(q, qkv_a, cos, sin, wuk, kv_gain.reshape(1, 256))


def _tree(op, xs):
    xs = list(xs)
    while len(xs) > 1:
        xs = [op(xs[i], xs[i + 1]) if i + 1 < len(xs) else xs[i] for i in range(0, len(xs), 2)]
    return xs[0]


def _online_softmax(s, row0, m_scr, l_scr, acc_scr, masked):
    rows_n, tk = s.shape
    ch = SOFTMAX_ROWS
    acc_tiles = acc_scr.shape[1] // LANES
    out = []
    for c in range(rows_n // ch):
        rows = pl.ds(row0 + c * ch, ch)
        sc = s[c * ch:(c + 1) * ch, :]
        if masked:
            qpos = c * ch + lax.broadcasted_iota(jnp.int32, (ch, tk), 0)
            kpos = lax.broadcasted_iota(jnp.int32, (ch, tk), 1)
            sc = jnp.where(kpos <= qpos, sc, NEG_INF)
        tiles = [sc[:, t * LANES:(t + 1) * LANES] for t in range(tk // LANES)]
        m_prev = m_scr[rows, :]
        m_new = jnp.maximum(m_prev, jnp.max(_tree(jnp.maximum, tiles), axis=1, keepdims=True))
        alpha = jnp.exp2(m_prev - m_new)
        ps = [jnp.exp2(t - m_new) for t in tiles]
        if l_scr is not None:
            l_scr[rows, :] = alpha * l_scr[rows, :] + jnp.sum(_tree(jnp.add, ps), axis=1, keepdims=True)
        m_scr[rows, :] = m_new
        acc_scr[rows, :] = acc_scr[rows, :] * jnp.concatenate([alpha] * acc_tiles, axis=1)
        out.append(jnp.concatenate(ps, axis=1).astype(BF16))
    return jnp.concatenate(out, axis=0)


def _flash_mla_kernel(q_ref, k_ref, o_ref, m_scr, l_scr, acc_scr, *, tile):
    qi, ki = pl.program_id(1), pl.program_id(2)

    @pl.when(ki == 0)
    def _():
        m_scr[...] = jnp.full_like(m_scr, NEG_INF)
        l_scr[...] = jnp.zeros_like(l_scr)
        acc_scr[...] = jnp.zeros_like(acc_scr)

    def all_heads(masked):
        def head(h, carry):
            row0 = pl.multiple_of(h * tile, tile)
            k = k_ref[...]
            p = _online_softmax(_dot_nt(q_ref[h], k), row0, m_scr, l_scr, acc_scr, masked)
            acc_scr[pl.ds(row0, tile), :] += _dot(p, k[:, :MLA_KV_LORA])
            return carry
        lax.fori_loop(0, MLA_HEADS, head, 0, unroll=4)

    @pl.when(ki < qi)
    def _():
        all_heads(False)

    @pl.when(ki == qi)
    def _():
        all_heads(True)
        for h in range(MLA_HEADS):
            rows = slice(h * tile, (h + 1) * tile)
            l = l_scr[rows, :]
            o = acc_scr[rows, :] / jnp.concatenate([l, l], axis=1)
            o_ref[:, h * 256:(h + 1) * 256] = o.astype(o_ref.dtype)


def _flash_mla(qcat, kcat, batch, seq, tile=512):
    nt = seq // tile
    return pl.pallas_call(
        functools.partial(_flash_mla_kernel, tile=tile),
        out_shape=jax.ShapeDtypeStruct((batch * seq, MLA_HEADS * MLA_KV_LORA), BF16),
        grid=(batch, nt, nt),
        in_specs=[
            pl.BlockSpec((MLA_HEADS, tile, MLA_QK), lambda b, qi, ki: (0, b * nt + qi, 0)),
            pl.BlockSpec((tile, MLA_QK), lambda b, qi, ki: (b * nt + jnp.minimum(ki, qi), 0)),
        ],
        out_specs=pl.BlockSpec((tile, MLA_HEADS * MLA_KV_LORA), lambda b, qi, ki: (b * nt + qi, 0)),
        scratch_shapes=[
            pltpu.VMEM((MLA_HEADS * tile, LANES), F32),
            pltpu.VMEM((MLA_HEADS * tile, LANES), F32),
            pltpu.VMEM((MLA_HEADS * tile, MLA_KV_LORA), F32),
        ],
        compiler_params=_params(("parallel", "parallel", "arbitrary")),
        name="flash_mla",
    )(qcat, kcat)


def _mla_out_kernel(o_ref, wuv_ref, wo_ref, x_ref, y_ref):
    o = o_ref[...].astype(BF16)
    heads = [_dot(o[:, h * 256:(h + 1) * 256], wuv_ref[h]).astype(BF16) for h in range(MLA_HEADS)]
    y_ref[...] = x_ref[...] + _dot(jnp.concatenate(heads, axis=1), wo_ref[...])


def _mla_out(o_lat, wuv, wo, x):
    m, d = x.shape
    tm = min(m, 512)
    return pl.pallas_call(
        _mla_out_kernel,
        out_shape=jax.ShapeDtypeStruct((m, d), F32),
        grid=(m // tm,),
        in_specs=[
            pl.BlockSpec((tm, 2048), lambda i: (i, 0)),
            pl.BlockSpec((MLA_HEADS, 256, 128), lambda i: (0, 0, 0)),
            pl.BlockSpec((1024, d), lambda i: (0, 0)),
            pl.BlockSpec((tm, d), lambda i: (i, 0)),
        ],
        out_specs=pl.BlockSpec((tm, d), lambda i: (i, 0)),
        compiler_params=_params(("parallel",)),
        name="mla_out",
    )(o_lat, wuv, wo, x)


def _decode_mla_kernel(pt_ref, q_ref, kn_ref, ckv_hbm, kpe_hbm, o_ref,
                       cbuf, pbuf, kb_scr, sem, *, layer, n_pages):
    b = pl.program_id(0)
    slot = b % 2

    def copies(tok, sl, p):
        page = pt_ref[tok * n_pages + p]
        return (pltpu.make_async_copy(ckv_hbm.at[layer, page], cbuf.at[sl, p], sem.at[0, sl]),
                pltpu.make_async_copy(kpe_hbm.at[layer, page], pbuf.at[sl, p], sem.at[1, sl]))

    def issue(tok, sl):
        def body(p, carry):
            for cp in copies(tok, sl, p):
                cp.start()
            return carry
        lax.fori_loop(0, n_pages, body, 0)

    @pl.when(b == 0)
    def _():
        issue(0, 0)

    @pl.when(b + 1 < pl.num_programs(0))
    def _():
        issue(b + 1, 1 - slot)

    def wait_body(p, carry):
        for cp in copies(b, slot, p):
            cp.wait()
        return carry
    lax.fori_loop(0, n_pages, wait_body, 0)

    n_keys = n_pages * PAGE_SIZE
    kb_scr[...] = cbuf[slot].reshape(n_keys, MLA_KV_LORA).astype(BF16)

    q = q_ref[0]
    kn = kn_ref[0].astype(F32)
    s_self = jnp.sum(q.astype(F32) * kn, axis=-1, keepdims=True)
    q_pe = q[:, 256:320]
    kpe_t = jnp.concatenate([pbuf[slot, p].astype(BF16) for p in range(n_pages)], axis=1)
    s = _dot_nt(q[:, :256], kb_scr[...]) + _dot(q_pe, kpe_t)
    m = jnp.maximum(jnp.max(s, axis=-1, keepdims=True), s_self)
    p = jnp.exp2(s - m)
    p_self = jnp.exp2(s_self - m)
    l = jnp.sum(p, axis=-1, keepdims=True) + p_self
    acc = _dot(p.astype(BF16), kb_scr[...]) + p_self * kn[:, :256]
    o_ref[0] = acc / l


def _decode_mla(page_table, q_tok, k_new, cache_ckv, cache_kpe, layer):
    nb, n_pages = page_table.shape
    n_keys = n_pages * PAGE_SIZE
    return pl.pallas_call(
        functools.partial(_decode_mla_kernel, layer=layer, n_pages=n_pages),
        out_shape=jax.ShapeDtypeStruct((nb, MLA_HEADS, MLA_KV_LORA), F32),
        grid_spec=pltpu.PrefetchScalarGridSpec(
            num_scalar_prefetch=1,
            grid=(nb,),
            in_specs=[
                pl.BlockSpec((1, MLA_HEADS, MLA_QK), lambda b, pt: (b, 0, 0)),
                pl.BlockSpec((1, 1, MLA_QK), lambda b, pt: (b, 0, 0)),
                pl.BlockSpec(memory_space=pl.ANY),
                pl.BlockSpec(memory_space=pl.ANY),
            ],
            out_specs=pl.BlockSpec((1, MLA_HEADS, MLA_KV_LORA), lambda b, pt: (b, 0, 0)),
            scratch_shapes=[
                pltpu.VMEM((2, n_pages, PAGE_SIZE, MLA_KV_LORA), F32),
                pltpu.VMEM((2, n_pages, MLA_ROPE, PAGE_SIZE), F32),
                pltpu.VMEM((n_keys, MLA_KV_LORA), BF16),
                pltpu.SemaphoreType.DMA((2, 2)),
            ],
        ),
        compiler_params=_params(("arbitrary",)),
        name="decode_mla",
    )(page_table.reshape(-1), q_tok, k_new.reshape(nb, 1, MLA_QK), cache_ckv, cache_kpe)


def _diff_prep_kernel(qkv_ref, cos_ref, sin_ref, q_ref, k32_ref, kb_ref, v32_ref, vb_ref):
    qkv = qkv_ref[...]
    qk = _rope(qkv[:, :1536], cos_ref[...], sin_ref[...])
    q_ref[...] = (qk[:, :1024] * (DIFF_SCALE * LOG2E)).astype(BF16)
    k = qk[:, 1024:]
    v = qkv[:, 1536:]
    k32_ref[...] = k
    kb_ref[...] = k.astype(BF16)
    v32_ref[...] = v
    vb_ref[...] = v.astype(BF16)


def _diff_prep(qkv, cos, sin):
    m = qkv.shape[0]
    tm = min(m, 512)
    nt = cos.shape[0] // tm
    row = lambda i: (i, 0)
    return pl.pallas_call(
        _diff_prep_kernel,
        out_shape=(
            jax.ShapeDtypeStruct((m, 1024), BF16),
            jax.ShapeDtypeStruct((m, 512), F32),
            jax.ShapeDtypeStruct((m, 512), BF16),
            jax.ShapeDtypeStruct((m, 512), F32),
            jax.ShapeDtypeStruct((m, 512), BF16),
        ),
        grid=(m // tm,),
        in_specs=[
            pl.BlockSpec((tm, 2048), row),
            pl.BlockSpec((tm, LANES), lambda i: (i % nt, 0)),
            pl.BlockSpec((tm, LANES), lambda i: (i % nt, 0)),
        ],
        out_specs=(
            pl.BlockSpec((tm, 1024), row),
            pl.BlockSpec((tm, 512), row),
            pl.BlockSpec((tm, 512), row),
            pl.BlockSpec((tm, 512), row),
            pl.BlockSpec((tm, 512), row),
        ),
        compiler_params=_params(("parallel",)),
        name="diff_prep",
    )(qkv, cos, sin)


def _diff_lambda(lq1_ref, lk1_ref, lq2_ref, lk2_ref, lam_init):
    e1 = jnp.exp(jnp.sum(lq1_ref[...] * lk1_ref[...], axis=-1, keepdims=True))
    e2 = jnp.exp(jnp.sum(lq2_ref[...] * lk2_ref[...], axis=-1, keepdims=True))
    return e1 - e2 + lam_init


def _flash_diff_kernel(lq1_ref, lk1_ref, lq2_ref, lk2_ref, sub_ref, q_ref, k_ref, v_ref, o_ref,
                       m_scr, acc_scr, *, tile, lam_init):
    qi, ki = pl.program_id(2), pl.program_id(3)

    @pl.when(ki == 0)
    def _():
        m_scr[...] = jnp.full_like(m_scr, NEG_INF)
        acc_scr[...] = jnp.zeros_like(acc_scr)

    def all_rows(masked):
        q = q_ref[...]
        k = k_ref[...]
        v_ext = jnp.concatenate([v_ref[...], jnp.ones((tile, LANES), BF16)], axis=1)
        comp0 = lax.broadcasted_iota(jnp.int32, (tile, LANES), 1) < DIFF_HEAD_DIM
        zero = jnp.zeros((tile, LANES), BF16)
        for r in range(2):
            head = q[:, r * LANES:(r + 1) * LANES]
            for c in range(2):
                qc = jnp.where(comp0, head, zero) if c == 0 else jnp.where(comp0, zero, head)
                row0 = (2 * r + c) * tile
                p = _online_softmax(_dot_nt(qc, k), row0, m_scr, None, acc_scr, masked)
                acc_scr[row0:row0 + tile, :] += _dot(p, v_ext)

    @pl.when(ki < qi)
    def _():
        all_rows(False)

    @pl.when(ki == qi)
    def _():
        all_rows(True)
        lam = _diff_lambda(lq1_ref, lk1_ref, lq2_ref, lk2_ref, lam_init)
        g = sub_ref[...]
        for r in range(2):
            o = []
            for c in range(2):
                rows = slice((2 * r + c) * tile, (2 * r + c + 1) * tile)
                o.append(acc_scr[rows, :LANES] / acc_scr[rows, LANES:])
            d = _rms(o[0] - lam * o[1], g) * (1.0 - lam_init)
            o_ref[:, r * LANES:(r + 1) * LANES] = d.astype(o_ref.dtype)


def _flash_diff(q, k, v, lams, subln, lam_init, batch, seq, tile=512):
    nt = seq // tile

    def kv_map(b, g, qi, ki):
        return (b * nt + jnp.minimum(ki, qi), g)

    vec = pl.BlockSpec((1, DIFF_HEAD_DIM), lambda b, g, qi, ki: (0, 0))
    return pl.pallas_call(
        functools.partial(_flash_diff_kernel, tile=tile, lam_init=lam_init),
        out_shape=jax.ShapeDtypeStruct((batch * seq, 1024), BF16),
        grid=(batch, DIFF_GROUPS, nt, nt),
        in_specs=[
            vec, vec, vec, vec,
            pl.BlockSpec((1, LANES), lambda b, g, qi, ki: (0, 0)),
            pl.BlockSpec((tile, 256), lambda b, g, qi, ki: (b * nt + qi, g)),
            pl.BlockSpec((tile, LANES), kv_map),
            pl.BlockSpec((tile, LANES), kv_map),
        ],
        out_specs=pl.BlockSpec((tile, 256), lambda b, g, qi, ki: (b * nt + qi, g)),
        scratch_shapes=[
            pltpu.VMEM((4 * tile, LANES), F32),
            pltpu.VMEM((4 * tile, 2 * LANES), F32),
        ],
        compiler_params=_params(("parallel", "parallel", "parallel", "arbitrary")),
        name="flash_diff",
    )(*[a.reshape(1, DIFF_HEAD_DIM) for a in lams], subln.reshape(1, LANES), q, k, v)


def _decode_diff_kernel(pt_ref, lq1_ref, lk1_ref, lq2_ref, lk2_ref, sub_ref, q_ref, kn_ref, vn_ref,
                        k_hbm, v_hbm, o_ref, kbuf, vbuf, sem, m_scr, l_scr, acc_scr,
                        *, layer, n_pages, unit, lam_init):
    b, u = pl.program_id(0), pl.program_id(1)
    n_units = n_pages // unit
    step = b * n_units + u
    slot = step % 2

    def copies(st, sl, p):
        tok, un = st // n_units, st % n_units
        page = pt_ref[tok * n_pages + un * unit + p]
        return (pltpu.make_async_copy(k_hbm.at[layer, page], kbuf.at[sl, p], sem.at[0, sl]),
                pltpu.make_async_copy(v_hbm.at[layer, page], vbuf.at[sl, p], sem.at[1, sl]))

    def issue(st, sl):
        def body(p, carry):
            for cp in copies(st, sl, p):
                cp.start()
            return carry
        lax.fori_loop(0, unit, body, 0)

    @pl.when(step == 0)
    def _():
        issue(0, 0)

    @pl.when(step + 1 < pl.num_programs(0) * n_units)
    def _():
        issue(step + 1, 1 - slot)

    def wait_body(p, carry):
        for cp in copies(step, slot, p):
            cp.wait()
        return carry
    lax.fori_loop(0, unit, wait_body, 0)

    q = q_ref[0]

    @pl.when(u == 0)
    def _():
        m_scr[...] = jnp.sum(q.astype(F32) * kn_ref[0].astype(F32), axis=-1, keepdims=True)
        l_scr[...] = jnp.ones_like(l_scr)
        acc_scr[...] = vn_ref[0]

    kt = jnp.concatenate([kbuf[slot, pg].astype(BF16) for pg in range(unit)], axis=1)
    s = _dot(q, kt)
    m_old = m_scr[...]
    m_new = jnp.maximum(m_old, jnp.max(s, axis=-1, keepdims=True))
    alpha = jnp.exp2(m_old - m_new)
    p = jnp.exp2(s - m_new)
    l_scr[...] = alpha * l_scr[...] + jnp.sum(p, axis=-1, keepdims=True)
    m_scr[...] = m_new
    pb = p.astype(BF16)
    halves = []
    for gp in range(DIFF_GROUPS // 2):
        out = jnp.zeros((16, 2 * LANES), F32)
        for pp in range(unit // 2):
            stacked = []
            for pg in (2 * pp, 2 * pp + 1):
                side = [vbuf[slot, pg, pl.ds(2 * gp + i, PAGE_SIZE, stride=DIFF_GROUPS), :] for i in range(2)]
                stacked.append(jnp.concatenate(side, axis=1).astype(BF16))
            out = out + _dot(pb[:, 2 * pp * PAGE_SIZE:(2 * pp + 2) * PAGE_SIZE], jnp.concatenate(stacked, axis=0))
        halves += [out[:, :LANES], out[:, LANES:]]
    row_g = lax.broadcasted_iota(jnp.int32, (16, LANES), 0) % DIFF_GROUPS
    pv = jnp.where(row_g == 0, halves[0], jnp.where(row_g == 1, halves[1],
                                                    jnp.where(row_g == 2, halves[2], halves[3])))
    acc_scr[...] = alpha * acc_scr[...] + pv

    @pl.when(u == n_units - 1)
    def _():
        lam = _diff_lambda(lq1_ref, lk1_ref, lq2_ref, lk2_ref, lam_init)
        on = acc_scr[...] / l_scr[...]
        d = on[0:8] - lam * on[8:16]
        o_ref[0] = _rms(d, sub_ref[...]) * (1.0 - lam_init)


def _decode_diff(page_table, q_rows, k_new, v_rows, cache_kt, cache_v, lams, subln, lam_init, layer, unit=16):
    nb, n_pages = page_table.shape
    vec = pl.BlockSpec((1, DIFF_HEAD_DIM), lambda b, u, pt: (0, 0))
    return pl.pallas_call(
        functools.partial(_decode_diff_kernel, layer=layer, n_pages=n_pages, unit=unit, lam_init=lam_init),
        out_shape=jax.ShapeDtypeStruct((nb, 8, LANES), F32),
        grid_spec=pltpu.PrefetchScalarGridSpec(
            num_scalar_prefetch=1,
            grid=(nb, n_pages // unit),
            in_specs=[
                vec, vec, vec, vec,
                pl.BlockSpec((1, LANES), lambda b, u, pt: (0, 0)),
                pl.BlockSpec((1, 16, 512), lambda b, u, pt: (b, 0, 0)),
                pl.BlockSpec((1, 1, 512), lambda b, u, pt: (b, 0, 0)),
                pl.BlockSpec((1, 16, LANES), lambda b, u, pt: (b, 0, 0)),
                pl.BlockSpec(memory_space=pl.ANY),
                pl.BlockSpec(memory_space=pl.ANY),
            ],
            out_specs=pl.BlockSpec((1, 8, LANES), lambda b, u, pt: (b, 0, 0)),
            scratch_shapes=[
                pltpu.VMEM((2, unit, 512, PAGE_SIZE), F32),
                pltpu.VMEM((2, unit, 512, LANES), F32),
                pltpu.SemaphoreType.DMA((2, 2)),
                pltpu.VMEM((16, 1), F32),
                pltpu.VMEM((16, 1), F32),
                pltpu.VMEM((16, LANES), F32),
            ],
        ),
        compiler_params=_params(("arbitrary", "arbitrary")),
        name="decode_diff",
    )(page_table.reshape(-1), *[a.reshape(1, DIFF_HEAD_DIM) for a in lams], subln.reshape(1, LANES),
      q_rows, k_new.reshape(nb, 1, 512), v_rows, cache_kt, cache_v)


def _ssd_gate_norm(y, xs, z, d_ref, nw_ref):
    yg = (y + d_ref[...] * xs) * _silu(z)
    w = SSD_D_INNER // SSD_GROUPS
    parts = []
    for g in range(SSD_GROUPS):
        part = yg[:, g * w:(g + 1) * w]
        parts.append(part * lax.rsqrt(jnp.mean(part * part, axis=-1, keepdims=True) + RMS_EPS))
    return jnp.concatenate(parts, axis=1) * nw_ref[...]


def _ssd_prompt_kernel(z_ref, xbc_ref, dt_ref, cw_ref, cb_ref, dtb_ref, a_ref, e_ref, d_ref, nw_ref,
                       yn_ref, st_ref, buf_ref, xp_scr, s_scr, *, chunk):
    c = pl.program_id(1)

    @pl.when(c == 0)
    def _():
        xp_scr[0:8, :] = jnp.zeros((8, SSD_CONV_DIM), F32)
        s_scr[...] = jnp.zeros_like(s_scr)

    xp_scr[8:8 + chunk, :] = xbc_ref[...]
    conv = cb_ref[...] + sum(xp_scr[5 + k:5 + k + chunk, :] * cw_ref[k:k + 1, :] for k in range(SSD_CONV))
    u = _silu(conv)
    xs = u[:, :SSD_D_INNER]
    bm = u[:, SSD_D_INNER:SSD_D_INNER + 512]
    cm = u[:, SSD_D_INNER + 512:]

    dt = jax.nn.softplus(dt_ref[...] + dtb_ref[...])
    a = dt * a_ref[...]
    row = lax.broadcasted_iota(jnp.int32, (chunk, chunk), 0)
    col = lax.broadcasted_iota(jnp.int32, (chunk, chunk), 1)
    causal = col <= row
    a_cs = _dot_exact(causal.astype(F32), a)
    a_cs_t = a_cs.T
    a_last = a_cs[chunk - 1:chunk, :]
    ex = _dot_exact(jnp.concatenate([dt, jnp.exp(a_cs), jnp.exp(a_last - a_cs)], axis=0), e_ref[...])
    dtx, ecs, dend = ex[:chunk], ex[chunk:2 * chunk], ex[2 * chunk:]
    xdt = xs * dtx
    xdt_b = xdt.astype(BF16)
    xdtd_b = (xdt * dend).astype(BF16)
    etot = ecs[chunk - 1:chunk, :]
    low_head = lax.broadcasted_iota(jnp.int32, (chunk, LANES), 1) < SSD_HEAD_DIM

    ys = []
    for g in range(SSD_GROUPS):
        bg = bm[:, g * 128:(g + 1) * 128]
        cg = cm[:, g * 128:(g + 1) * 128].astype(BF16)
        cb = _dot_nt(cg, bg.astype(BF16))
        bg_t = bg.T.astype(BF16)
        for pi in range(4):
            pair = g * 4 + pi
            sl = slice(pair * LANES, (pair + 1) * LANES)
            ms = []
            for h in (2 * pair, 2 * pair + 1):
                seg = a_cs[:, h:h + 1] - a_cs_t[h:h + 1, :]
                ms.append((cb * jnp.exp(jnp.where(causal, seg, NEG_INF))).astype(BF16))
            y_diag = jnp.where(low_head, _dot(ms[0], xdt_b[:, sl]), _dot(ms[1], xdt_b[:, sl]))
            st = s_scr[:, sl]
            y_off = _dot(cg, st.astype(BF16)) * ecs[:, sl]
            s_scr[:, sl] = etot[:, sl] * st + _dot(bg_t, xdtd_b[:, sl])
            ys.append(y_diag + y_off)

    y = jnp.concatenate(ys, axis=1)
    yn_ref[...] = _ssd_gate_norm(y, xs, z_ref[...], d_ref, nw_ref).astype(yn_ref.dtype)
    xp_scr[0:8, :] = xp_scr[chunk:chunk + 8, :]

    @pl.when(c == pl.num_programs(1) - 1)
    def _():
        st_ref[0] = s_scr[...].T
        buf_ref[0] = xp_scr[chunk + 5:chunk + 8, :]


def _ssd_prompt(z, xbc, dt, cw, cb, dtb, a_neg, expand, d_exp, nw, batch, seq, chunk=SSD_CHUNK):
    nc = seq // chunk
    row = lambda b, c: (b * nc + c, 0)
    const = lambda b, c: (0, 0)
    return pl.pallas_call(
        functools.partial(_ssd_prompt_kernel, chunk=chunk),
        out_shape=(
            jax.ShapeDtypeStruct((batch * seq, SSD_D_INNER), BF16),
            jax.ShapeDtypeStruct((batch, SSD_D_INNER, SSD_STATE), F32),
            jax.ShapeDtypeStruct((batch, SSD_CONV - 1, SSD_CONV_DIM), F32),
        ),
        grid=(batch, nc),
        in_specs=[
            pl.BlockSpec((chunk, SSD_D_INNER), row),
            pl.BlockSpec((chunk, SSD_CONV_DIM), row),
            pl.BlockSpec((chunk, LANES), row),
            pl.BlockSpec((SSD_CONV, SSD_CONV_DIM), const),
            pl.BlockSpec((1, SSD_CONV_DIM), const),
            pl.BlockSpec((1, LANES), const),
            pl.BlockSpec((1, LANES), const),
            pl.BlockSpec((LANES, SSD_D_INNER), const),
            pl.BlockSpec((1, SSD_D_INNER), const),
            pl.BlockSpec((1, SSD_D_INNER), const),
        ],
        out_specs=(
            pl.BlockSpec((chunk, SSD_D_INNER), row),
            pl.BlockSpec((1, SSD_D_INNER, SSD_STATE), lambda b, c: (b, 0, 0)),
            pl.BlockSpec((1, SSD_CONV - 1, SSD_CONV_DIM), lambda b, c: (b, 0, 0)),
        ),
        scratch_shapes=[
            pltpu.VMEM((chunk + 8, SSD_CONV_DIM), F32),
            pltpu.VMEM((SSD_STATE, SSD_D_INNER), F32),
        ],
        compiler_params=_params(("parallel", "arbitrary")),
        name="ssd_prompt",
    )(z, xbc, dt, cw, cb, dtb, a_neg, expand, d_exp, nw)


def _ssd_decode_pre_kernel(xbc_ref, dt_ref, buf_ref, cw_ref, cb_ref, dtb_ref, a_ref, e_ref,
                           xs_ref, xdt_ref, dec_ref, b_ref, c_ref, nbuf_ref):
    xbc = xbc_ref[...]
    conv = cb_ref[...] + xbc * cw_ref[3:4, :]
    for k in range(SSD_CONV - 1):
        conv = conv + buf_ref[k] * cw_ref[k:k + 1, :]
    u = _silu(conv)
    xs = u[:, :SSD_D_INNER]
    dt = jax.nn.softplus(dt_ref[...] + dtb_ref[...])
    ex = _dot_exact(jnp.concatenate([dt, dt * a_ref[...]], axis=0), e_ref[...])
    t = xbc.shape[0]
    xs_ref[...] = xs
    xdt_ref[...] = xs * ex[:t]
    dec_ref[...] = jnp.exp(ex[t:])
    b_ref[...] = u[:, SSD_D_INNER:SSD_D_INNER + 512]
    c_ref[...] = u[:, SSD_D_INNER + 512:]
    nbuf_ref[0] = buf_ref[1]
    nbuf_ref[1] = buf_ref[2]
    nbuf_ref[2] = xbc


def _ssd_decode_pre(xbc, dt, buf, cw, cb, dtb, a_neg, expand):
    t = xbc.shape[0]
    wide = jax.ShapeDtypeStruct((t, SSD_D_INNER), F32)
    grp = jax.ShapeDtypeStruct((t, 512), F32)
    return pl.pallas_call(
        _ssd_decode_pre_kernel,
        out_shape=(wide, wide, wide, grp, grp, jax.ShapeDtypeStruct((SSD_CONV - 1, t, SSD_CONV_DIM), F32)),
        compiler_params=pltpu.CompilerParams(vmem_limit_bytes=VMEM_LIMIT),
        name="ssd_decode_pre",
    )(xbc, dt, buf, cw, cb, dtb, a_neg, expand)


def _split_bf16x3(x):
    hi = x.astype(BF16).astype(F32)
    mid = (x - hi).astype(BF16).astype(F32)
    lo = (x - hi - mid).astype(BF16).astype(F32)
    return hi, mid, lo


def _ssd_decode_state_kernel(xdt_ref, dec_ref, b_ref, c_ref, st_ref, y_ref, nst_ref):
    xdt = xdt_ref[0]
    lane_g = lax.broadcasted_iota(jnp.int32, (8, SSD_D_INNER), 1) // 512
    row = lax.broadcasted_iota(jnp.int32, (8, SSD_D_INNER), 0)
    lhs = jnp.where(row == lane_g, xdt, 0.0) + jnp.where(row == 4, dec_ref[0], 0.0)
    brow = lax.broadcasted_iota(jnp.int32, (8, LANES), 0)
    bmat = jnp.zeros((8, LANES), F32)
    cmat = jnp.zeros((8, LANES), F32)
    for g in range(SSD_GROUPS):
        bmat = jnp.where(brow == g, b_ref[0][:, g * 128:(g + 1) * 128], bmat)
        cmat = jnp.where(brow == g, c_ref[0][:, g * 128:(g + 1) * 128], cmat)
    rhs = jnp.concatenate([bmat, (brow == 4).astype(F32)], axis=1)
    lh, lm, ll = _split_bf16x3(lhs)
    rh, rm, rl = _split_bf16x3(rhs)
    lhs6 = jnp.concatenate([lh, lh, lm, lm, lh, ll], axis=0).astype(BF16)
    rhs6 = jnp.concatenate([rh, rm, rh, rm, rl, rh], axis=0).astype(BF16)
    both = lax.dot_general(lhs6, rhs6, TN_DIMS, preferred_element_type=F32)
    new = st_ref[0] * both[:, LANES:] + both[:, :LANES]
    nst_ref[0] = new
    yg = _dot_nt(cmat.astype(BF16), new.astype(BF16))
    y_ref[0] = jnp.sum(jnp.where(row == lane_g, yg, 0.0), axis=0, keepdims=True)


def _ssd_decode_state(xdt, dec, bm, cm, state):
    t = xdt.shape[0]
    tok3 = lambda w: pl.BlockSpec((1, 1, w), lambda i: (i, 0, 0))
    st_spec = pl.BlockSpec((1, SSD_D_INNER, SSD_STATE), lambda i: (i, 0, 0))
    return pl.pallas_call(
        _ssd_decode_state_kernel,
        out_shape=(jax.ShapeDtypeStruct((t, 1, SSD_D_INNER), F32),
                   jax.ShapeDtypeStruct((t, SSD_D_INNER, SSD_STATE), F32)),
        grid=(t,),
        in_specs=[tok3(SSD_D_INNER), tok3(SSD_D_INNER), tok3(512), tok3(512), st_spec],
        out_specs=(tok3(SSD_D_INNER), st_spec),
        compiler_params=_params(("parallel",)),
        name="ssd_decode_state",
    )(xdt.reshape(t, 1, -1), dec.reshape(t, 1, -1), bm.reshape(t, 1, -1), cm.reshape(t, 1, -1), state)


def _ssd_decode_post_kernel(y_ref, xs_ref, z_ref, d_ref, nw_ref, o_ref):
    o_ref[...] = _ssd_gate_norm(y_ref[...], xs_ref[...], z_ref[...], d_ref, nw_ref).astype(o_ref.dtype)


def _ssd_decode_post(y, xs, z, d_exp, nw):
    return pl.pallas_call(
        _ssd_decode_post_kernel,
        out_shape=jax.ShapeDtypeStruct(y.shape, BF16),
        compiler_params=pltpu.CompilerParams(vmem_limit_bytes=VMEM_LIMIT),
        name="ssd_decode_post",
    )(y, xs, z, d_exp, nw)


def _xa_prompt_kernel(x_ref, g_ref, wq_ref, k_ref, v_ref, wo_ref, o_ref):
    x = x_ref[...]
    q = _dot(_rms(x, g_ref[...]).astype(BF16), wq_ref[...]).astype(BF16)
    k = k_ref[0].astype(BF16)
    v = v_ref[0].astype(BF16)
    heads = []
    for h in range(XA_HEADS):
        sl = slice(h * XA_HEAD_DIM, (h + 1) * XA_HEAD_DIM)
        s = _dot_nt(q[:, sl], k[:, sl]) * XA_SCALE
        p = jnp.exp(s - jnp.max(s, axis=-1, keepdims=True))
        o = _dot(p.astype(BF16), v[:, sl]) / jnp.sum(p, axis=-1, keepdims=True)
        heads.append(o.astype(BF16))
    o_ref[...] = x + _dot(jnp.concatenate(heads, axis=1), wo_ref[...])


def _xa_prompt(x, gain, wq, kv_mem, wo, batch, seq):
    m, d = x.shape
    tm = 512
    nt = seq // tm
    mem = kv_mem.shape[1]
    return pl.pallas_call(
        _xa_prompt_kernel,
        out_shape=jax.ShapeDtypeStruct((m, d), F32),
        grid=(m // tm,),
        in_specs=[
            pl.BlockSpec((tm, d), lambda i: (i, 0)),
            pl.BlockSpec((1, d), lambda i: (0, 0)),
            pl.BlockSpec((d, d), lambda i: (0, 0)),
            pl.BlockSpec((1, mem, d), lambda i: (i // nt, 0, 0)),
            pl.BlockSpec((1, mem, d), lambda i: (i // nt, 0, 1)),
            pl.BlockSpec((d, d), lambda i: (0, 0)),
        ],
        out_specs=pl.BlockSpec((tm, d), lambda i: (i, 0)),
        compiler_params=_params(("parallel",)),
        name="xa_prompt",
    )(x, gain.reshape(1, d), wq, kv_mem, kv_mem, wo)


def _xa_decode_kernel(q_ref, k_ref, v_ref, o_ref, *, tokens):
    n = k_ref.shape[2]
    lane = lax.broadcasted_iota(jnp.int32, (8, n), 1)
    own = lane % 8 == lax.broadcasted_iota(jnp.int32, (8, n), 0)
    first_half = lane % 8 < XA_HEADS
    strides = [8 << i for i in range((n // 8).bit_length() - 1)]

    def over_tokens(op, x):
        for sh in strides:
            x = op(x, pltpu.roll(x, sh, 1))
        return x

    for t in range(tokens):
        g = _dot_nt(q_ref[0, t].astype(BF16), k_ref[0, t].astype(BF16))
        part = jnp.broadcast_to(jnp.sum(jnp.where(own, g, 0.0), axis=0, keepdims=True), (8, n))
        s = (part + pltpu.roll(part, n - XA_HEADS, 1)) * XA_SCALE
        p = jnp.exp(s - over_tokens(jnp.maximum, s))
        p = p / over_tokens(jnp.add, p)
        p = jnp.where(first_half, p, pltpu.roll(p, XA_HEADS, 1))
        o_ref[0, t] = _dot(jnp.where(own, p, 0.0).astype(BF16), v_ref[0, t].astype(BF16))


def _xa_decode(q, mem_k, mem_v, layer, tokens=4):
    t, d = q.shape
    rows = mem_k.shape[2]
    half = XA_HEAD_DIM // 2
    q8 = jnp.transpose(q.reshape(t, XA_HEADS, 2, half), (0, 2, 1, 3)).reshape(t // tokens, tokens, 8, half)
    kv_spec = pl.BlockSpec((1, tokens, rows, half), lambda i: (layer, i, 0, 0))
    tok_spec = pl.BlockSpec((1, tokens, 8, half), lambda i: (i, 0, 0, 0))
    o8 = pl.pallas_call(
        functools.partial(_xa_decode_kernel, tokens=tokens),
        out_shape=jax.ShapeDtypeStruct((t // tokens, tokens, 8, half), F32),
        grid=(t // tokens,),
        in_specs=[tok_spec, kv_spec, kv_spec],
        out_specs=tok_spec,
        compiler_params=_params(("parallel",)),
        name="xa_decode",
    )(q8, mem_k, mem_v)
    return jnp.transpose(o8.reshape(t, 2, XA_HEADS, half), (0, 2, 1, 3)).reshape(t, d)


def kernel(x_prompt, x_sample, cache_mla_ckv, cache_mla_kpe, cache_diff_k, cache_diff_v, state_ssm, state_conv, cache_mem_k, cache_mem_v, page_table, mem_prompt, norm_mix, norm_xa, norm_ffn, norm_final, xa_mem_norm, xa_wq, xa_wk, xa_wv, xa_wo, ffn_w_gate, ffn_w_up, ffn_w_down, mla_wq_a, mla_q_norm, mla_wq_b, mla_wkv_a, mla_kv_norm, mla_w_uk, mla_w_uv, mla_wo, diff_wq, diff_wk, diff_wv, diff_lambda_q1, diff_lambda_k1, diff_lambda_q2, diff_lambda_k2, diff_subln, diff_wo, ssd_w_in, ssd_conv_w, ssd_conv_b, ssd_dt_bias, ssd_A_log, ssd_D, ssd_norm, ssd_w_out):
    batch, seq, d = x_prompt.shape
    nb = x_sample.shape[0]
    depth = norm_mix.shape[0]
    n_mix = 3
    bf = lambda w: w.astype(BF16)

    n_a = mla_wq_a.shape[0]
    wkv_a = jnp.pad(mla_wkv_a, ((0, 0), (0, 0), (0, MLA_QK - mla_wkv_a.shape[2])))
    w_qkv_a = bf(jnp.concatenate([mla_wq_a, wkv_a], axis=2))
    wq_b = mla_wq_b.reshape(n_a, -1, MLA_HEADS, MLA_NOPE + MLA_ROPE)
    wq_b = bf(jnp.concatenate([
        wq_b[..., :MLA_NOPE].reshape(n_a, -1, MLA_HEADS * MLA_NOPE),
        jnp.pad(wq_b[..., MLA_NOPE:], ((0, 0), (0, 0), (0, 0), (0, 64))).reshape(n_a, -1, MLA_HEADS * 128),
    ], axis=2))
    w_uk = bf(jnp.transpose(mla_w_uk, (0, 2, 3, 1)))
    w_uv = bf(jnp.transpose(mla_w_uv, (0, 2, 1, 3)))
    w_mla_o = bf(mla_wo)
    w_diff_qkv = bf(jnp.concatenate([diff_wq, diff_wk, diff_wv], axis=2))
    w_diff_o = bf(diff_wo)
    w_ssd_z = bf(ssd_w_in[:, :, :SSD_D_INNER])
    w_ssd_xbc = bf(ssd_w_in[:, :, SSD_D_INNER:SSD_D_INNER + SSD_CONV_DIM])
    w_ssd_dt = bf(jnp.pad(ssd_w_in[:, :, SSD_D_INNER + SSD_CONV_DIM:], ((0, 0), (0, 0), (0, LANES - SSD_HEADS))))
    w_ssd_out = bf(ssd_w_out)
    pad_heads = lambda a: jnp.pad(a, ((0, 0), (0, LANES - SSD_HEADS)))
    ssd_dtb = pad_heads(ssd_dt_bias)
    ssd_a = pad_heads(-jnp.exp(ssd_A_log.astype(F32)))
    ssd_d_exp = jnp.repeat(ssd_D, SSD_HEAD_DIM, axis=1)
    expand = (jnp.arange(LANES)[:, None] == jnp.arange(SSD_D_INNER)[None, :] // SSD_HEAD_DIM).astype(F32)
    w_xa_q, w_xa_o = bf(xa_wq), bf(xa_wo)
    w_xa_kv = bf(jnp.concatenate([xa_wk, xa_wv], axis=2))
    w_gate, w_up, w_down = bf(ffn_w_gate), bf(ffn_w_up), bf(ffn_w_down)

    def mem_view(c):
        mt = c.shape[2]
        c = c.reshape(depth, nb, mt, XA_HEADS, 2, XA_HEAD_DIM // 2)
        return jnp.transpose(c, (0, 1, 2, 4, 3, 5)).reshape(depth, nb, mt * 8, XA_HEAD_DIM // 2)

    mem_k, mem_v = mem_view(cache_mem_k), mem_view(cache_mem_v)
    n_pool = cache_diff_k.shape[1]
    pool_kpe_t = jnp.swapaxes(cache_mla_kpe, 2, 3)
    pool_kt = jnp.transpose(cache_diff_k, (0, 1, 3, 4, 5, 2)).reshape(-1, n_pool, 512, PAGE_SIZE)
    pool_v = cache_diff_v.reshape(-1, n_pool, PAGE_SIZE * DIFF_GROUPS, 2 * DIFF_HEAD_DIM)
    mem_flat = mem_prompt.reshape(-1, d)

    def run(x, pos_tab, prompt):
        m = x.shape[0]
        cos, sin = _rope_tables(pos_tab)
        new = {k: [] for k in ("mla_ckv", "mla_kpe", "diff_k", "diff_v", "ssm", "conv", "mem_k", "mem_v")}
        for i in range(depth):
            kind, j = i % n_mix, i // n_mix
            g_mix = norm_mix[i]
            if kind == 0:
                qkv_a = _linear(x, w_qkv_a[j], gain=g_mix, name="mla_qkv_a")
                q = _linear(qkv_a, wq_b[j], gain=mla_q_norm[j], name="mla_q_b")
                qcat, kcat, ckv, kpe = _mla_prep(q, qkv_a, cos, sin, w_uk[j], mla_kv_norm[j])
                if prompt:
                    o_lat = _flash_mla(qcat, kcat, batch, seq)
                else:
                    o_lat = _decode_mla(page_table, jnp.transpose(qcat, (1, 0, 2)), kcat,
                                        cache_mla_ckv, pool_kpe_t, j).reshape(m, -1)
                x = _mla_out(o_lat, w_uv[j], w_mla_o[j], x)
                new["mla_ckv"].append(ckv)
                new["mla_kpe"].append(kpe)
            elif kind == 1:
                lam_init = 0.8 - 0.6 * math.exp(-0.3 * i)
                lams = (diff_lambda_q1[j], diff_lambda_k1[j], diff_lambda_q2[j], diff_lambda_k2[j])
                qkv = _linear(x, w_diff_qkv[j], gain=g_mix, name="diff_qkv")
                q_b, k32, k_b, v32, v_b = _diff_prep(qkv, cos, sin)
                if prompt:
                    o = _flash_diff(q_b, k_b, v_b, lams, diff_subln[j], lam_init, batch, seq)
                else:
                    q5 = q_b.reshape(m, DIFF_GROUPS, 2, 2, DIFF_HEAD_DIM)
                    eye_g = jnp.eye(DIFF_GROUPS, dtype=BF16)
                    eye_c = jnp.eye(2, dtype=BF16)
                    q_rows = jnp.einsum("tgrcd,gG,cC->tcrgGCd", q5, eye_g, eye_c).reshape(m, 16, 512)
                    v_rows = jnp.tile(v_b.astype(F32).reshape(m, DIFF_GROUPS, LANES), (1, 4, 1))
                    o = _decode_diff(page_table, q_rows, k_b, v_rows, pool_kt, pool_v, lams, diff_subln[j],
                                     lam_init, j)
                    o = jnp.transpose(o.reshape(m, 2, DIFF_GROUPS, LANES), (0, 2, 1, 3)).reshape(m, -1)
                x = _linear(o, w_diff_o[j], res=x, name="diff_o")
                new["diff_k"].append(k32)
                new["diff_v"].append(v32)
            else:
                z = _linear(x, w_ssd_z[j], gain=g_mix, name="ssd_in_z")
                xbc = _linear(x, w_ssd_xbc[j], gain=g_mix, name="ssd_in_xbc")
                dt = _linear(x, w_ssd_dt[j], gain=g_mix, name="ssd_in_dt")
                cw, cb = ssd_conv_w[j], ssd_conv_b[j].reshape(1, -1)
                dtb, a_neg = ssd_dtb[j].reshape(1, -1), ssd_a[j].reshape(1, -1)
                d_exp, nw = ssd_d_exp[j].reshape(1, -1), ssd_norm[j].reshape(1, -1)
                if prompt:
                    yn, st, buf = _ssd_prompt(z, xbc, dt, cw, cb, dtb, a_neg, expand, d_exp, nw, batch, seq)
                else:
                    xs, xdt, dec, bm, cm, nbuf = _ssd_decode_pre(
                        xbc, dt, jnp.transpose(state_conv[j], (1, 0, 2)), cw, cb, dtb, a_neg, expand)
                    y, st = _ssd_decode_state(xdt, dec, bm, cm, state_ssm[j].reshape(m, SSD_D_INNER, SSD_STATE))
                    yn = _ssd_decode_post(y.reshape(m, -1), xs, z, d_exp, nw)
                    buf = jnp.transpose(nbuf, (1, 0, 2))
                x = _linear(yn, w_ssd_out[j], res=x, name="ssd_out")
                new["ssm"].append(st.reshape(-1, SSD_HEADS, SSD_HEAD_DIM, SSD_STATE))
                new["conv"].append(buf)
            if prompt:
                kv_mem = _linear(mem_flat, w_xa_kv[i], gain=xa_mem_norm[i], name="xa_mem_kv")
                new["mem_k"].append(kv_mem[:, :d])
                new["mem_v"].append(kv_mem[:, d:])
                x = _xa_prompt(x, norm_xa[i], w_xa_q[i], kv_mem.reshape(batch, -1, 2 * d), w_xa_o[i], batch, seq)
            else:
                q = _linear(x, w_xa_q[i], gain=norm_xa[i], name="xa_q")
                o = _xa_decode(q, mem_k, mem_v, i)
                x = _linear(o, w_xa_o[i], res=x, name="xa_o")
            x = _ffn(x, norm_ffn[i], w_gate[i], w_up[i], w_down[i])
        return _final_norm(x, norm_final), {k: jnp.stack(v) for k, v in new.items() if v}

    pos_p = jnp.arange(seq, dtype=jnp.int32)
    pos_s = jnp.full((nb,), PAST_LEN, jnp.int32)
    y_p, new_p = run(x_prompt.reshape(batch * seq, d), pos_p, True)
    y_s, new_s = run(x_sample.reshape(nb, d), pos_s, False)

    n_b = new_p["diff_k"].shape[0]
    n_c = new_p["ssm"].shape[0]
    mem_t = mem_prompt.shape[1]
    return (
        y_p.reshape(batch, seq, d),
        y_s.reshape(nb, 1, d),
        new_p["mla_ckv"].reshape(n_a, batch, seq, MLA_KV_LORA),
        new_p["mla_kpe"].reshape(n_a, batch, seq, MLA_ROPE),
        new_p["diff_k"].reshape(n_b, batch, seq, DIFF_GROUPS, 2, DIFF_HEAD_DIM),
        new_p["diff_v"].reshape(n_b, batch, seq, DIFF_GROUPS, 2 * DIFF_HEAD_DIM),
        new_p["ssm"].reshape(n_c, batch, SSD_HEADS, SSD_HEAD_DIM, SSD_STATE),
        new_p["conv"],
        new_p["mem_k"].reshape(depth, batch, mem_t, XA_HEADS, XA_HEAD_DIM),
        new_p["mem_v"].reshape(depth, batch, mem_t, XA_HEADS, XA_HEAD_DIM),
        new_s["mla_ckv"].reshape(n_a, nb, 1, MLA_KV_LORA),
        new_s["mla_kpe"].reshape(n_a, nb, 1, MLA_ROPE),
        new_s["diff_k"].reshape(n_b, nb, 1, DIFF_GROUPS, 2, DIFF_HEAD_DIM),
        new_s["diff_v"].reshape(n_b, nb, 1, DIFF_GROUPS, 2 * DIFF_HEAD_DIM),
        new_s["ssm"].reshape(n_c, nb, SSD_HEADS, SSD_HEAD_DIM, SSD_STATE),
        new_s["conv"],
    )
```

```python
import functools
import math

import jax
import jax.numpy as jnp
from jax import lax
from jax.experimental import pallas as pl
from jax.experimental.pallas import tpu as pltpu

F32 = jnp.float32
BF16 = jnp.bfloat16

RMS_EPS = 1e-6
ROPE_THETA = 10000.0
NEG_INF = -1e30
PAST_LEN = 8192
PAGE_SIZE = 128

MLA_HEADS = 8
MLA_NOPE = 128
MLA_ROPE = 64
MLA_KV_LORA = 256
MLA_V = 128
MLA_QK = 384
MLA_SCALE = (MLA_NOPE + MLA_ROPE) ** -0.5

DIFF_HEADS = 8
DIFF_GROUPS = 4
DIFF_HEAD_DIM = 64
DIFF_SCALE = DIFF_HEAD_DIM ** -0.5

SSD_D_INNER = 2048
SSD_HEADS = 32
SSD_HEAD_DIM = 64
SSD_GROUPS = 4
SSD_STATE = 128
SSD_CONV = 4
SSD_CONV_DIM = SSD_D_INNER + 2 * SSD_GROUPS * SSD_STATE
SSD_CHUNK = 128

XA_HEADS = 4
XA_HEAD_DIM = 256
XA_SCALE = XA_HEAD_DIM ** -0.5

LOG2E = math.log2(math.e)
SOFTMAX_ROWS = 32
LANES = 128
VMEM_LIMIT = 48 * 1024 * 1024

NT_DIMS = (((1,), (1,)), ((), ()))
TN_DIMS = (((0,), (0,)), ((), ()))


def _params(sem):
    return pltpu.CompilerParams(dimension_semantics=sem, vmem_limit_bytes=VMEM_LIMIT)


def _rms(x, g):
    return x * lax.rsqrt(jnp.mean(x * x, axis=-1, keepdims=True) + RMS_EPS) * g


def _silu(x):
    return x * jax.nn.sigmoid(x)


def _dot(a, b):
    return jnp.dot(a, b, preferred_element_type=F32)


def _dot_nt(a, b):
    return lax.dot_general(a, b, NT_DIMS, preferred_element_type=F32)


def _dot_exact(a, b, dims=(((1,), (0,)), ((), ()))):
    return lax.dot_general(a, b, dims, precision=lax.Precision.HIGHEST, preferred_element_type=F32)


def _linear_kernel(*refs, has_norm, has_res):
    refs = list(refs)
    x_ref = refs.pop(0)
    g_ref = refs.pop(0) if has_norm else None
    w_ref = refs.pop(0)
    r_ref = refs.pop(0) if has_res else None
    o_ref, h_scr = refs

    @pl.when(pl.program_id(1) == 0)
    def _():
        x = x_ref[...].astype(F32)
        if has_norm:
            x = _rms(x, g_ref[...])
        h_scr[...] = x.astype(BF16)

    acc = _dot(h_scr[...], w_ref[...])
    if has_res:
        acc = acc + r_ref[...]
    o_ref[...] = acc.astype(o_ref.dtype)


def _pick_tn(n):
    for tn in (1024, 768, 512, 384, 256, 128):
        if n % tn == 0:
            return tn
    return n


def _linear(x, w, gain=None, res=None, out_dtype=F32, x_col=0, name="linear"):
    m = x.shape[0]
    k, n = w.shape
    tm = min(m, 1024)
    tn = _pick_tn(n)
    in_specs = [pl.BlockSpec((tm, k), lambda i, j: (i, x_col))]
    args = [x]
    if gain is not None:
        in_specs.append(pl.BlockSpec((1, k), lambda i, j: (0, 0)))
        args.append(gain.reshape(1, k).astype(F32))
    in_specs.append(pl.BlockSpec((k, tn), lambda i, j: (0, j)))
    args.append(w)
    if res is not None:
        in_specs.append(pl.BlockSpec((tm, tn), lambda i, j: (i, j)))
        args.append(res)
    return pl.pallas_call(
        functools.partial(_linear_kernel, has_norm=gain is not None, has_res=res is not None),
        out_shape=jax.ShapeDtypeStruct((m, n), out_dtype),
        grid=(m // tm, n // tn),
        in_specs=in_specs,
        out_specs=pl.BlockSpec((tm, tn), lambda i, j: (i, j)),
        scratch_shapes=[pltpu.VMEM((tm, k), BF16)],
        compiler_params=_params(("parallel", "arbitrary")),
        name=name,
    )(*args)


def _ffn_kernel(x_ref, g_ref, wg_ref, wu_ref, wd_ref, o_ref, *, chunk):
    x = x_ref[...]
    h = _rms(x, g_ref[...]).astype(BF16)
    acc = x
    for c in range(wg_ref.shape[1] // chunk):
        cols = slice(c * chunk, (c + 1) * chunk)
        a = _silu(_dot(h, wg_ref[:, cols])) * _dot(h, wu_ref[:, cols])
        acc = acc + _dot(a.astype(BF16), wd_ref[cols, :])
    o_ref[...] = acc


def _ffn(x, gain, wg, wu, wd):
    m, d = x.shape
    hid = wg.shape[1]
    tm = min(m, 1024)
    resident = lambda shape: pl.BlockSpec(shape, lambda i: (0, 0), pipeline_mode=pl.Buffered(1))
    return pl.pallas_call(
        functools.partial(_ffn_kernel, chunk=256),
        out_shape=jax.ShapeDtypeStruct((m, d), F32),
        grid=(m // tm,),
        in_specs=[
            pl.BlockSpec((tm, d), lambda i: (i, 0)),
            pl.BlockSpec((1, d), lambda i: (0, 0)),
            resident((d, hid)),
            resident((d, hid)),
            resident((hid, d)),
        ],
        out_specs=pl.BlockSpec((tm, d), lambda i: (i, 0)),
        compiler_params=_params(("parallel",)),
        name="ffn",
    )(x, gain.reshape(1, d), wg, wu, wd)


def _norm_kernel(x_ref, g_ref, o_ref):
    o_ref[...] = _rms(x_ref[...], g_ref[...])


def _final_norm(x, gain):
    m, d = x.shape
    tm = min(m, 1024)
    return pl.pallas_call(
        _norm_kernel,
        out_shape=jax.ShapeDtypeStruct((m, d), F32),
        grid=(m // tm,),
        in_specs=[pl.BlockSpec((tm, d), lambda i: (i, 0)), pl.BlockSpec((1, d), lambda i: (0, 0))],
        out_specs=pl.BlockSpec((tm, d), lambda i: (i, 0)),
        compiler_params=_params(("parallel",)),
        name="final_norm",
    )(x, gain.reshape(1, d))


def _rope(x, cos, sin):
    n = x.shape[1]
    reps = n // LANES
    if reps > 1:
        cos = jnp.concatenate([cos] * reps, axis=1)
        sin = jnp.concatenate([sin] * reps, axis=1)
    lane = lax.broadcasted_iota(jnp.int32, x.shape, 1)
    first_half = (lane % 64) < 32
    partner = jnp.where(first_half, pltpu.roll(x, n - 32, 1), pltpu.roll(x, 32, 1))
    return x * cos + partner * sin


def _rope_tables(pos):
    inv = ROPE_THETA ** (-jnp.arange(32, dtype=F32) * 2.0 / 64)
    ang = pos.astype(F32)[:, None] * inv[None, :]
    c, s = jnp.cos(ang), jnp.sin(ang)
    return jnp.concatenate([c, c, c, c], axis=1), jnp.concatenate([-s, s, -s, s], axis=1)


def _mla_prep_kernel(q_ref, kv_ref, cos_ref, sin_ref, wuk_ref, kvg_ref,
                     qh_ref, kh_ref, ckv_ref, kpe_ref, *v_ref, absorb):
    cos, sin = cos_ref[...], sin_ref[...]
    q = q_ref[...] * (MLA_SCALE * LOG2E)
    q_pe = _rope(q[:, 1024:], cos, sin).astype(BF16)
    q_nope = q[:, :1024].astype(BF16)
    kv = kv_ref[...]
    ckv = _rms(kv[:, :256], kvg_ref[...])
    kpe = _rope(kv[:, 256:], cos, sin)
    ckv_ref[...] = ckv
    kpe_ref[...] = kpe[:, :64]
    ckv_b, kpe_b = ckv.astype(BF16), kpe.astype(BF16)
    if absorb:
        kh_ref[:, 0:256] = ckv_b
        kh_ref[:, 256:384] = kpe_b
    else:
        v_ref[0][...] = ckv_b
    for h in range(MLA_HEADS):
        head = slice(h * 128, (h + 1) * 128)
        if absorb:
            qh_ref[h, :, 0:256] = _dot(q_nope[:, head], wuk_ref[h]).astype(BF16)
            qh_ref[h, :, 256:384] = q_pe[:, head]
        else:
            qh_ref[h, :, 0:128] = q_nope[:, head]
            qh_ref[h, :, 128:256] = q_pe[:, head]
            kh_ref[h, :, 0:128] = _dot(ckv_b, wuk_ref[h]).astype(BF16)
            kh_ref[h, :, 128:256] = kpe_b


def _mla_prep(q, qkv_a, cos, sin, wuk, kv_gain, absorb):
    m = q.shape[0]
    tm = min(m, 512)
    nt = cos.shape[0] // tm
    row = lambda i: (i, 0)
    per_head = lambda i: (0, i, 0)
    if absorb:
        head_shapes = [jax.ShapeDtypeStruct((MLA_HEADS, m, MLA_QK), BF16), jax.ShapeDtypeStruct((m, MLA_QK), BF16)]
        head_specs = [pl.BlockSpec((MLA_HEADS, tm, MLA_QK), per_head), pl.BlockSpec((tm, MLA_QK), row)]
        extra_shapes, extra_specs = [], []
    else:
        head_shapes = [jax.ShapeDtypeStruct((MLA_HEADS, m, 256), BF16)] * 2
        head_specs = [pl.BlockSpec((MLA_HEADS, tm, 256), per_head)] * 2
        extra_shapes = [jax.ShapeDtypeStruct((m, MLA_KV_LORA), BF16)]
        extra_specs = [pl.BlockSpec((tm, MLA_KV_LORA), row)]
    return pl.pallas_call(
        functools.partial(_mla_prep_kernel, absorb=absorb),
        out_shape=tuple(head_shapes + [jax.ShapeDtypeStruct((m, MLA_KV_LORA), F32),
                                       jax.ShapeDtypeStruct((m, MLA_ROPE), F32)] + extra_shapes),
        grid=(m // tm,),
        in_specs=[
            pl.BlockSpec((tm, 2048), row),
            pl.BlockSpec((tm, MLA_QK), lambda i: (i, 1)),
            pl.BlockSpec((tm, LANES), lambda i: (i % nt, 0)),
            pl.BlockSpec((tm, LANES), lambda i: (i % nt, 0)),
            pl.BlockSpec((MLA_HEADS,) + wuk.shape[1:], lambda i: (0, 0, 0)),
            pl.BlockSpec((1, 256), lambda i: (0, 0)),
        ],
        out_specs=tuple(head_specs + [pl.BlockSpec((tm, MLA_KV_LORA), row), pl.BlockSpec((tm, MLA_ROPE), row)]
                        + extra_specs),
        compiler_params=_params(("parallel",)),
        name="mla_prep",
    )(q, qkv_a, cos, sin, wuk, kv_gain.reshape(1, 256))


def _tree(op, xs):
    xs = list(xs)
    while len(xs) > 1:
        xs = [op(xs[i], xs[i + 1]) if i + 1 < len(xs) else xs[i] for i in range(0, len(xs), 2)]
    return xs[0]


def _online_softmax(s, row0, m_scr, l_scr, acc_scr, masked):
    rows_n, tk = s.shape
    ch = SOFTMAX_ROWS
    acc_tiles = acc_scr.shape[1] // LANES
    out = []
    for c in range(rows_n // ch):
        rows = pl.ds(row0 + c * ch, ch)
        sc = s[c * ch:(c + 1) * ch, :]
        if masked:
            qpos = c * ch + lax.broadcasted_iota(jnp.int32, (ch, tk), 0)
            kpos = lax.broadcasted_iota(jnp.int32, (ch, tk), 1)
            sc = jnp.where(kpos <= qpos, sc, NEG_INF)
        tiles = [sc[:, t * LANES:(t + 1) * LANES] for t in range(tk // LANES)]
        m_prev = m_scr[rows, :]
        m_new = jnp.maximum(m_prev, jnp.max(_tree(jnp.maximum, tiles), axis=1, keepdims=True))
        alpha = jnp.exp2(m_prev - m_new)
        ps = [jnp.exp2(t - m_new) for t in tiles]
        if l_scr is not None:
            l_scr[rows, :] = alpha * l_scr[rows, :] + jnp.sum(_tree(jnp.add, ps), axis=1, keepdims=True)
        m_scr[rows, :] = m_new
        acc_scr[rows, :] = acc_scr[rows, :] * jnp.concatenate([alpha] * acc_tiles, axis=1)
        out.append(jnp.concatenate(ps, axis=1).astype(BF16))
    return jnp.concatenate(out, axis=0)


def _flash_mla_kernel(q_ref, k_ref, v_ref, o_ref, m_scr, l_scr, acc_scr, *, tile):
    qi, ki = pl.program_id(1), pl.program_id(2)

    @pl.when(ki == 0)
    def _():
        m_scr[...] = jnp.full_like(m_scr, NEG_INF)
        l_scr[...] = jnp.zeros_like(l_scr)
        acc_scr[...] = jnp.zeros_like(acc_scr)

    def all_heads(masked):
        def head(h, carry):
            row0 = pl.multiple_of(h * tile, tile)
            p = _online_softmax(_dot_nt(q_ref[h], k_ref[h]), row0, m_scr, l_scr, acc_scr, masked)
            acc_scr[pl.ds(row0, tile), :] += _dot(p, v_ref[...])
            return carry
        lax.fori_loop(0, MLA_HEADS, head, 0, unroll=4)

    @pl.when(ki < qi)
    def _():
        all_heads(False)

    @pl.when(ki == qi)
    def _():
        all_heads(True)
        for h in range(MLA_HEADS):
            rows = slice(h * tile, (h + 1) * tile)
            l = l_scr[rows, :]
            o = acc_scr[rows, :] / jnp.concatenate([l, l], axis=1)
            o_ref[:, h * 256:(h + 1) * 256] = o.astype(o_ref.dtype)


def _flash_mla(q_heads, k_heads, v, batch, seq, tile=512):
    nt = seq // tile
    return pl.pallas_call(
        functools.partial(_flash_mla_kernel, tile=tile),
        out_shape=jax.ShapeDtypeStruct((batch * seq, MLA_HEADS * MLA_KV_LORA), BF16),
        grid=(batch, nt, nt),
        in_specs=[
            pl.BlockSpec((MLA_HEADS, tile, 256), lambda b, qi, ki: (0, b * nt + qi, 0)),
            pl.BlockSpec((MLA_HEADS, tile, 256), lambda b, qi, ki: (0, b * nt + jnp.minimum(ki, qi), 0)),
            pl.BlockSpec((tile, MLA_KV_LORA), lambda b, qi, ki: (b * nt + jnp.minimum(ki, qi), 0)),
        ],
        out_specs=pl.BlockSpec((tile, MLA_HEADS * MLA_KV_LORA), lambda b, qi, ki: (b * nt + qi, 0)),
        scratch_shapes=[
            pltpu.VMEM((MLA_HEADS * tile, LANES), F32),
            pltpu.VMEM((MLA_HEADS * tile, LANES), F32),
            pltpu.VMEM((MLA_HEADS * tile, MLA_KV_LORA), F32),
        ],
        compiler_params=_params(("parallel", "parallel", "arbitrary")),
        name="flash_mla",
    )(q_heads, k_heads, v)


def _mla_out_kernel(o_ref, wuv_ref, wo_ref, x_ref, y_ref):
    o = o_ref[...].astype(BF16)
    heads = [_dot(o[:, h * 256:(h + 1) * 256], wuv_ref[h]).astype(BF16) for h in range(MLA_HEADS)]
    y_ref[...] = x_ref[...] + _dot(jnp.concatenate(heads, axis=1), wo_ref[...])


def _mla_out(o_lat, wuv, wo, x):
    m, d = x.shape
    tm = min(m, 512)
    return pl.pallas_call(
        _mla_out_kernel,
        out_shape=jax.ShapeDtypeStruct((m, d), F32),
        grid=(m // tm,),
        in_specs=[
            pl.BlockSpec((tm, 2048), lambda i: (i, 0)),
            pl.BlockSpec((MLA_HEADS, 256, 128), lambda i: (0, 0, 0)),
            pl.BlockSpec((1024, d), lambda i: (0, 0)),
            pl.BlockSpec((tm, d), lambda i: (i, 0)),
        ],
        out_specs=pl.BlockSpec((tm, d), lambda i: (i, 0)),
        compiler_params=_params(("parallel",)),
        name="mla_out",
    )(o_lat, wuv, wo, x)


def _decode_mla_kernel(pt_ref, q_ref, kn_ref, ckv_hbm, kpe_hbm, o_ref,
                       cbuf, pbuf, kb_scr, sem, *, layer, n_pages):
    b = pl.program_id(0)
    slot = b % 2

    def copies(tok, sl, p):
        page = pt_ref[tok * n_pages + p]
        return (pltpu.make_async_copy(ckv_hbm.at[layer, page], cbuf.at[sl, p], sem.at[0, sl]),
                pltpu.make_async_copy(kpe_hbm.at[layer, page], pbuf.at[sl, p], sem.at[1, sl]))

    def issue(tok, sl):
        def body(p, carry):
            for cp in copies(tok, sl, p):
                cp.start()
            return carry
        lax.fori_loop(0, n_pages, body, 0)

    @pl.when(b == 0)
    def _():
        issue(0, 0)

    @pl.when(b + 1 < pl.num_programs(0))
    def _():
        issue(b + 1, 1 - slot)

    def wait_body(p, carry):
        for cp in copies(b, slot, p):
            cp.wait()
        return carry
    lax.fori_loop(0, n_pages, wait_body, 0)

    n_keys = n_pages * PAGE_SIZE
    kb_scr[...] = cbuf[slot].reshape(n_keys, MLA_KV_LORA).astype(BF16)

    q = q_ref[0]
    kn = kn_ref[0].astype(F32)
    s_self = jnp.sum(q.astype(F32) * kn, axis=-1, keepdims=True)
    q_pe = q[:, 256:320]
    kpe_t = jnp.concatenate([pbuf[slot, p].astype(BF16) for p in range(n_pages)], axis=1)
    s = _dot_nt(q[:, :256], kb_scr[...]) + _dot(q_pe, kpe_t)
    m = jnp.maximum(jnp.max(s, axis=-1, keepdims=True), s_self)
    p = jnp.exp2(s - m)
    p_self = jnp.exp2(s_self - m)
    l = jnp.sum(p, axis=-1, keepdims=True) + p_self
    acc = _dot(p.astype(BF16), kb_scr[...]) + p_self * kn[:, :256]
    o_ref[0] = acc / l


def _decode_mla(page_table, q_tok, k_new, cache_ckv, cache_kpe, layer):
    nb, n_pages = page_table.shape
    n_keys = n_pages * PAGE_SIZE
    return pl.pallas_call(
        functools.partial(_decode_mla_kernel, layer=layer, n_pages=n_pages),
        out_shape=jax.ShapeDtypeStruct((nb, MLA_HEADS, MLA_KV_LORA), F32),
        grid_spec=pltpu.PrefetchScalarGridSpec(
            num_scalar_prefetch=1,
            grid=(nb,),
            in_specs=[
                pl.BlockSpec((1, MLA_HEADS, MLA_QK), lambda b, pt: (b, 0, 0)),
                pl.BlockSpec((1, 1, MLA_QK), lambda b, pt: (b, 0, 0)),
                pl.BlockSpec(memory_space=pl.ANY),
                pl.BlockSpec(memory_space=pl.ANY),
            ],
            out_specs=pl.BlockSpec((1, MLA_HEADS, MLA_KV_LORA), lambda b, pt: (b, 0, 0)),
            scratch_shapes=[
                pltpu.VMEM((2, n_pages, PAGE_SIZE, MLA_KV_LORA), F32),
                pltpu.VMEM((2, n_pages, MLA_ROPE, PAGE_SIZE), F32),
                pltpu.VMEM((n_keys, MLA_KV_LORA), BF16),
                pltpu.SemaphoreType.DMA((2, 2)),
            ],
        ),
        compiler_params=_params(("arbitrary",)),
        name="decode_mla",
    )(page_table.reshape(-1), q_tok, k_new.reshape(nb, 1, MLA_QK), cache_ckv, cache_kpe)


def _diff_prep_kernel(qkv_ref, cos_ref, sin_ref, q_ref, k32_ref, kb_ref, v32_ref, vb_ref):
    qkv = qkv_ref[...]
    qk = _rope(qkv[:, :1536], cos_ref[...], sin_ref[...])
    q_ref[...] = (qk[:, :1024] * (DIFF_SCALE * LOG2E)).astype(BF16)
    k = qk[:, 1024:]
    v = qkv[:, 1536:]
    k32_ref[...] = k
    kb_ref[...] = k.astype(BF16)
    v32_ref[...] = v
    vb_ref[...] = v.astype(BF16)


def _diff_prep(qkv, cos, sin):
    m = qkv.shape[0]
    tm = min(m, 512)
    nt = cos.shape[0] // tm
    row = lambda i: (i, 0)
    return pl.pallas_call(
        _diff_prep_kernel,
        out_shape=(
            jax.ShapeDtypeStruct((m, 1024), BF16),
            jax.ShapeDtypeStruct((m, 512), F32),
            jax.ShapeDtypeStruct((m, 512), BF16),
            jax.ShapeDtypeStruct((m, 512), F32),
            jax.ShapeDtypeStruct((m, 512), BF16),
        ),
        grid=(m // tm,),
        in_specs=[
            pl.BlockSpec((tm, 2048), row),
            pl.BlockSpec((tm, LANES), lambda i: (i % nt, 0)),
            pl.BlockSpec((tm, LANES), lambda i: (i % nt, 0)),
        ],
        out_specs=(
            pl.BlockSpec((tm, 1024), row),
            pl.BlockSpec((tm, 512), row),
            pl.BlockSpec((tm, 512), row),
            pl.BlockSpec((tm, 512), row),
            pl.BlockSpec((tm, 512), row),
        ),
        compiler_params=_params(("parallel",)),
        name="diff_prep",
    )(qkv, cos, sin)


def _diff_lambda(lq1_ref, lk1_ref, lq2_ref, lk2_ref, lam_init):
    e1 = jnp.exp(jnp.sum(lq1_ref[...] * lk1_ref[...], axis=-1, keepdims=True))
    e2 = jnp.exp(jnp.sum(lq2_ref[...] * lk2_ref[...], axis=-1, keepdims=True))
    return e1 - e2 + lam_init


def _flash_diff_kernel(lq1_ref, lk1_ref, lq2_ref, lk2_ref, sub_ref, q_ref, k_ref, v_ref, o_ref,
                       m_scr, acc_scr, *, tile, lam_init):
    qi, ki = pl.program_id(2), pl.program_id(3)

    @pl.when(ki == 0)
    def _():
        m_scr[...] = jnp.full_like(m_scr, NEG_INF)
        acc_scr[...] = jnp.zeros_like(acc_scr)

    def all_rows(masked):
        q = q_ref[...]
        k = k_ref[...]
        v_ext = jnp.concatenate([v_ref[...], jnp.ones((tile, LANES), BF16)], axis=1)
        comp0 = lax.broadcasted_iota(jnp.int32, (tile, LANES), 1) < DIFF_HEAD_DIM
        zero = jnp.zeros((tile, LANES), BF16)
        for r in range(2):
            head = q[:, r * LANES:(r + 1) * LANES]
            for c in range(2):
                qc = jnp.where(comp0, head, zero) if c == 0 else jnp.where(comp0, zero, head)
                row0 = (2 * r + c) * tile
                p = _online_softmax(_dot_nt(qc, k), row0, m_scr, None, acc_scr, masked)
                acc_scr[row0:row0 + tile, :] += _dot(p, v_ext)

    @pl.when(ki < qi)
    def _():
        all_rows(False)

    @pl.when(ki == qi)
    def _():
        all_rows(True)
        lam = _diff_lambda(lq1_ref, lk1_ref, lq2_ref, lk2_ref, lam_init)
        g = sub_ref[...]
        for r in range(2):
            o = []
            for c in range(2):
                rows = slice((2 * r + c) * tile, (2 * r + c + 1) * tile)
                o.append(acc_scr[rows, :LANES] / acc_scr[rows, LANES:])
            d = _rms(o[0] - lam * o[1], g) * (1.0 - lam_init)
            o_ref[:, r * LANES:(r + 1) * LANES] = d.astype(o_ref.dtype)


def _flash_diff(q, k, v, lams, subln, lam_init, batch, seq, tile=512):
    nt = seq // tile

    def kv_map(b, g, qi, ki):
        return (b * nt + jnp.minimum(ki, qi), g)

    vec = pl.BlockSpec((1, DIFF_HEAD_DIM), lambda b, g, qi, ki: (0, 0))
    return pl.pallas_call(
        functools.partial(_flash_diff_kernel, tile=tile, lam_init=lam_init),
        out_shape=jax.ShapeDtypeStruct((batch * seq, 1024), BF16),
        grid=(batch, DIFF_GROUPS, nt, nt),
        in_specs=[
            vec, vec, vec, vec,
            pl.BlockSpec((1, LANES), lambda b, g, qi, ki: (0, 0)),
            pl.BlockSpec((tile, 256), lambda b, g, qi, ki: (b * nt + qi, g)),
            pl.BlockSpec((tile, LANES), kv_map),
            pl.BlockSpec((tile, LANES), kv_map),
        ],
        out_specs=pl.BlockSpec((tile, 256), lambda b, g, qi, ki: (b * nt + qi, g)),
        scratch_shapes=[
            pltpu.VMEM((4 * tile, LANES), F32),
            pltpu.VMEM((4 * tile, 2 * LANES), F32),
        ],
        compiler_params=_params(("parallel", "parallel", "parallel", "arbitrary")),
        name="flash_diff",
    )(*[a.reshape(1, DIFF_HEAD_DIM) for a in lams], subln.reshape(1, LANES), q, k, v)


def _decode_diff_kernel(pt_ref, lq1_ref, lk1_ref, lq2_ref, lk2_ref, sub_ref, q_ref, kn_ref, vn_ref,
                        k_hbm, v_hbm, o_ref, kbuf, vbuf, sem, m_scr, l_scr, acc_scr,
                        *, layer, n_pages, unit, lam_init):
    b, u = pl.program_id(0), pl.program_id(1)
    n_units = n_pages // unit
    step = b * n_units + u
    slot = step % 2

    def copies(st, sl, p):
        tok, un = st // n_units, st % n_units
        page = pt_ref[tok * n_pages + un * unit + p]
        return (pltpu.make_async_copy(k_hbm.at[layer, page], kbuf.at[sl, p], sem.at[0, sl]),
                pltpu.make_async_copy(v_hbm.at[layer, page], vbuf.at[sl, p], sem.at[1, sl]))

    def issue(st, sl):
        def body(p, carry):
            for cp in copies(st, sl, p):
                cp.start()
            return carry
        lax.fori_loop(0, unit, body, 0)

    @pl.when(step == 0)
    def _():
        issue(0, 0)

    @pl.when(step + 1 < pl.num_programs(0) * n_units)
    def _():
        issue(step + 1, 1 - slot)

    def wait_body(p, carry):
        for cp in copies(step, slot, p):
            cp.wait()
        return carry
    lax.fori_loop(0, unit, wait_body, 0)

    q = q_ref[0]

    @pl.when(u == 0)
    def _():
        m_scr[...] = jnp.sum(q.astype(F32) * kn_ref[0].astype(F32), axis=-1, keepdims=True)
        l_scr[...] = jnp.ones_like(l_scr)
        acc_scr[...] = vn_ref[0]

    kt = jnp.concatenate([kbuf[slot, pg].astype(BF16) for pg in range(unit)], axis=1)
    s = _dot(q, kt)
    m_old = m_scr[...]
    m_new = jnp.maximum(m_old, jnp.max(s, axis=-1, keepdims=True))
    alpha = jnp.exp2(m_old - m_new)
    p = jnp.exp2(s - m_new)
    l_scr[...] = alpha * l_scr[...] + jnp.sum(p, axis=-1, keepdims=True)
    m_scr[...] = m_new
    pb = p.astype(BF16)
    halves = []
    for gp in range(DIFF_GROUPS // 2):
        out = jnp.zeros((16, 2 * LANES), F32)
        for pp in range(unit // 2):
            stacked = []
            for pg in (2 * pp, 2 * pp + 1):
                side = [vbuf[slot, pg, pl.ds(2 * gp + i, PAGE_SIZE, stride=DIFF_GROUPS), :] for i in range(2)]
                stacked.append(jnp.concatenate(side, axis=1).astype(BF16))
            out = out + _dot(pb[:, 2 * pp * PAGE_SIZE:(2 * pp + 2) * PAGE_SIZE], jnp.concatenate(stacked, axis=0))
        halves += [out[:, :LANES], out[:, LANES:]]
    row_g = lax.broadcasted_iota(jnp.int32, (16, LANES), 0) % DIFF_GROUPS
    pv = jnp.where(row_g == 0, halves[0], jnp.where(row_g == 1, halves[1],
                                                    jnp.where(row_g == 2, halves[2], halves[3])))
    acc_scr[...] = alpha * acc_scr[...] + pv

    @pl.when(u == n_units - 1)
    def _():
        lam = _diff_lambda(lq1_ref, lk1_ref, lq2_ref, lk2_ref, lam_init)
        on = acc_scr[...] / l_scr[...]
        d = on[0:8] - lam * on[8:16]
        o_ref[0] = _rms(d, sub_ref[...]) * (1.0 - lam_init)


def _decode_diff(page_table, q_rows, k_new, v_rows, cache_kt, cache_v, lams, subln, lam_init, layer, unit=16):
    nb, n_pages = page_table.shape
    vec = pl.BlockSpec((1, DIFF_HEAD_DIM), lambda b, u, pt: (0, 0))
    return pl.pallas_call(
        functools.partial(_decode_diff_kernel, layer=layer, n_pages=n_pages, unit=unit, lam_init=lam_init),
        out_shape=jax.ShapeDtypeStruct((nb, 8, LANES), F32),
        grid_spec=pltpu.PrefetchScalarGridSpec(
            num_scalar_prefetch=1,
            grid=(nb, n_pages // unit),
            in_specs=[
                vec, vec, vec, vec,
                pl.BlockSpec((1, LANES), lambda b, u, pt: (0, 0)),
                pl.BlockSpec((1, 16, 512), lambda b, u, pt: (b, 0, 0)),
                pl.BlockSpec((1, 1, 512), lambda b, u, pt: (b, 0, 0)),
                pl.BlockSpec((1, 16, LANES), lambda b, u, pt: (b, 0, 0)),
                pl.BlockSpec(memory_space=pl.ANY),
                pl.BlockSpec(memory_space=pl.ANY),
            ],
            out_specs=pl.BlockSpec((1, 8, LANES), lambda b, u, pt: (b, 0, 0)),
            scratch_shapes=[
                pltpu.VMEM((2, unit, 512, PAGE_SIZE), F32),
                pltpu.VMEM((2, unit, 512, LANES), F32),
                pltpu.SemaphoreType.DMA((2, 2)),
                pltpu.VMEM((16, 1), F32),
                pltpu.VMEM((16, 1), F32),
                pltpu.VMEM((16, LANES), F32),
            ],
        ),
        compiler_params=_params(("arbitrary", "arbitrary")),
        name="decode_diff",
    )(page_table.reshape(-1), *[a.reshape(1, DIFF_HEAD_DIM) for a in lams], subln.reshape(1, LANES),
      q_rows, k_new.reshape(nb, 1, 512), v_rows, cache_kt, cache_v)


def _ssd_gate_norm(y, xs, z, d_ref, nw_ref):
    yg = (y + d_ref[...] * xs) * _silu(z)
    w = SSD_D_INNER // SSD_GROUPS
    parts = []
    for g in range(SSD_GROUPS):
        part = yg[:, g * w:(g + 1) * w]
        parts.append(part * lax.rsqrt(jnp.mean(part * part, axis=-1, keepdims=True) + RMS_EPS))
    return jnp.concatenate(parts, axis=1) * nw_ref[...]


def _ssd_prompt_kernel(z_ref, xbc_ref, dt_ref, cw_ref, cb_ref, dtb_ref, a_ref, e_ref, d_ref, nw_ref,
                       yn_ref, st_ref, buf_ref, xp_scr, s_scr, *, chunk):
    c = pl.program_id(1)

    @pl.when(c == 0)
    def _():
        xp_scr[0:8, :] = jnp.zeros((8, SSD_CONV_DIM), F32)
        s_scr[...] = jnp.zeros_like(s_scr)

    xp_scr[8:8 + chunk, :] = xbc_ref[...]
    conv = cb_ref[...] + sum(xp_scr[5 + k:5 + k + chunk, :] * cw_ref[k:k + 1, :] for k in range(SSD_CONV))
    u = _silu(conv)
    xs = u[:, :SSD_D_INNER]
    bm = u[:, SSD_D_INNER:SSD_D_INNER + 512]
    cm = u[:, SSD_D_INNER + 512:]

    dt = jax.nn.softplus(dt_ref[...] + dtb_ref[...])
    a = dt * a_ref[...]
    row = lax.broadcasted_iota(jnp.int32, (chunk, chunk), 0)
    col = lax.broadcasted_iota(jnp.int32, (chunk, chunk), 1)
    causal = col <= row
    a_cs = _dot_exact(causal.astype(F32), a)
    a_cs_t = a_cs.T
    a_last = a_cs[chunk - 1:chunk, :]
    ex = _dot_exact(jnp.concatenate([dt, jnp.exp(a_cs), jnp.exp(a_last - a_cs)], axis=0), e_ref[...])
    dtx, ecs, dend = ex[:chunk], ex[chunk:2 * chunk], ex[2 * chunk:]
    xdt = xs * dtx
    xdt_b = xdt.astype(BF16)
    xdtd_b = (xdt * dend).astype(BF16)
    etot = ecs[chunk - 1:chunk, :]
    low_head = lax.broadcasted_iota(jnp.int32, (chunk, LANES), 1) < SSD_HEAD_DIM

    ys = []
    for g in range(SSD_GROUPS):
        bg = bm[:, g * 128:(g + 1) * 128]
        cg = cm[:, g * 128:(g + 1) * 128].astype(BF16)
        cb = _dot_nt(cg, bg.astype(BF16))
        bg_t = bg.T.astype(BF16)
        for pi in range(4):
            pair = g * 4 + pi
            sl = slice(pair * LANES, (pair + 1) * LANES)
            ms = []
            for h in (2 * pair, 2 * pair + 1):
                seg = a_cs[:, h:h + 1] - a_cs_t[h:h + 1, :]
                ms.append((cb * jnp.exp(jnp.where(causal, seg, NEG_INF))).astype(BF16))
            y_diag = jnp.where(low_head, _dot(ms[0], xdt_b[:, sl]), _dot(ms[1], xdt_b[:, sl]))
            st = s_scr[:, sl]
            y_off = _dot(cg, st.astype(BF16)) * ecs[:, sl]
            s_scr[:, sl] = etot[:, sl] * st + _dot(bg_t, xdtd_b[:, sl])
            ys.append(y_diag + y_off)

    y = jnp.concatenate(ys, axis=1)
    yn_ref[...] = _ssd_gate_norm(y, xs, z_ref[...], d_ref, nw_ref).astype(yn_ref.dtype)
    xp_scr[0:8, :] = xp_scr[chunk:chunk + 8, :]

    @pl.when(c == pl.num_programs(1) - 1)
    def _():
        st_ref[0] = s_scr[...].T
        buf_ref[0] = xp_scr[chunk + 5:chunk + 8, :]


def _ssd_prompt(z, xbc, dt, cw, cb, dtb, a_neg, expand, d_exp, nw, batch, seq, chunk=SSD_CHUNK):
    nc = seq // chunk
    row = lambda b, c: (b * nc + c, 0)
    const = lambda b, c: (0, 0)
    return pl.pallas_call(
        functools.partial(_ssd_prompt_kernel, chunk=chunk),
        out_shape=(
            jax.ShapeDtypeStruct((batch * seq, SSD_D_INNER), BF16),
            jax.ShapeDtypeStruct((batch, SSD_D_INNER, SSD_STATE), F32),
            jax.ShapeDtypeStruct((batch, SSD_CONV - 1, SSD_CONV_DIM), F32),
        ),
        grid=(batch, nc),
        in_specs=[
            pl.BlockSpec((chunk, SSD_D_INNER), row),
            pl.BlockSpec((chunk, SSD_CONV_DIM), row),
            pl.BlockSpec((chunk, LANES), row),
            pl.BlockSpec((SSD_CONV, SSD_CONV_DIM), const),
            pl.BlockSpec((1, SSD_CONV_DIM), const),
            pl.BlockSpec((1, LANES), const),
            pl.BlockSpec((1, LANES), const),
            pl.BlockSpec((LANES, SSD_D_INNER), const),
            pl.BlockSpec((1, SSD_D_INNER), const),
            pl.BlockSpec((1, SSD_D_INNER), const),
        ],
        out_specs=(
            pl.BlockSpec((chunk, SSD_D_INNER), row),
            pl.BlockSpec((1, SSD_D_INNER, SSD_STATE), lambda b, c: (b, 0, 0)),
            pl.BlockSpec((1, SSD_CONV - 1, SSD_CONV_DIM), lambda b, c: (b, 0, 0)),
        ),
        scratch_shapes=[
            pltpu.VMEM((chunk + 8, SSD_CONV_DIM), F32),
            pltpu.VMEM((SSD_STATE, SSD_D_INNER), F32),
        ],
        compiler_params=_params(("parallel", "arbitrary")),
        name="ssd_prompt",
    )(z, xbc, dt, cw, cb, dtb, a_neg, expand, d_exp, nw)


def _ssd_decode_pre_kernel(xbc_ref, dt_ref, buf_ref, cw_ref, cb_ref, dtb_ref, a_ref, e_ref,
                           xs_ref, xdt_ref, dec_ref, b_ref, c_ref, nbuf_ref):
    xbc = xbc_ref[...]
    conv = cb_ref[...] + xbc * cw_ref[3:4, :]
    for k in range(SSD_CONV - 1):
        conv = conv + buf_ref[k] * cw_ref[k:k + 1, :]
    u = _silu(conv)
    xs = u[:, :SSD_D_INNER]
    dt = jax.nn.softplus(dt_ref[...] + dtb_ref[...])
    ex = _dot_exact(jnp.concatenate([dt, dt * a_ref[...]], axis=0), e_ref[...])
    t = xbc.shape[0]
    xs_ref[...] = xs
    xdt_ref[...] = xs * ex[:t]
    dec_ref[...] = jnp.exp(ex[t:])
    b_ref[...] = u[:, SSD_D_INNER:SSD_D_INNER + 512]
    c_ref[...] = u[:, SSD_D_INNER + 512:]
    nbuf_ref[0] = buf_ref[1]
    nbuf_ref[1] = buf_ref[2]
    nbuf_ref[2] = xbc


def _ssd_decode_pre(xbc, dt, buf, cw, cb, dtb, a_neg, expand):
    t = xbc.shape[0]
    wide = jax.ShapeDtypeStruct((t, SSD_D_INNER), F32)
    grp = jax.ShapeDtypeStruct((t, 512), F32)
    return pl.pallas_call(
        _ssd_decode_pre_kernel,
        out_shape=(wide, wide, wide, grp, grp, jax.ShapeDtypeStruct((SSD_CONV - 1, t, SSD_CONV_DIM), F32)),
        compiler_params=pltpu.CompilerParams(vmem_limit_bytes=VMEM_LIMIT),
        name="ssd_decode_pre",
    )(xbc, dt, buf, cw, cb, dtb, a_neg, expand)


def _split_bf16x3(x):
    hi = x.astype(BF16).astype(F32)
    mid = (x - hi).astype(BF16).astype(F32)
    lo = (x - hi - mid).astype(BF16).astype(F32)
    return hi, mid, lo


def _ssd_decode_state_kernel(xdt_ref, dec_ref, b_ref, c_ref, st_ref, y_ref, nst_ref):
    xdt = xdt_ref[0]
    lane_g = lax.broadcasted_iota(jnp.int32, (8, SSD_D_INNER), 1) // 512
    row = lax.broadcasted_iota(jnp.int32, (8, SSD_D_INNER), 0)
    lhs = jnp.where(row == lane_g, xdt, 0.0) + jnp.where(row == 4, dec_ref[0], 0.0)
    brow = lax.broadcasted_iota(jnp.int32, (8, LANES), 0)
    bmat = jnp.zeros((8, LANES), F32)
    cmat = jnp.zeros((8, LANES), F32)
    for g in range(SSD_GROUPS):
        bmat = jnp.where(brow == g, b_ref[0][:, g * 128:(g + 1) * 128], bmat)
        cmat = jnp.where(brow == g, c_ref[0][:, g * 128:(g + 1) * 128], cmat)
    rhs = jnp.concatenate([bmat, (brow == 4).astype(F32)], axis=1)
    lh, lm, ll = _split_bf16x3(lhs)
    rh, rm, rl = _split_bf16x3(rhs)
    lhs6 = jnp.concatenate([lh, lh, lm, lm, lh, ll], axis=0).astype(BF16)
    rhs6 = jnp.concatenate([rh, rm, rh, rm, rl, rh], axis=0).astype(BF16)
    both = lax.dot_general(lhs6, rhs6, TN_DIMS, preferred_element_type=F32)
    new = st_ref[0] * both[:, LANES:] + both[:, :LANES]
    nst_ref[0] = new
    yg = _dot_nt(cmat.astype(BF16), new.astype(BF16))
    y_ref[0] = jnp.sum(jnp.where(row == lane_g, yg, 0.0), axis=0, keepdims=True)


def _ssd_decode_state(xdt, dec, bm, cm, state):
    t = xdt.shape[0]
    tok3 = lambda w: pl.BlockSpec((1, 1, w), lambda i: (i, 0, 0))
    st_spec = pl.BlockSpec((1, SSD_D_INNER, SSD_STATE), lambda i: (i, 0, 0))
    return pl.pallas_call(
        _ssd_decode_state_kernel,
        out_shape=(jax.ShapeDtypeStruct((t, 1, SSD_D_INNER), F32),
                   jax.ShapeDtypeStruct((t, SSD_D_INNER, SSD_STATE), F32)),
        grid=(t,),
        in_specs=[tok3(SSD_D_INNER), tok3(SSD_D_INNER), tok3(512), tok3(512), st_spec],
        out_specs=(tok3(SSD_D_INNER), st_spec),
        compiler_params=_params(("parallel",)),
        name="ssd_decode_state",
    )(xdt.reshape(t, 1, -1), dec.reshape(t, 1, -1), bm.reshape(t, 1, -1), cm.reshape(t, 1, -1), state)


def _ssd_decode_post_kernel(y_ref, xs_ref, z_ref, d_ref, nw_ref, o_ref):
    o_ref[...] = _ssd_gate_norm(y_ref[...], xs_ref[...], z_ref[...], d_ref, nw_ref).astype(o_ref.dtype)


def _ssd_decode_post(y, xs, z, d_exp, nw):
    return pl.pallas_call(
        _ssd_decode_post_kernel,
        out_shape=jax.ShapeDtypeStruct(y.shape, BF16),
        compiler_params=pltpu.CompilerParams(vmem_limit_bytes=VMEM_LIMIT),
        name="ssd_decode_post",
    )(y, xs, z, d_exp, nw)


def _xa_prompt_kernel(x_ref, g_ref, wq_ref, k_ref, v_ref, wo_ref, o_ref):
    x = x_ref[...]
    q = _dot(_rms(x, g_ref[...]).astype(BF16), wq_ref[...]).astype(BF16)
    k = k_ref[0].astype(BF16)
    v = v_ref[0].astype(BF16)
    heads = []
    for h in range(XA_HEADS):
        sl = slice(h * XA_HEAD_DIM, (h + 1) * XA_HEAD_DIM)
        s = _dot_nt(q[:, sl], k[:, sl]) * XA_SCALE
        p = jnp.exp(s - jnp.max(s, axis=-1, keepdims=True))
        o = _dot(p.astype(BF16), v[:, sl]) / jnp.sum(p, axis=-1, keepdims=True)
        heads.append(o.astype(BF16))
    o_ref[...] = x + _dot(jnp.concatenate(heads, axis=1), wo_ref[...])


def _xa_prompt(x, gain, wq, kv_mem, wo, batch, seq):
    m, d = x.shape
    tm = 512
    nt = seq // tm
    mem = kv_mem.shape[1]
    return pl.pallas_call(
        _xa_prompt_kernel,
        out_shape=jax.ShapeDtypeStruct((m, d), F32),
        grid=(m // tm,),
        in_specs=[
            pl.BlockSpec((tm, d), lambda i: (i, 0)),
            pl.BlockSpec((1, d), lambda i: (0, 0)),
            pl.BlockSpec((d, d), lambda i: (0, 0)),
            pl.BlockSpec((1, mem, d), lambda i: (i // nt, 0, 0)),
            pl.BlockSpec((1, mem, d), lambda i: (i // nt, 0, 1)),
            pl.BlockSpec((d, d), lambda i: (0, 0)),
        ],
        out_specs=pl.BlockSpec((tm, d), lambda i: (i, 0)),
        compiler_params=_params(("parallel",)),
        name="xa_prompt",
    )(x, gain.reshape(1, d), wq, kv_mem, kv_mem, wo)


def _xa_decode_kernel(q_ref, k_ref, v_ref, o_ref, *, tokens):
    n = k_ref.shape[2]
    rows = 8 * tokens
    lane = lax.broadcasted_iota(jnp.int32, (rows, n), 1)
    own = lane % 8 == lax.broadcasted_iota(jnp.int32, (rows, n), 0) % 8
    first_half = lane % 8 < XA_HEADS
    strides = [8 << i for i in range((n // 8).bit_length() - 1)]

    def over_tokens(op, x):
        for sh in strides:
            x = op(x, pltpu.roll(x, sh, 1))
        return x

    g = jnp.concatenate([_dot_nt(q_ref[0, t].astype(BF16), k_ref[0, t].astype(BF16)) for t in range(tokens)],
                        axis=0)
    part = jnp.sum(jnp.where(own, g, 0.0).reshape(tokens, 8, n), axis=1, keepdims=True)
    part = jnp.broadcast_to(part, (tokens, 8, n)).reshape(rows, n)
    s = (part + pltpu.roll(part, n - XA_HEADS, 1)) * XA_SCALE
    p = jnp.exp(s - over_tokens(jnp.maximum, s))
    p = p / over_tokens(jnp.add, p)
    p = jnp.where(own, jnp.where(first_half, p, pltpu.roll(p, XA_HEADS, 1)), 0.0)
    for t in range(tokens):
        o_ref[0, t] = _dot(p[8 * t:8 * t + 8].astype(BF16), v_ref[0, t].astype(BF16))


def _xa_decode(q, mem_k, mem_v, layer, tokens=4):
    t, d = q.shape
    rows = mem_k.shape[2]
    half = XA_HEAD_DIM // 2
    q8 = jnp.transpose(q.reshape(t, XA_HEADS, 2, half), (0, 2, 1, 3)).reshape(t // tokens, tokens, 8, half)
    kv_spec = pl.BlockSpec((1, tokens, rows, half), lambda i: (layer, i, 0, 0))
    tok_spec = pl.BlockSpec((1, tokens, 8, half), lambda i: (i, 0, 0, 0))
    o8 = pl.pallas_call(
        functools.partial(_xa_decode_kernel, tokens=tokens),
        out_shape=jax.ShapeDtypeStruct((t // tokens, tokens, 8, half), F32),
        grid=(t // tokens,),
        in_specs=[tok_spec, kv_spec, kv_spec],
        out_specs=tok_spec,
        compiler_params=_params(("parallel",)),
        name="xa_decode",
    )(q8, mem_k, mem_v)
    return jnp.transpose(o8.reshape(t, 2, XA_HEADS, half), (0, 2, 1, 3)).reshape(t, d)


def kernel(x_prompt, x_sample, cache_mla_ckv, cache_mla_kpe, cache_diff_k, cache_diff_v, state_ssm, state_conv, cache_mem_k, cache_mem_v, page_table, mem_prompt, norm_mix, norm_xa, norm_ffn, norm_final, xa_mem_norm, xa_wq, xa_wk, xa_wv, xa_wo, ffn_w_gate, ffn_w_up, ffn_w_down, mla_wq_a, mla_q_norm, mla_wq_b, mla_wkv_a, mla_kv_norm, mla_w_uk, mla_w_uv, mla_wo, diff_wq, diff_wk, diff_wv, diff_lambda_q1, diff_lambda_k1, diff_lambda_q2, diff_lambda_k2, diff_subln, diff_wo, ssd_w_in, ssd_conv_w, ssd_conv_b, ssd_dt_bias, ssd_A_log, ssd_D, ssd_norm, ssd_w_out):
    batch, seq, d = x_prompt.shape
    nb = x_sample.shape[0]
    depth = norm_mix.shape[0]
    n_mix = 3
    bf = lambda w: w.astype(BF16)

    n_a = mla_wq_a.shape[0]
    wkv_a = jnp.pad(mla_wkv_a, ((0, 0), (0, 0), (0, MLA_QK - mla_wkv_a.shape[2])))
    w_qkv_a = bf(jnp.concatenate([mla_wq_a, wkv_a], axis=2))
    wq_b = mla_wq_b.reshape(n_a, -1, MLA_HEADS, MLA_NOPE + MLA_ROPE)
    wq_b = bf(jnp.concatenate([
        wq_b[..., :MLA_NOPE].reshape(n_a, -1, MLA_HEADS * MLA_NOPE),
        jnp.pad(wq_b[..., MLA_NOPE:], ((0, 0), (0, 0), (0, 0), (0, 64))).reshape(n_a, -1, MLA_HEADS * 128),
    ], axis=2))
    w_uk = bf(jnp.transpose(mla_w_uk, (0, 2, 3, 1)))
    w_uk_t = bf(jnp.transpose(mla_w_uk, (0, 2, 1, 3)))
    w_uv = bf(jnp.transpose(mla_w_uv, (0, 2, 1, 3)))
    w_mla_o = bf(mla_wo)
    w_diff_qkv = bf(jnp.concatenate([diff_wq, diff_wk, diff_wv], axis=2))
    w_diff_o = bf(diff_wo)
    w_ssd_z = bf(ssd_w_in[:, :, :SSD_D_INNER])
    w_ssd_xbc = bf(ssd_w_in[:, :, SSD_D_INNER:SSD_D_INNER + SSD_CONV_DIM])
    w_ssd_dt = bf(jnp.pad(ssd_w_in[:, :, SSD_D_INNER + SSD_CONV_DIM:], ((0, 0), (0, 0), (0, LANES - SSD_HEADS))))
    w_ssd_out = bf(ssd_w_out)
    pad_heads = lambda a: jnp.pad(a, ((0, 0), (0, LANES - SSD_HEADS)))
    ssd_dtb = pad_heads(ssd_dt_bias)
    ssd_a = pad_heads(-jnp.exp(ssd_A_log.astype(F32)))
    ssd_d_exp = jnp.repeat(ssd_D, SSD_HEAD_DIM, axis=1)
    expand = (jnp.arange(LANES)[:, None] == jnp.arange(SSD_D_INNER)[None, :] // SSD_HEAD_DIM).astype(F32)
    w_xa_q, w_xa_o = bf(xa_wq), bf(xa_wo)
    w_xa_kv = bf(jnp.concatenate([xa_wk, xa_wv], axis=2))
    w_gate, w_up, w_down = bf(ffn_w_gate), bf(ffn_w_up), bf(ffn_w_down)

    def mem_view(c):
        mt = c.shape[2]
        c = c.reshape(depth, nb, mt, XA_HEADS, 2, XA_HEAD_DIM // 2)
        return jnp.transpose(c, (0, 1, 2, 4, 3, 5)).reshape(depth, nb, mt * 8, XA_HEAD_DIM // 2)

    mem_k, mem_v = mem_view(cache_mem_k), mem_view(cache_mem_v)
    n_pool = cache_diff_k.shape[1]
    pool_kpe_t = jnp.swapaxes(cache_mla_kpe, 2, 3)
    pool_kt = jnp.transpose(cache_diff_k, (0, 1, 3, 4, 5, 2)).reshape(-1, n_pool, 512, PAGE_SIZE)
    pool_v = cache_diff_v.reshape(-1, n_pool, PAGE_SIZE * DIFF_GROUPS, 2 * DIFF_HEAD_DIM)
    mem_flat = mem_prompt.reshape(-1, d)

    def run(x, pos_tab, prompt):
        m = x.shape[0]
        cos, sin = _rope_tables(pos_tab)
        new = {k: [] for k in ("mla_ckv", "mla_kpe", "diff_k", "diff_v", "ssm", "conv", "mem_k", "mem_v")}
        for i in range(depth):
            kind, j = i % n_mix, i // n_mix
            g_mix = norm_mix[i]
            if kind == 0:
                qkv_a = _linear(x, w_qkv_a[j], gain=g_mix, name="mla_qkv_a")
                q = _linear(qkv_a, wq_b[j], gain=mla_q_norm[j], name="mla_q_b")
                if prompt:
                    q_heads, k_heads, ckv, kpe, v_lat = _mla_prep(q, qkv_a, cos, sin, w_uk_t[j], mla_kv_norm[j], False)
                    o_lat = _flash_mla(q_heads, k_heads, v_lat, batch, seq)
                else:
                    qcat, kcat, ckv, kpe = _mla_prep(q, qkv_a, cos, sin, w_uk[j], mla_kv_norm[j], True)
                    o_lat = _decode_mla(page_table, jnp.transpose(qcat, (1, 0, 2)), kcat,
                                        cache_mla_ckv, pool_kpe_t, j).reshape(m, -1)
                x = _mla_out(o_lat, w_uv[j], w_mla_o[j], x)
                new["mla_ckv"].append(ckv)
                new["mla_kpe"].append(kpe)
            elif kind == 1:
                lam_init = 0.8 - 0.6 * math.exp(-0.3 * i)
                lams = (diff_lambda_q1[j], diff_lambda_k1[j], diff_lambda_q2[j], diff_lambda_k2[j])
                qkv = _linear(x, w_diff_qkv[j], gain=g_mix, name="diff_qkv")
                q_b, k32, k_b, v32, v_b = _diff_prep(qkv, cos, sin)
                if prompt:
                    o = _flash_diff(q_b, k_b, v_b, lams, diff_subln[j], lam_init, batch, seq)
                else:
                    q5 = q_b.reshape(m, DIFF_GROUPS, 2, 2, DIFF_HEAD_DIM)
                    eye_g = jnp.eye(DIFF_GROUPS, dtype=BF16)
                    eye_c = jnp.eye(2, dtype=BF16)
                    q_rows = jnp.einsum("tgrcd,gG,cC->tcrgGCd", q5, eye_g, eye_c).reshape(m, 16, 512)
                    v_rows = jnp.tile(v_b.astype(F32).reshape(m, DIFF_GROUPS, LANES), (1, 4, 1))
                    o = _decode_diff(page_table, q_rows, k_b, v_rows, pool_kt, pool_v, lams, diff_subln[j],
                                     lam_init, j)
                    o = jnp.transpose(o.reshape(m, 2, DIFF_GROUPS, LANES), (0, 2, 1, 3)).reshape(m, -1)
                x = _linear(o, w_diff_o[j], res=x, name="diff_o")
                new["diff_k"].append(k32)
                new["diff_v"].append(v32)
            else:
                z = _linear(x, w_ssd_z[j], gain=g_mix, name="ssd_in_z")
                xbc = _linear(x, w_ssd_xbc[j], gain=g_mix, name="ssd_in_xbc")
                dt = _linear(x, w_ssd_dt[j], gain=g_mix, name="ssd_in_dt")
                cw, cb = ssd_conv_w[j], ssd_conv_b[j].reshape(1, -1)
                dtb, a_neg = ssd_dtb[j].reshape(1, -1), ssd_a[j].reshape(1, -1)
                d_exp, nw = ssd_d_exp[j].reshape(1, -1), ssd_norm[j].reshape(1, -1)
                if prompt:
                    yn, st, buf = _ssd_prompt(z, xbc, dt, cw, cb, dtb, a_neg, expand, d_exp, nw, batch, seq)
                else:
                    xs, xdt, dec, bm, cm, nbuf = _ssd_decode_pre(
                        xbc, dt, jnp.transpose(state_conv[j], (1, 0, 2)), cw, cb, dtb, a_neg, expand)
                    y, st = _ssd_decode_state(xdt, dec, bm, cm, state_ssm[j].reshape(m, SSD_D_INNER, SSD_STATE))
                    yn = _ssd_decode_post(y.reshape(m, -1), xs, z, d_exp, nw)
                    buf = jnp.transpose(nbuf, (1, 0, 2))
                x = _linear(yn, w_ssd_out[j], res=x, name="ssd_out")
                new["ssm"].append(st.reshape(-1, SSD_HEADS, SSD_HEAD_DIM, SSD_STATE))
                new["conv"].append(buf)
            if prompt:
                kv_mem = _linear(mem_flat, w_xa_kv[i], gain=xa_mem_norm[i], name="xa_mem_kv")
                new["mem_k"].append(kv_mem[:, :d])
                new["mem_v"].append(kv_mem[:, d:])
                x = _xa_prompt(x, norm_xa[i], w_xa_q[i], kv_mem.reshape(batch, -1, 2 * d), w_xa_o[i], batch, seq)
            else:
                q = _linear(x, w_xa_q[i], gain=norm_xa[i], name="xa_q")
                o = _xa_decode(q, mem_k, mem_v, i)
                x = _linear(o, w_xa_o[i], res=x, name="xa_o")
            x = _ffn(x, norm_ffn[i], w_gate[i], w_up[i], w_down[i])
        return _final_norm(x, norm_final), {k: jnp.stack(v) for k, v in new.items() if v}

    pos_p = jnp.arange(seq, dtype=jnp.int32)
    pos_s = jnp.full((nb,), PAST_LEN, jnp.int32)
    y_p, new_p = run(x_prompt.reshape(batch * seq, d), pos_p, True)
    y_s, new_s = run(x_sample.reshape(nb, d), pos_s, False)

    n_b = new_p["diff_k"].shape[0]
    n_c = new_p["ssm"].shape[0]
    mem_t = mem_prompt.shape[1]
    return (
        y_p.reshape(batch, seq, d),
        y_s.reshape(nb, 1, d),
        new_p["mla_ckv"].reshape(n_a, batch, seq, MLA_KV_LORA),
        new_p["mla_kpe"].reshape(n_a, batch, seq, MLA_ROPE),
        new_p["diff_k"].reshape(n_b, batch, seq, DIFF_GROUPS, 2, DIFF_HEAD_DIM),
        new_p["diff_v"].reshape(n_b, batch, seq, DIFF_GROUPS, 2 * DIFF_HEAD_DIM),
        new_p["ssm"].reshape(n_c, batch, SSD_HEADS, SSD_HEAD_DIM, SSD_STATE),
        new_p["conv"],
        new_p["mem_k"].reshape(depth, batch, mem_t, XA_HEADS, XA_HEAD_DIM),
        new_p["mem_v"].reshape(depth, batch, mem_t, XA_HEADS, XA_HEAD_DIM),
        new_s["mla_ckv"].reshape(n_a, nb, 1, MLA_KV_LORA),
        new_s["mla_kpe"].reshape(n_a, nb, 1, MLA_ROPE),
        new_s["diff_k"].reshape(n_b, nb, 1, DIFF_GROUPS, 2, DIFF_HEAD_DIM),
        new_s["diff_v"].reshape(n_b, nb, 1, DIFF_GROUPS, 2 * DIFF_HEAD_DIM),
        new_s["ssm"].reshape(n_c, nb, SSD_HEADS, SSD_HEAD_DIM, SSD_STATE),
        new_s["conv"],
    )
```

```python
import functools
import math

import jax
import jax.numpy as jnp
from jax import lax
from jax.experimental import pallas as pl
from jax.experimental.pallas import tpu as pltpu

F32 = jnp.float32
BF16 = jnp.bfloat16

RMS_EPS = 1e-6
ROPE_THETA = 10000.0
NEG_INF = -1e30
PAST_LEN = 8192
PAGE_SIZE = 128

MLA_HEADS = 8
MLA_NOPE = 128
MLA_ROPE = 64
MLA_KV_LORA = 256
MLA_V = 128
MLA_QK = 384
MLA_SCALE = (MLA_NOPE + MLA_ROPE) ** -0.5

DIFF_HEADS = 8
DIFF_GROUPS = 4
DIFF_HEAD_DIM = 64
DIFF_SCALE = DIFF_HEAD_DIM ** -0.5

SSD_D_INNER = 2048
SSD_HEADS = 32
SSD_HEAD_DIM = 64
SSD_GROUPS = 4
SSD_STATE = 128
SSD_CONV = 4
SSD_CONV_DIM = SSD_D_INNER + 2 * SSD_GROUPS * SSD_STATE
SSD_CHUNK = 128

XA_HEADS = 4
XA_HEAD_DIM = 256
XA_SCALE = XA_HEAD_DIM ** -0.5

LOG2E = math.log2(math.e)
SOFTMAX_ROWS = 32
LANES = 128
VMEM_LIMIT = 48 * 1024 * 1024

NT_DIMS = (((1,), (1,)), ((), ()))
TN_DIMS = (((0,), (0,)), ((), ()))


def _params(sem):
    return pltpu.CompilerParams(dimension_semantics=sem, vmem_limit_bytes=VMEM_LIMIT)


def _rms(x, g):
    return x * lax.rsqrt(jnp.mean(x * x, axis=-1, keepdims=True) + RMS_EPS) * g


def _silu(x):
    return x * jax.nn.sigmoid(x)


def _dot(a, b):
    return jnp.dot(a, b, preferred_element_type=F32)


def _dot_nt(a, b):
    return lax.dot_general(a, b, NT_DIMS, preferred_element_type=F32)


def _dot_exact(a, b, dims=(((1,), (0,)), ((), ()))):
    return lax.dot_general(a, b, dims, precision=lax.Precision.HIGHEST, preferred_element_type=F32)


def _linear_kernel(*refs, has_norm, has_res, tn):
    refs = list(refs)
    x_ref = refs.pop(0)
    g_ref = refs.pop(0) if has_norm else None
    w_ref = refs.pop(0)
    r_ref = refs.pop(0) if has_res else None
    (o_ref,) = refs
    x = x_ref[...]
    if has_norm:
        x = _rms(x.astype(F32), g_ref[...])
    h = x.astype(BF16)
    for j in range(o_ref.shape[1] // tn):
        cols = slice(j * tn, (j + 1) * tn)
        acc = _dot(h, w_ref[:, cols])
        if has_res:
            acc = acc + r_ref[:, cols]
        o_ref[:, cols] = acc.astype(o_ref.dtype)


def _pick_tn(n):
    for tn in (512, 256, 128):
        if n % tn == 0:
            return tn
    return n


def _linear(x, w, gain=None, res=None, out_dtype=F32, x_col=0, name="linear"):
    m = x.shape[0]
    k, n = w.shape
    tm = min(m, 1024 if n <= 2048 else 512)
    in_specs = [pl.BlockSpec((tm, k), lambda i: (i, x_col))]
    args = [x]
    if gain is not None:
        in_specs.append(pl.BlockSpec((1, k), lambda i: (0, 0)))
        args.append(gain.reshape(1, k).astype(F32))
    in_specs.append(pl.BlockSpec((k, n), lambda i: (0, 0), pipeline_mode=pl.Buffered(1)))
    args.append(w)
    if res is not None:
        in_specs.append(pl.BlockSpec((tm, n), lambda i: (i, 0)))
        args.append(res)
    return pl.pallas_call(
        functools.partial(_linear_kernel, has_norm=gain is not None, has_res=res is not None, tn=_pick_tn(n)),
        out_shape=jax.ShapeDtypeStruct((m, n), out_dtype),
        grid=(m // tm,),
        in_specs=in_specs,
        out_specs=pl.BlockSpec((tm, n), lambda i: (i, 0)),
        compiler_params=_params(("parallel",)),
        name=name,
    )(*args)


def _ffn_kernel(x_ref, g_ref, wg_ref, wu_ref, wd_ref, o_ref, *, chunk):
    x = x_ref[...]
    h = _rms(x, g_ref[...]).astype(BF16)
    acc = x
    for c in range(wg_ref.shape[1] // chunk):
        cols = slice(c * chunk, (c + 1) * chunk)
        a = _silu(_dot(h, wg_ref[:, cols])) * _dot(h, wu_ref[:, cols])
        acc = acc + _dot(a.astype(BF16), wd_ref[cols, :])
    o_ref[...] = acc


def _ffn(x, gain, wg, wu, wd):
    m, d = x.shape
    hid = wg.shape[1]
    tm = min(m, 1024)
    resident = lambda shape: pl.BlockSpec(shape, lambda i: (0, 0), pipeline_mode=pl.Buffered(1))
    return pl.pallas_call(
        functools.partial(_ffn_kernel, chunk=256),
        out_shape=jax.ShapeDtypeStruct((m, d), F32),
        grid=(m // tm,),
        in_specs=[
            pl.BlockSpec((tm, d), lambda i: (i, 0)),
            pl.BlockSpec((1, d), lambda i: (0, 0)),
            resident((d, hid)),
            resident((d, hid)),
            resident((hid, d)),
        ],
        out_specs=pl.BlockSpec((tm, d), lambda i: (i, 0)),
        compiler_params=_params(("parallel",)),
        name="ffn",
    )(x, gain.reshape(1, d), wg, wu, wd)


def _norm_kernel(x_ref, g_ref, o_ref):
    o_ref[...] = _rms(x_ref[...], g_ref[...])


def _final_norm(x, gain):
    m, d = x.shape
    tm = min(m, 1024)
    return pl.pallas_call(
        _norm_kernel,
        out_shape=jax.ShapeDtypeStruct((m, d), F32),
        grid=(m // tm,),
        in_specs=[pl.BlockSpec((tm, d), lambda i: (i, 0)), pl.BlockSpec((1, d), lambda i: (0, 0))],
        out_specs=pl.BlockSpec((tm, d), lambda i: (i, 0)),
        compiler_params=_params(("parallel",)),
        name="final_norm",
    )(x, gain.reshape(1, d))


def _rope(x, cos, sin):
    n = x.shape[1]
    reps = n // LANES
    if reps > 1:
        cos = jnp.concatenate([cos] * reps, axis=1)
        sin = jnp.concatenate([sin] * reps, axis=1)
    lane = lax.broadcasted_iota(jnp.int32, x.shape, 1)
    first_half = (lane % 64) < 32
    partner = jnp.where(first_half, pltpu.roll(x, n - 32, 1), pltpu.roll(x, 32, 1))
    return x * cos + partner * sin


def _rope_tables(pos):
    inv = ROPE_THETA ** (-jnp.arange(32, dtype=F32) * 2.0 / 64)
    ang = pos.astype(F32)[:, None] * inv[None, :]
    c, s = jnp.cos(ang), jnp.sin(ang)
    return jnp.concatenate([c, c, c, c], axis=1), jnp.concatenate([-s, s, -s, s], axis=1)


def _mla_prep_kernel(q_ref, kv_ref, cos_ref, sin_ref, wuk_ref, kvg_ref,
                     qh_ref, kh_ref, ckv_ref, kpe_ref, *v_ref, absorb):
    cos, sin = cos_ref[...], sin_ref[...]
    q = q_ref[...] * (MLA_SCALE * LOG2E)
    q_pe = _rope(q[:, 1024:], cos, sin).astype(BF16)
    q_nope = q[:, :1024].astype(BF16)
    kv = kv_ref[...]
    ckv = _rms(kv[:, :256], kvg_ref[...])
    kpe = _rope(kv[:, 256:], cos, sin)
    ckv_ref[...] = ckv
    kpe_ref[...] = kpe[:, :64]
    ckv_b, kpe_b = ckv.astype(BF16), kpe.astype(BF16)
    if absorb:
        kh_ref[:, 0:256] = ckv_b
        kh_ref[:, 256:384] = kpe_b
    else:
        v_ref[0][...] = ckv_b
    for h in range(MLA_HEADS):
        head = slice(h * 128, (h + 1) * 128)
        if absorb:
            qh_ref[h, :, 0:256] = _dot(q_nope[:, head], wuk_ref[h]).astype(BF16)
            qh_ref[h, :, 256:384] = q_pe[:, head]
        else:
            qh_ref[h, :, 0:128] = q_nope[:, head]
            qh_ref[h, :, 128:256] = q_pe[:, head]
            kh_ref[h, :, 0:128] = _dot(ckv_b, wuk_ref[h]).astype(BF16)
            kh_ref[h, :, 128:256] = kpe_b


def _mla_prep(q, qkv_a, cos, sin, wuk, kv_gain, absorb):
    m = q.shape[0]
    tm = min(m, 512)
    nt = cos.shape[0] // tm
    row = lambda i: (i, 0)
    per_head = lambda i: (0, i, 0)
    if absorb:
        head_shapes = [jax.ShapeDtypeStruct((MLA_HEADS, m, MLA_QK), BF16), jax.ShapeDtypeStruct((m, MLA_QK), BF16)]
        head_specs = [pl.BlockSpec((MLA_HEADS, tm, MLA_QK), per_head), pl.BlockSpec((tm, MLA_QK), row)]
        extra_shapes, extra_specs = [], []
    else:
        head_shapes = [jax.ShapeDtypeStruct((MLA_HEADS, m, 256), BF16)] * 2
        head_specs = [pl.BlockSpec((MLA_HEADS, tm, 256), per_head)] * 2
        extra_shapes = [jax.ShapeDtypeStruct((m, MLA_KV_LORA), BF16)]
        extra_specs = [pl.BlockSpec((tm, MLA_KV_LORA), row)]
    return pl.pallas_call(
        functools.partial(_mla_prep_kernel, absorb=absorb),
        out_shape=tuple(head_shapes + [jax.ShapeDtypeStruct((m, MLA_KV_LORA), F32),
                                       jax.ShapeDtypeStruct((m, MLA_ROPE), F32)] + extra_shapes),
        grid=(m // tm,),
        in_specs=[
            pl.BlockSpec((tm, 2048), row),
            pl.BlockSpec((tm, MLA_QK), lambda i: (i, 1)),
            pl.BlockSpec((tm, LANES), lambda i: (i % nt, 0)),
            pl.BlockSpec((tm, LANES), lambda i: (i % nt, 0)),
            pl.BlockSpec((MLA_HEADS,) + wuk.shape[1:], lambda i: (0, 0, 0)),
            pl.BlockSpec((1, 256), lambda i: (0, 0)),
        ],
        out_specs=tuple(head_specs + [pl.BlockSpec((tm, MLA_KV_LORA), row), pl.BlockSpec((tm, MLA_ROPE), row)]
                        + extra_specs),
        compiler_params=_params(("parallel",)),
        name="mla_prep",
    )(q, qkv_a, cos, sin, wuk, kv_gain.reshape(1, 256))


def _tree(op, xs):
    xs = list(xs)
    while len(xs) > 1:
        xs = [op(xs[i], xs[i + 1]) if i + 1 < len(xs) else xs[i] for i in range(0, len(xs), 2)]
    return xs[0]


def _online_softmax(s, row0, m_scr, l_scr, acc_scr, masked):
    rows_n, tk = s.shape
    ch = SOFTMAX_ROWS
    acc_tiles = acc_scr.shape[1] // LANES
    out = []
    for c in range(rows_n // ch):
        rows = pl.ds(row0 + c * ch, ch)
        sc = s[c * ch:(c + 1) * ch, :]
        if masked:
            qpos = c * ch + lax.broadcasted_iota(jnp.int32, (ch, tk), 0)
            kpos = lax.broadcasted_iota(jnp.int32, (ch, tk), 1)
            sc = jnp.where(kpos <= qpos, sc, NEG_INF)
        tiles = [sc[:, t * LANES:(t + 1) * LANES] for t in range(tk // LANES)]
        m_prev = m_scr[rows, :]
        m_new = jnp.maximum(m_prev, jnp.max(_tree(jnp.maximum, tiles), axis=1, keepdims=True))
        alpha = jnp.exp2(m_prev - m_new)
        ps = [jnp.exp2(t - m_new) for t in tiles]
        if l_scr is not None:
            l_scr[rows, :] = alpha * l_scr[rows, :] + jnp.sum(_tree(jnp.add, ps), axis=1, keepdims=True)
        m_scr[rows, :] = m_new
        acc_scr[rows, :] = acc_scr[rows, :] * jnp.concatenate([alpha] * acc_tiles, axis=1)
        out.append(jnp.concatenate(ps, axis=1).astype(BF16))
    return jnp.concatenate(out, axis=0)


def _flash_mla_kernel(q_ref, k_ref, v_ref, o_ref, m_scr, l_scr, acc_scr, *, tile):
    qi, ki = pl.program_id(1), pl.program_id(2)

    @pl.when(ki == 0)
    def _():
        m_scr[...] = jnp.full_like(m_scr, NEG_INF)
        l_scr[...] = jnp.zeros_like(l_scr)
        acc_scr[...] = jnp.zeros_like(acc_scr)

    def all_heads(masked):
        def head(h, carry):
            row0 = pl.multiple_of(h * tile, tile)
            p = _online_softmax(_dot_nt(q_ref[h], k_ref[h]), row0, m_scr, l_scr, acc_scr, masked)
            acc_scr[pl.ds(row0, tile), :] += _dot(p, v_ref[...])
            return carry
        lax.fori_loop(0, MLA_HEADS, head, 0, unroll=4)

    @pl.when(ki < qi)
    def _():
        all_heads(False)

    @pl.when(ki == qi)
    def _():
        all_heads(True)
        for h in range(MLA_HEADS):
            rows = slice(h * tile, (h + 1) * tile)
            l = l_scr[rows, :]
            o = acc_scr[rows, :] / jnp.concatenate([l, l], axis=1)
            o_ref[:, h * 256:(h + 1) * 256] = o.astype(o_ref.dtype)


def _flash_mla(q_heads, k_heads, v, batch, seq, tile=512):
    nt = seq // tile
    return pl.pallas_call(
        functools.partial(_flash_mla_kernel, tile=tile),
        out_shape=jax.ShapeDtypeStruct((batch * seq, MLA_HEADS * MLA_KV_LORA), BF16),
        grid=(batch, nt, nt),
        in_specs=[
            pl.BlockSpec((MLA_HEADS, tile, 256), lambda b, qi, ki: (0, b * nt + qi, 0)),
            pl.BlockSpec((MLA_HEADS, tile, 256), lambda b, qi, ki: (0, b * nt + jnp.minimum(ki, qi), 0)),
            pl.BlockSpec((tile, MLA_KV_LORA), lambda b, qi, ki: (b * nt + jnp.minimum(ki, qi), 0)),
        ],
        out_specs=pl.BlockSpec((tile, MLA_HEADS * MLA_KV_LORA), lambda b, qi, ki: (b * nt + qi, 0)),
        scratch_shapes=[
            pltpu.VMEM((MLA_HEADS * tile, LANES), F32),
            pltpu.VMEM((MLA_HEADS * tile, LANES), F32),
            pltpu.VMEM((MLA_HEADS * tile, MLA_KV_LORA), F32),
        ],
        compiler_params=_params(("parallel", "parallel", "arbitrary")),
        name="flash_mla",
    )(q_heads, k_heads, v)


def _mla_out_kernel(o_ref, wuv_ref, wo_ref, x_ref, y_ref):
    o = o_ref[...].astype(BF16)
    heads = [_dot(o[:, h * 256:(h + 1) * 256], wuv_ref[h]).astype(BF16) for h in range(MLA_HEADS)]
    y_ref[...] = x_ref[...] + _dot(jnp.concatenate(heads, axis=1), wo_ref[...])


def _mla_out(o_lat, wuv, wo, x):
    m, d = x.shape
    tm = min(m, 512)
    return pl.pallas_call(
        _mla_out_kernel,
        out_shape=jax.ShapeDtypeStruct((m, d), F32),
        grid=(m // tm,),
        in_specs=[
            pl.BlockSpec((tm, 2048), lambda i: (i, 0)),
            pl.BlockSpec((MLA_HEADS, 256, 128), lambda i: (0, 0, 0)),
            pl.BlockSpec((1024, d), lambda i: (0, 0)),
            pl.BlockSpec((tm, d), lambda i: (i, 0)),
        ],
        out_specs=pl.BlockSpec((tm, d), lambda i: (i, 0)),
        compiler_params=_params(("parallel",)),
        name="mla_out",
    )(o_lat, wuv, wo, x)


def _decode_mla_kernel(pt_ref, q_ref, kn_ref, ckv_hbm, kpe_hbm, o_ref,
                       cbuf, pbuf, kb_scr, sem, *, layer, n_pages):
    b = pl.program_id(0)
    slot = b % 2

    def copies(tok, sl, p):
        page = pt_ref[tok * n_pages + p]
        return (pltpu.make_async_copy(ckv_hbm.at[layer, page], cbuf.at[sl, p], sem.at[0, sl]),
                pltpu.make_async_copy(kpe_hbm.at[layer, page], pbuf.at[sl, p], sem.at[1, sl]))

    def issue(tok, sl):
        def body(p, carry):
            for cp in copies(tok, sl, p):
                cp.start()
            return carry
        lax.fori_loop(0, n_pages, body, 0)

    def wait_all(tok, sl):
        def body(p, carry):
            for cp in copies(tok, sl, p):
                cp.wait()
            return carry
        lax.fori_loop(0, n_pages, body, 0)

    last = pl.num_programs(0) - 1

    @pl.when(b == 0)
    def _():
        issue(0, 0)

    wait_all(b, slot)

    nxt = jnp.minimum(b + 1, last)
    for p in range(n_pages):
        for cp in copies(nxt, 1 - slot, p):
            cp.start()

    n_keys = n_pages * PAGE_SIZE
    kb_scr[...] = cbuf[slot].reshape(n_keys, MLA_KV_LORA).astype(BF16)

    q = q_ref[0]
    kn = kn_ref[0].astype(F32)
    s_self = jnp.sum(q.astype(F32) * kn, axis=-1, keepdims=True)
    q_pe = q[:, 256:320]
    kpe_t = jnp.concatenate([pbuf[slot, p].astype(BF16) for p in range(n_pages)], axis=1)
    s = _dot_nt(q[:, :256], kb_scr[...]) + _dot(q_pe, kpe_t)
    m = jnp.maximum(jnp.max(s, axis=-1, keepdims=True), s_self)
    p = jnp.exp2(s - m)
    p_self = jnp.exp2(s_self - m)
    l = jnp.sum(p, axis=-1, keepdims=True) + p_self
    acc = _dot(p.astype(BF16), kb_scr[...]) + p_self * kn[:, :256]
    o_ref[0] = acc / l

    @pl.when(b == last)
    def _():
        wait_all(last, 1 - slot)


def _decode_mla(page_table, q_tok, k_new, cache_ckv, cache_kpe, layer):
    nb, n_pages = page_table.shape
    n_keys = n_pages * PAGE_SIZE
    return pl.pallas_call(
        functools.partial(_decode_mla_kernel, layer=layer, n_pages=n_pages),
        out_shape=jax.ShapeDtypeStruct((nb, MLA_HEADS, MLA_KV_LORA), F32),
        grid_spec=pltpu.PrefetchScalarGridSpec(
            num_scalar_prefetch=1,
            grid=(nb,),
            in_specs=[
                pl.BlockSpec((1, MLA_HEADS, MLA_QK), lambda b, pt: (b, 0, 0)),
                pl.BlockSpec((1, 1, MLA_QK), lambda b, pt: (b, 0, 0)),
                pl.BlockSpec(memory_space=pl.ANY),
                pl.BlockSpec(memory_space=pl.ANY),
            ],
            out_specs=pl.BlockSpec((1, MLA_HEADS, MLA_KV_LORA), lambda b, pt: (b, 0, 0)),
            scratch_shapes=[
                pltpu.VMEM((2, n_pages, PAGE_SIZE, MLA_KV_LORA), F32),
                pltpu.VMEM((2, n_pages, MLA_ROPE, PAGE_SIZE), F32),
                pltpu.VMEM((n_keys, MLA_KV_LORA), BF16),
                pltpu.SemaphoreType.DMA((2, 2)),
            ],
        ),
        compiler_params=_params(("arbitrary",)),
        name="decode_mla",
    )(page_table.reshape(-1), q_tok, k_new.reshape(nb, 1, MLA_QK), cache_ckv, cache_kpe)


def _diff_prep_kernel(qkv_ref, cos_ref, sin_ref, q_ref, k32_ref, kb_ref, v32_ref, vb_ref, *, k_transposed):
    qkv = qkv_ref[...]
    qk = _rope(qkv[:, :1536], cos_ref[...], sin_ref[...])
    q_ref[...] = (qk[:, :1024] * (DIFF_SCALE * LOG2E)).astype(BF16)
    k = qk[:, 1024:]
    v = qkv[:, 1536:]
    if k_transposed:
        k32_ref[0] = k.T
    else:
        k32_ref[...] = k
    kb_ref[...] = k.astype(BF16)
    v32_ref[...] = v
    vb_ref[...] = v.astype(BF16)


def _diff_prep(qkv, cos, sin, batch=None):
    m = qkv.shape[0]
    tm = min(m, 512)
    nt = cos.shape[0] // tm
    row = lambda i: (i, 0)
    if batch is None:
        k32_shape, k32_spec = jax.ShapeDtypeStruct((m, 512), F32), pl.BlockSpec((tm, 512), row)
    else:
        k32_shape = jax.ShapeDtypeStruct((batch, 512, m // batch), F32)
        k32_spec = pl.BlockSpec((1, 512, tm), lambda i: (i // nt, 0, i % nt))
    return pl.pallas_call(
        functools.partial(_diff_prep_kernel, k_transposed=batch is not None),
        out_shape=(
            jax.ShapeDtypeStruct((m, 1024), BF16),
            k32_shape,
            jax.ShapeDtypeStruct((m, 512), BF16),
            jax.ShapeDtypeStruct((m, 512), F32),
            jax.ShapeDtypeStruct((m, 512), BF16),
        ),
        grid=(m // tm,),
        in_specs=[
            pl.BlockSpec((tm, 2048), row),
            pl.BlockSpec((tm, LANES), lambda i: (i % nt, 0)),
            pl.BlockSpec((tm, LANES), lambda i: (i % nt, 0)),
        ],
        out_specs=(
            pl.BlockSpec((tm, 1024), row),
            k32_spec,
            pl.BlockSpec((tm, 512), row),
            pl.BlockSpec((tm, 512), row),
            pl.BlockSpec((tm, 512), row),
        ),
        compiler_params=_params(("parallel",)),
        name="diff_prep",
    )(qkv, cos, sin)


def _diff_lambda(lq1_ref, lk1_ref, lq2_ref, lk2_ref, lam_init):
    e1 = jnp.exp(jnp.sum(lq1_ref[...] * lk1_ref[...], axis=-1, keepdims=True))
    e2 = jnp.exp(jnp.sum(lq2_ref[...] * lk2_ref[...], axis=-1, keepdims=True))
    return e1 - e2 + lam_init


def _flash_diff_kernel(lq1_ref, lk1_ref, lq2_ref, lk2_ref, sub_ref, q_ref, k_ref, v_ref, o_ref,
                       m_scr, acc_scr, *, tile, lam_init):
    qi, ki = pl.program_id(2), pl.program_id(3)

    @pl.when(ki == 0)
    def _():
        m_scr[...] = jnp.full_like(m_scr, NEG_INF)
        acc_scr[...] = jnp.zeros_like(acc_scr)

    def all_rows(masked):
        q = q_ref[...]
        k = k_ref[...]
        v_ext = jnp.concatenate([v_ref[...], jnp.ones((tile, LANES), BF16)], axis=1)
        comp0 = lax.broadcasted_iota(jnp.int32, (tile, LANES), 1) < DIFF_HEAD_DIM
        zero = jnp.zeros((tile, LANES), BF16)
        for r in range(2):
            head = q[:, r * LANES:(r + 1) * LANES]
            for c in range(2):
                qc = jnp.where(comp0, head, zero) if c == 0 else jnp.where(comp0, zero, head)
                row0 = (2 * r + c) * tile
                p = _online_softmax(_dot_nt(qc, k), row0, m_scr, None, acc_scr, masked)
                acc_scr[row0:row0 + tile, :] += _dot(p, v_ext)

    @pl.when(ki < qi)
    def _():
        all_rows(False)

    @pl.when(ki == qi)
    def _():
        all_rows(True)
        lam = _diff_lambda(lq1_ref, lk1_ref, lq2_ref, lk2_ref, lam_init)
        g = sub_ref[...]
        for r in range(2):
            o = []
            for c in range(2):
                rows = slice((2 * r + c) * tile, (2 * r + c + 1) * tile)
                o.append(acc_scr[rows, :LANES] / acc_scr[rows, LANES:])
            d = _rms(o[0] - lam * o[1], g) * (1.0 - lam_init)
            o_ref[:, r * LANES:(r + 1) * LANES] = d.astype(o_ref.dtype)


def _flash_diff(q, k, v, lams, subln, lam_init, batch, seq, tile=512):
    nt = seq // tile

    def kv_map(b, g, qi, ki):
        return (b * nt + jnp.minimum(ki, qi), g)

    vec = pl.BlockSpec((1, DIFF_HEAD_DIM), lambda b, g, qi, ki: (0, 0))
    return pl.pallas_call(
        functools.partial(_flash_diff_kernel, tile=tile, lam_init=lam_init),
        out_shape=jax.ShapeDtypeStruct((batch * seq, 1024), BF16),
        grid=(batch, DIFF_GROUPS, nt, nt),
        in_specs=[
            vec, vec, vec, vec,
            pl.BlockSpec((1, LANES), lambda b, g, qi, ki: (0, 0)),
            pl.BlockSpec((tile, 256), lambda b, g, qi, ki: (b * nt + qi, g)),
            pl.BlockSpec((tile, LANES), kv_map),
            pl.BlockSpec((tile, LANES), kv_map),
        ],
        out_specs=pl.BlockSpec((tile, 256), lambda b, g, qi, ki: (b * nt + qi, g)),
        scratch_shapes=[
            pltpu.VMEM((4 * tile, LANES), F32),
            pltpu.VMEM((4 * tile, 2 * LANES), F32),
        ],
        compiler_params=_params(("parallel", "parallel", "parallel", "arbitrary")),
        name="flash_diff",
    )(*[a.reshape(1, DIFF_HEAD_DIM) for a in lams], subln.reshape(1, LANES), q, k, v)


def _decode_diff_kernel(pt_ref, lq1_ref, lk1_ref, lq2_ref, lk2_ref, sub_ref, q_ref, kn_ref, vn_ref,
                        k_hbm, v_hbm, o_ref, kbuf, vbuf, sem, m_scr, l_scr, acc_scr,
                        *, layer, n_pages, unit, lam_init):
    b, u = pl.program_id(0), pl.program_id(1)
    n_units = n_pages // unit
    step = b * n_units + u
    slot = step % 2

    def copies(st, sl, p):
        tok, un = st // n_units, st % n_units
        page = pt_ref[tok * n_pages + un * unit + p]
        return (pltpu.make_async_copy(k_hbm.at[layer, page], kbuf.at[sl, p], sem.at[0, sl]),
                pltpu.make_async_copy(v_hbm.at[layer, page], vbuf.at[sl, p], sem.at[1, sl]))

    def issue(st, sl):
        def body(p, carry):
            for cp in copies(st, sl, p):
                cp.start()
            return carry
        lax.fori_loop(0, unit, body, 0)

    @pl.when(step == 0)
    def _():
        issue(0, 0)

    @pl.when(step + 1 < pl.num_programs(0) * n_units)
    def _():
        issue(step + 1, 1 - slot)

    def wait_body(p, carry):
        for cp in copies(step, slot, p):
            cp.wait()
        return carry
    lax.fori_loop(0, unit, wait_body, 0)

    q = q_ref[0]

    @pl.when(u == 0)
    def _():
        m_scr[...] = jnp.sum(q.astype(F32) * kn_ref[0].astype(F32), axis=-1, keepdims=True)
        l_scr[...] = jnp.ones_like(l_scr)
        acc_scr[...] = vn_ref[0]

    kt = jnp.concatenate([kbuf[slot, pg].astype(BF16) for pg in range(unit)], axis=1)
    s = _dot(q, kt)
    m_old = m_scr[...]
    m_new = jnp.maximum(m_old, jnp.max(s, axis=-1, keepdims=True))
    alpha = jnp.exp2(m_old - m_new)
    p = jnp.exp2(s - m_new)
    l_scr[...] = alpha * l_scr[...] + jnp.sum(p, axis=-1, keepdims=True)
    m_scr[...] = m_new
    pb = p.astype(BF16)
    halves = []
    for gp in range(DIFF_GROUPS // 2):
        out = jnp.zeros((16, 2 * LANES), F32)
        for pp in range(unit // 2):
            stacked = []
            for pg in (2 * pp, 2 * pp + 1):
                side = [vbuf[slot, pg, pl.ds(2 * gp + i, PAGE_SIZE, stride=DIFF_GROUPS), :] for i in range(2)]
                stacked.append(jnp.concatenate(side, axis=1).astype(BF16))
            out = out + _dot(pb[:, 2 * pp * PAGE_SIZE:(2 * pp + 2) * PAGE_SIZE], jnp.concatenate(stacked, axis=0))
        halves += [out[:, :LANES], out[:, LANES:]]
    row_g = lax.broadcasted_iota(jnp.int32, (16, LANES), 0) % DIFF_GROUPS
    pv = jnp.where(row_g == 0, halves[0], jnp.where(row_g == 1, halves[1],
                                                    jnp.where(row_g == 2, halves[2], halves[3])))
    acc_scr[...] = alpha * acc_scr[...] + pv

    @pl.when(u == n_units - 1)
    def _():
        lam = _diff_lambda(lq1_ref, lk1_ref, lq2_ref, lk2_ref, lam_init)
        on = acc_scr[...] / l_scr[...]
        d = on[0:8] - lam * on[8:16]
        o_ref[0] = _rms(d, sub_ref[...]) * (1.0 - lam_init)


def _decode_diff(page_table, q_rows, k_new, v_rows, cache_kt, cache_v, lams, subln, lam_init, layer, unit=16):
    nb, n_pages = page_table.shape
    vec = pl.BlockSpec((1, DIFF_HEAD_DIM), lambda b, u, pt: (0, 0))
    return pl.pallas_call(
        functools.partial(_decode_diff_kernel, layer=layer, n_pages=n_pages, unit=unit, lam_init=lam_init),
        out_shape=jax.ShapeDtypeStruct((nb, 8, LANES), F32),
        grid_spec=pltpu.PrefetchScalarGridSpec(
            num_scalar_prefetch=1,
            grid=(nb, n_pages // unit),
            in_specs=[
                vec, vec, vec, vec,
                pl.BlockSpec((1, LANES), lambda b, u, pt: (0, 0)),
                pl.BlockSpec((1, 16, 512), lambda b, u, pt: (b, 0, 0)),
                pl.BlockSpec((1, 1, 512), lambda b, u, pt: (b, 0, 0)),
                pl.BlockSpec((1, 16, LANES), lambda b, u, pt: (b, 0, 0)),
                pl.BlockSpec(memory_space=pl.ANY),
                pl.BlockSpec(memory_space=pl.ANY),
            ],
            out_specs=pl.BlockSpec((1, 8, LANES), lambda b, u, pt: (b, 0, 0)),
            scratch_shapes=[
                pltpu.VMEM((2, unit, 512, PAGE_SIZE), F32),
                pltpu.VMEM((2, unit, 512, LANES), F32),
                pltpu.SemaphoreType.DMA((2, 2)),
                pltpu.VMEM((16, 1), F32),
                pltpu.VMEM((16, 1), F32),
                pltpu.VMEM((16, LANES), F32),
            ],
        ),
        compiler_params=_params(("arbitrary", "arbitrary")),
        name="decode_diff",
    )(page_table.reshape(-1), *[a.reshape(1, DIFF_HEAD_DIM) for a in lams], subln.reshape(1, LANES),
      q_rows, k_new.reshape(nb, 1, 512), v_rows, cache_kt, cache_v)


def _ssd_gate_norm(y, xs, z, d_ref, nw_ref):
    yg = (y + d_ref[...] * xs) * _silu(z)
    w = SSD_D_INNER // SSD_GROUPS
    parts = []
    for g in range(SSD_GROUPS):
        part = yg[:, g * w:(g + 1) * w]
        parts.append(part * lax.rsqrt(jnp.mean(part * part, axis=-1, keepdims=True) + RMS_EPS))
    return jnp.concatenate(parts, axis=1) * nw_ref[...]


def _ssd_prompt_kernel(z_ref, xbc_ref, dt_ref, cw_ref, cb_ref, dtb_ref, a_ref, e_ref, d_ref, nw_ref,
                       yn_ref, st_ref, buf_ref, xp_scr, s_scr, *, chunk):
    c = pl.program_id(1)

    @pl.when(c == 0)
    def _():
        xp_scr[0:8, :] = jnp.zeros((8, SSD_CONV_DIM), F32)
        s_scr[...] = jnp.zeros_like(s_scr)

    xp_scr[8:8 + chunk, :] = xbc_ref[...]
    conv = cb_ref[...] + sum(xp_scr[5 + k:5 + k + chunk, :] * cw_ref[k:k + 1, :] for k in range(SSD_CONV))
    u = _silu(conv)
    xs = u[:, :SSD_D_INNER]
    bm = u[:, SSD_D_INNER:SSD_D_INNER + 512]
    cm = u[:, SSD_D_INNER + 512:]

    dt = jax.nn.softplus(dt_ref[...] + dtb_ref[...])
    a = dt * a_ref[...]
    row = lax.broadcasted_iota(jnp.int32, (chunk, chunk), 0)
    col = lax.broadcasted_iota(jnp.int32, (chunk, chunk), 1)
    causal = col <= row
    tri = jnp.where(causal, 1.0, 0.0).astype(BF16)
    a_cs = _dot(jnp.concatenate([tri, tri, tri], axis=1),
                jnp.concatenate(_split_bf16x3(a), axis=0).astype(BF16))
    a_cs_t = a_cs.T
    a_last = a_cs[chunk - 1:chunk, :]
    cols = jnp.concatenate([dt, jnp.exp(a_cs), jnp.exp(a_last - a_cs)], axis=0)
    head_lane = lax.broadcasted_iota(jnp.int32, cols.shape, 1) < SSD_HEADS
    hi, mid, lo = _split_bf16x3(jnp.where(head_lane, cols, 0.0))
    packed = hi + pltpu.roll(mid, SSD_HEADS, 1) + pltpu.roll(lo, 2 * SSD_HEADS, 1)
    ex = _dot(packed.astype(BF16), e_ref[...])
    dtx, ecs, dend = ex[:chunk], ex[chunk:2 * chunk], ex[2 * chunk:]
    xdt = xs * dtx
    xdt_b = xdt.astype(BF16)
    xdtd_b = (xdt * dend).astype(BF16)
    etot = ecs[chunk - 1:chunk, :]
    low_head = lax.broadcasted_iota(jnp.int32, (chunk, LANES), 1) < SSD_HEAD_DIM

    ys = []
    for g in range(SSD_GROUPS):
        bg = bm[:, g * 128:(g + 1) * 128]
        cg = cm[:, g * 128:(g + 1) * 128].astype(BF16)
        cb = _dot_nt(cg, bg.astype(BF16))
        bg_t = bg.T.astype(BF16)
        for pi in range(4):
            pair = g * 4 + pi
            sl = slice(pair * LANES, (pair + 1) * LANES)
            ms = []
            for h in (2 * pair, 2 * pair + 1):
                seg = a_cs[:, h:h + 1] - a_cs_t[h:h + 1, :]
                ms.append((cb * jnp.exp(jnp.where(causal, seg, NEG_INF))).astype(BF16))
            y_diag = jnp.where(low_head, _dot(ms[0], xdt_b[:, sl]), _dot(ms[1], xdt_b[:, sl]))
            st = s_scr[:, sl]
            y_off = _dot(cg, st.astype(BF16)) * ecs[:, sl]
            s_scr[:, sl] = etot[:, sl] * st + _dot(bg_t, xdtd_b[:, sl])
            ys.append(y_diag + y_off)

    y = jnp.concatenate(ys, axis=1)
    yn_ref[...] = _ssd_gate_norm(y, xs, z_ref[...], d_ref, nw_ref).astype(yn_ref.dtype)
    xp_scr[0:8, :] = xp_scr[chunk:chunk + 8, :]

    @pl.when(c == pl.num_programs(1) - 1)
    def _():
        st_ref[0] = s_scr[...].T
        buf_ref[0] = xp_scr[chunk + 5:chunk + 8, :]


def _ssd_prompt(z, xbc, dt, cw, cb, dtb, a_neg, expand, d_exp, nw, batch, seq, chunk=SSD_CHUNK):
    nc = seq // chunk
    row = lambda b, c: (b * nc + c, 0)
    const = lambda b, c: (0, 0)
    return pl.pallas_call(
        functools.partial(_ssd_prompt_kernel, chunk=chunk),
        out_shape=(
            jax.ShapeDtypeStruct((batch * seq, SSD_D_INNER), BF16),
            jax.ShapeDtypeStruct((batch, SSD_D_INNER, SSD_STATE), F32),
            jax.ShapeDtypeStruct((batch, SSD_CONV - 1, SSD_CONV_DIM), F32),
        ),
        grid=(batch, nc),
        in_specs=[
            pl.BlockSpec((chunk, SSD_D_INNER), row),
            pl.BlockSpec((chunk, SSD_CONV_DIM), row),
            pl.BlockSpec((chunk, LANES), row),
            pl.BlockSpec((SSD_CONV, SSD_CONV_DIM), const),
            pl.BlockSpec((1, SSD_CONV_DIM), const),
            pl.BlockSpec((1, LANES), const),
            pl.BlockSpec((1, LANES), const),
            pl.BlockSpec((LANES, SSD_D_INNER), const),
            pl.BlockSpec((1, SSD_D_INNER), const),
            pl.BlockSpec((1, SSD_D_INNER), const),
        ],
        out_specs=(
            pl.BlockSpec((chunk, SSD_D_INNER), row),
            pl.BlockSpec((1, SSD_D_INNER, SSD_STATE), lambda b, c: (b, 0, 0)),
            pl.BlockSpec((1, SSD_CONV - 1, SSD_CONV_DIM), lambda b, c: (b, 0, 0)),
        ),
        scratch_shapes=[
            pltpu.VMEM((chunk + 8, SSD_CONV_DIM), F32),
            pltpu.VMEM((SSD_STATE, SSD_D_INNER), F32),
        ],
        compiler_params=_params(("parallel", "arbitrary")),
        name="ssd_prompt",
    )(z, xbc, dt, cw, cb, dtb, a_neg, expand, d_exp, nw)


def _ssd_decode_pre_kernel(xbc_ref, dt_ref, buf_ref, cw_ref, cb_ref, dtb_ref, a_ref, e_ref,
                           xs_ref, xdt_ref, dec_ref, b_ref, c_ref, nbuf_ref):
    xbc = xbc_ref[...]
    conv = cb_ref[...] + xbc * cw_ref[3:4, :]
    for k in range(SSD_CONV - 1):
        conv = conv + buf_ref[k] * cw_ref[k:k + 1, :]
    u = _silu(conv)
    xs = u[:, :SSD_D_INNER]
    dt = jax.nn.softplus(dt_ref[...] + dtb_ref[...])
    ex = _dot_exact(jnp.concatenate([dt, dt * a_ref[...]], axis=0), e_ref[...])
    t = xbc.shape[0]
    xs_ref[...] = xs
    xdt_ref[...] = xs * ex[:t]
    dec_ref[...] = jnp.exp(ex[t:])
    b_ref[...] = u[:, SSD_D_INNER:SSD_D_INNER + 512]
    c_ref[...] = u[:, SSD_D_INNER + 512:]
    nbuf_ref[0] = buf_ref[1]
    nbuf_ref[1] = buf_ref[2]
    nbuf_ref[2] = xbc


def _ssd_decode_pre(xbc, dt, buf, cw, cb, dtb, a_neg, expand):
    t = xbc.shape[0]
    wide = jax.ShapeDtypeStruct((t, SSD_D_INNER), F32)
    grp = jax.ShapeDtypeStruct((t, 512), F32)
    return pl.pallas_call(
        _ssd_decode_pre_kernel,
        out_shape=(wide, wide, wide, grp, grp, jax.ShapeDtypeStruct((SSD_CONV - 1, t, SSD_CONV_DIM), F32)),
        compiler_params=pltpu.CompilerParams(vmem_limit_bytes=VMEM_LIMIT),
        name="ssd_decode_pre",
    )(xbc, dt, buf, cw, cb, dtb, a_neg, expand)


def _split_bf16x3(x):
    hi = x.astype(BF16).astype(F32)
    mid = (x - hi).astype(BF16).astype(F32)
    lo = (x - hi - mid).astype(BF16).astype(F32)
    return hi, mid, lo


def _ssd_decode_state_kernel(xdt_ref, dec_ref, b_ref, c_ref, st_ref, y_ref, nst_ref):
    xdt = xdt_ref[0]
    lane_g = lax.broadcasted_iota(jnp.int32, (8, SSD_D_INNER), 1) // 512
    row = lax.broadcasted_iota(jnp.int32, (8, SSD_D_INNER), 0)
    lhs = jnp.where(row == lane_g, xdt, 0.0) + jnp.where(row == 4, dec_ref[0], 0.0)
    brow = lax.broadcasted_iota(jnp.int32, (8, LANES), 0)
    bmat = jnp.zeros((8, LANES), F32)
    cmat = jnp.zeros((8, LANES), F32)
    for g in range(SSD_GROUPS):
        bmat = jnp.where(brow == g, b_ref[0][:, g * 128:(g + 1) * 128], bmat)
        cmat = jnp.where(brow == g, c_ref[0][:, g * 128:(g + 1) * 128], cmat)
    rhs = jnp.concatenate([bmat, (brow == 4).astype(F32)], axis=1)
    lh, lm, ll = _split_bf16x3(lhs)
    rh, rm, rl = _split_bf16x3(rhs)
    lhs6 = jnp.concatenate([lh, lh, lm, lm, lh, ll], axis=0).astype(BF16)
    rhs6 = jnp.concatenate([rh, rm, rh, rm, rl, rh], axis=0).astype(BF16)
    both = lax.dot_general(lhs6, rhs6, TN_DIMS, preferred_element_type=F32)
    new = st_ref[0] * both[:, LANES:] + both[:, :LANES]
    nst_ref[0] = new
    yg = _dot_nt(cmat.astype(BF16), new.astype(BF16))
    y_ref[0] = jnp.sum(jnp.where(row == lane_g, yg, 0.0), axis=0, keepdims=True)


def _ssd_decode_state(xdt, dec, bm, cm, state):
    t = xdt.shape[0]
    tok3 = lambda w: pl.BlockSpec((1, 1, w), lambda i: (i, 0, 0))
    st_spec = pl.BlockSpec((1, SSD_D_INNER, SSD_STATE), lambda i: (i, 0, 0))
    return pl.pallas_call(
        _ssd_decode_state_kernel,
        out_shape=(jax.ShapeDtypeStruct((t, 1, SSD_D_INNER), F32),
                   jax.ShapeDtypeStruct((t, SSD_D_INNER, SSD_STATE), F32)),
        grid=(t,),
        in_specs=[tok3(SSD_D_INNER), tok3(SSD_D_INNER), tok3(512), tok3(512), st_spec],
        out_specs=(tok3(SSD_D_INNER), st_spec),
        compiler_params=_params(("parallel",)),
        name="ssd_decode_state",
    )(xdt.reshape(t, 1, -1), dec.reshape(t, 1, -1), bm.reshape(t, 1, -1), cm.reshape(t, 1, -1), state)


def _ssd_decode_post_kernel(y_ref, xs_ref, z_ref, d_ref, nw_ref, o_ref):
    o_ref[...] = _ssd_gate_norm(y_ref[...], xs_ref[...], z_ref[...], d_ref, nw_ref).astype(o_ref.dtype)


def _ssd_decode_post(y, xs, z, d_exp, nw):
    return pl.pallas_call(
        _ssd_decode_post_kernel,
        out_shape=jax.ShapeDtypeStruct(y.shape, BF16),
        compiler_params=pltpu.CompilerParams(vmem_limit_bytes=VMEM_LIMIT),
        name="ssd_decode_post",
    )(y, xs, z, d_exp, nw)


def _xa_prompt_kernel(x_ref, g_ref, wq_ref, k_ref, v_ref, wo_ref, o_ref):
    x = x_ref[...]
    q = _dot(_rms(x, g_ref[...]).astype(BF16), wq_ref[...]).astype(BF16)
    k = k_ref[0].astype(BF16)
    v = v_ref[0].astype(BF16)
    heads = []
    for h in range(XA_HEADS):
        sl = slice(h * XA_HEAD_DIM, (h + 1) * XA_HEAD_DIM)
        s = _dot_nt(q[:, sl], k[:, sl]) * XA_SCALE
        p = jnp.exp(s - jnp.max(s, axis=-1, keepdims=True))
        o = _dot(p.astype(BF16), v[:, sl]) / jnp.sum(p, axis=-1, keepdims=True)
        heads.append(o.astype(BF16))
    o_ref[...] = x + _dot(jnp.concatenate(heads, axis=1), wo_ref[...])


def _xa_prompt(x, gain, wq, kv_mem, wo, batch, seq):
    m, d = x.shape
    tm = 512
    nt = seq // tm
    mem = kv_mem.shape[1]
    return pl.pallas_call(
        _xa_prompt_kernel,
        out_shape=jax.ShapeDtypeStruct((m, d), F32),
        grid=(m // tm,),
        in_specs=[
            pl.BlockSpec((tm, d), lambda i: (i, 0)),
            pl.BlockSpec((1, d), lambda i: (0, 0)),
            pl.BlockSpec((d, d), lambda i: (0, 0)),
            pl.BlockSpec((1, mem, d), lambda i: (i // nt, 0, 0)),
            pl.BlockSpec((1, mem, d), lambda i: (i // nt, 0, 1)),
            pl.BlockSpec((d, d), lambda i: (0, 0)),
        ],
        out_specs=pl.BlockSpec((tm, d), lambda i: (i, 0)),
        compiler_params=_params(("parallel",)),
        name="xa_prompt",
    )(x, gain.reshape(1, d), wq, kv_mem, kv_mem, wo)


def _xa_decode_kernel(q_ref, k_ref, v_ref, o_ref, *, tokens):
    n = k_ref.shape[2]
    rows = 8 * tokens
    lane = lax.broadcasted_iota(jnp.int32, (rows, n), 1)
    own = lane % 8 == lax.broadcasted_iota(jnp.int32, (rows, n), 0) % 8
    first_half = lane % 8 < XA_HEADS
    strides = [8 << i for i in range((n // 8).bit_length() - 1)]

    def over_tokens(op, x):
        for sh in strides:
            x = op(x, pltpu.roll(x, sh, 1))
        return x

    g = jnp.concatenate([_dot_nt(q_ref[0, t].astype(BF16), k_ref[0, t].astype(BF16)) for t in range(tokens)],
                        axis=0)
    part = jnp.sum(jnp.where(own, g, 0.0).reshape(tokens, 8, n), axis=1, keepdims=True)
    part = jnp.broadcast_to(part, (tokens, 8, n)).reshape(rows, n)
    s = (part + pltpu.roll(part, n - XA_HEADS, 1)) * XA_SCALE
    p = jnp.exp(s - over_tokens(jnp.maximum, s))
    p = p / over_tokens(jnp.add, p)
    p = jnp.where(own, jnp.where(first_half, p, pltpu.roll(p, XA_HEADS, 1)), 0.0)
    for t in range(tokens):
        o_ref[0, t] = _dot(p[8 * t:8 * t + 8].astype(BF16), v_ref[0, t].astype(BF16))


def _xa_decode(q, mem_k, mem_v, layer, tokens=4):
    t, d = q.shape
    rows = mem_k.shape[2]
    half = XA_HEAD_DIM // 2
    q8 = jnp.transpose(q.reshape(t, XA_HEADS, 2, half), (0, 2, 1, 3)).reshape(t // tokens, tokens, 8, half)
    kv_spec = pl.BlockSpec((1, tokens, rows, half), lambda i: (layer, i, 0, 0))
    tok_spec = pl.BlockSpec((1, tokens, 8, half), lambda i: (i, 0, 0, 0))
    o8 = pl.pallas_call(
        functools.partial(_xa_decode_kernel, tokens=tokens),
        out_shape=jax.ShapeDtypeStruct((t // tokens, tokens, 8, half), F32),
        grid=(t // tokens,),
        in_specs=[tok_spec, kv_spec, kv_spec],
        out_specs=tok_spec,
        compiler_params=_params(("parallel",)),
        name="xa_decode",
    )(q8, mem_k, mem_v)
    return jnp.transpose(o8.reshape(t, 2, XA_HEADS, half), (0, 2, 1, 3)).reshape(t, d)


def kernel(x_prompt, x_sample, cache_mla_ckv, cache_mla_kpe, cache_diff_k, cache_diff_v, state_ssm, state_conv, cache_mem_k, cache_mem_v, page_table, mem_prompt, norm_mix, norm_xa, norm_ffn, norm_final, xa_mem_norm, xa_wq, xa_wk, xa_wv, xa_wo, ffn_w_gate, ffn_w_up, ffn_w_down, mla_wq_a, mla_q_norm, mla_wq_b, mla_wkv_a, mla_kv_norm, mla_w_uk, mla_w_uv, mla_wo, diff_wq, diff_wk, diff_wv, diff_lambda_q1, diff_lambda_k1, diff_lambda_q2, diff_lambda_k2, diff_subln, diff_wo, ssd_w_in, ssd_conv_w, ssd_conv_b, ssd_dt_bias, ssd_A_log, ssd_D, ssd_norm, ssd_w_out):
    batch, seq, d = x_prompt.shape
    nb = x_sample.shape[0]
    depth = norm_mix.shape[0]
    n_mix = 3
    bf = lambda w: w.astype(BF16)

    n_a = mla_wq_a.shape[0]
    wkv_a = jnp.pad(mla_wkv_a, ((0, 0), (0, 0), (0, MLA_QK - mla_wkv_a.shape[2])))
    w_qkv_a = bf(jnp.concatenate([mla_wq_a, wkv_a], axis=2))
    wq_b = mla_wq_b.reshape(n_a, -1, MLA_HEADS, MLA_NOPE + MLA_ROPE)
    wq_b = bf(jnp.concatenate([
        wq_b[..., :MLA_NOPE].reshape(n_a, -1, MLA_HEADS * MLA_NOPE),
        jnp.pad(wq_b[..., MLA_NOPE:], ((0, 0), (0, 0), (0, 0), (0, 64))).reshape(n_a, -1, MLA_HEADS * 128),
    ], axis=2))
    w_uk = bf(jnp.transpose(mla_w_uk, (0, 2, 3, 1)))
    w_uk_t = bf(jnp.transpose(mla_w_uk, (0, 2, 1, 3)))
    w_uv = bf(jnp.transpose(mla_w_uv, (0, 2, 1, 3)))
    w_mla_o = bf(mla_wo)
    w_diff_qkv = bf(jnp.concatenate([diff_wq, diff_wk, diff_wv], axis=2))
    w_diff_o = bf(diff_wo)
    w_ssd_z = bf(ssd_w_in[:, :, :SSD_D_INNER])
    w_ssd_xbc = bf(ssd_w_in[:, :, SSD_D_INNER:SSD_D_INNER + SSD_CONV_DIM])
    w_ssd_dt = bf(jnp.pad(ssd_w_in[:, :, SSD_D_INNER + SSD_CONV_DIM:], ((0, 0), (0, 0), (0, LANES - SSD_HEADS))))
    w_ssd_out = bf(ssd_w_out)
    pad_heads = lambda a: jnp.pad(a, ((0, 0), (0, LANES - SSD_HEADS)))
    ssd_dtb = pad_heads(ssd_dt_bias)
    ssd_a = pad_heads(-jnp.exp(ssd_A_log.astype(F32)))
    ssd_d_exp = jnp.repeat(ssd_D, SSD_HEAD_DIM, axis=1)
    lane_head = jnp.arange(SSD_D_INNER)[None, :] // SSD_HEAD_DIM
    expand = (jnp.arange(LANES)[:, None] == lane_head).astype(F32)
    expand3 = ((jnp.arange(LANES)[:, None] % SSD_HEADS == lane_head)
               & (jnp.arange(LANES)[:, None] < 3 * SSD_HEADS)).astype(BF16)
    w_xa_q, w_xa_o = bf(xa_wq), bf(xa_wo)
    w_xa_kv = bf(jnp.concatenate([xa_wk, xa_wv], axis=2))
    w_gate, w_up, w_down = bf(ffn_w_gate), bf(ffn_w_up), bf(ffn_w_down)

    def mem_view(c):
        mt = c.shape[2]
        c = c.reshape(depth, nb, mt, XA_HEADS, 2, XA_HEAD_DIM // 2)
        return jnp.transpose(c, (0, 1, 2, 4, 3, 5)).reshape(depth, nb, mt * 8, XA_HEAD_DIM // 2)

    mem_k, mem_v = mem_view(cache_mem_k), mem_view(cache_mem_v)
    n_pool = cache_diff_k.shape[1]
    pool_kpe_t = jnp.swapaxes(cache_mla_kpe, 2, 3)
    pool_kt = jnp.transpose(cache_diff_k, (0, 1, 3, 4, 5, 2)).reshape(-1, n_pool, 512, PAGE_SIZE)
    pool_v = cache_diff_v.reshape(-1, n_pool, PAGE_SIZE * DIFF_GROUPS, 2 * DIFF_HEAD_DIM)
    mem_flat = mem_prompt.reshape(-1, d)

    def run(x, pos_tab, prompt):
        m = x.shape[0]
        cos, sin = _rope_tables(pos_tab)
        new = {k: [] for k in ("mla_ckv", "mla_kpe", "diff_k", "diff_v", "ssm", "conv", "mem_k", "mem_v")}
        for i in range(depth):
            kind, j = i % n_mix, i // n_mix
            g_mix = norm_mix[i]
            if kind == 0:
                qkv_a = _linear(x, w_qkv_a[j], gain=g_mix, name="mla_qkv_a")
                q = _linear(qkv_a, wq_b[j], gain=mla_q_norm[j], name="mla_q_b")
                if prompt:
                    q_heads, k_heads, ckv, kpe, v_lat = _mla_prep(q, qkv_a, cos, sin, w_uk_t[j], mla_kv_norm[j], False)
                    o_lat = _flash_mla(q_heads, k_heads, v_lat, batch, seq)
                else:
                    qcat, kcat, ckv, kpe = _mla_prep(q, qkv_a, cos, sin, w_uk[j], mla_kv_norm[j], True)
                    o_lat = _decode_mla(page_table, jnp.transpose(qcat, (1, 0, 2)), kcat,
                                        cache_mla_ckv, pool_kpe_t, j).reshape(m, -1)
                x = _mla_out(o_lat, w_uv[j], w_mla_o[j], x)
                new["mla_ckv"].append(ckv)
                new["mla_kpe"].append(kpe)
            elif kind == 1:
                lam_init = 0.8 - 0.6 * math.exp(-0.3 * i)
                lams = (diff_lambda_q1[j], diff_lambda_k1[j], diff_lambda_q2[j], diff_lambda_k2[j])
                qkv = _linear(x, w_diff_qkv[j], gain=g_mix, name="diff_qkv")
                q_b, k32, k_b, v32, v_b = _diff_prep(qkv, cos, sin, batch if prompt else None)
                if prompt:
                    k32 = jnp.transpose(k32.reshape(batch, DIFF_GROUPS, 2, DIFF_HEAD_DIM, seq), (0, 4, 1, 2, 3))
                if prompt:
                    o = _flash_diff(q_b, k_b, v_b, lams, diff_subln[j], lam_init, batch, seq)
                else:
                    q5 = q_b.reshape(m, DIFF_GROUPS, 2, 2, DIFF_HEAD_DIM)
                    eye_g = jnp.eye(DIFF_GROUPS, dtype=BF16)
                    eye_c = jnp.eye(2, dtype=BF16)
                    q_rows = jnp.einsum("tgrcd,gG,cC->tcrgGCd", q5, eye_g, eye_c).reshape(m, 16, 512)
                    v_rows = jnp.tile(v_b.astype(F32).reshape(m, DIFF_GROUPS, LANES), (1, 4, 1))
                    o = _decode_diff(page_table, q_rows, k_b, v_rows, pool_kt, pool_v, lams, diff_subln[j],
                                     lam_init, j)
                    o = jnp.transpose(o.reshape(m, 2, DIFF_GROUPS, LANES), (0, 2, 1, 3)).reshape(m, -1)
                x = _linear(o, w_diff_o[j], res=x, name="diff_o")
                new["diff_k"].append(k32)
                new["diff_v"].append(v32)
            else:
                z = _linear(x, w_ssd_z[j], gain=g_mix, name="ssd_in_z")
                xbc = _linear(x, w_ssd_xbc[j], gain=g_mix, name="ssd_in_xbc")
                dt = _linear(x, w_ssd_dt[j], gain=g_mix, name="ssd_in_dt")
                cw, cb = ssd_conv_w[j], ssd_conv_b[j].reshape(1, -1)
                dtb, a_neg = ssd_dtb[j].reshape(1, -1), ssd_a[j].reshape(1, -1)
                d_exp, nw = ssd_d_exp[j].reshape(1, -1), ssd_norm[j].reshape(1, -1)
                if prompt:
                    yn, st, buf = _ssd_prompt(z, xbc, dt, cw, cb, dtb, a_neg, expand3, d_exp, nw, batch, seq)
                else:
                    xs, xdt, dec, bm, cm, nbuf = _ssd_decode_pre(
                        xbc, dt, jnp.transpose(state_conv[j], (1, 0, 2)), cw, cb, dtb, a_neg, expand)
                    y, st = _ssd_decode_state(xdt, dec, bm, cm, state_ssm[j].reshape(m, SSD_D_INNER, SSD_STATE))
                    yn = _ssd_decode_post(y.reshape(m, -1), xs, z, d_exp, nw)
                    buf = jnp.transpose(nbuf, (1, 0, 2))
                x = _linear(yn, w_ssd_out[j], res=x, name="ssd_out")
                new["ssm"].append(st.reshape(-1, SSD_HEADS, SSD_HEAD_DIM, SSD_STATE))
                new["conv"].append(buf)
            if prompt:
                kv_mem = _linear(mem_flat, w_xa_kv[i], gain=xa_mem_norm[i], name="xa_mem_kv")
                new["mem_k"].append(kv_mem[:, :d])
                new["mem_v"].append(kv_mem[:, d:])
                x = _xa_prompt(x, norm_xa[i], w_xa_q[i], kv_mem.reshape(batch, -1, 2 * d), w_xa_o[i], batch, seq)
            else:
                q = _linear(x, w_xa_q[i], gain=norm_xa[i], name="xa_q")
                o = _xa_decode(q, mem_k, mem_v, i)
                x = _linear(o, w_xa_o[i], res=x, name="xa_o")
            x = _ffn(x, norm_ffn[i], w_gate[i], w_up[i], w_down[i])
        stack = lambda v: v[0][None] if len(v) == 1 else jnp.stack(v)
        return _final_norm(x, norm_final), {k: stack(v) for k, v in new.items() if v}

    pos_p = jnp.arange(seq, dtype=jnp.int32)
    pos_s = jnp.full((nb,), PAST_LEN, jnp.int32)
    y_p, new_p = run(x_prompt.reshape(batch * seq, d), pos_p, True)
    y_s, new_s = run(x_sample.reshape(nb, d), pos_s, False)

    n_b = new_p["diff_k"].shape[0]
    n_c = new_p["ssm"].shape[0]
    mem_t = mem_prompt.shape[1]
    return (
        y_p.reshape(batch, seq, d),
        y_s.reshape(nb, 1, d),
        new_p["mla_ckv"].reshape(n_a, batch, seq, MLA_KV_LORA),
        new_p["mla_kpe"].reshape(n_a, batch, seq, MLA_ROPE),
        new_p["diff_k"].reshape(n_b, batch, seq, DIFF_GROUPS, 2, DIFF_HEAD_DIM),
        new_p["diff_v"].reshape(n_b, batch, seq, DIFF_GROUPS, 2 * DIFF_HEAD_DIM),
        new_p["ssm"].reshape(n_c, batch, SSD_HEADS, SSD_HEAD_DIM, SSD_STATE),
        new_p["conv"],
        new_p["mem_k"].reshape(depth, batch, mem_t, XA_HEADS, XA_HEAD_DIM),
        new_p["mem_v"].reshape(depth, batch, mem_t, XA_HEADS, XA_HEAD_DIM),
        new_s["mla_ckv"].reshape(n_a, nb, 1, MLA_KV_LORA),
        new_s["mla_kpe"].reshape(n_a, nb, 1, MLA_ROPE),
        new_s["diff_k"].reshape(n_b, nb, 1, DIFF_GROUPS, 2, DIFF_HEAD_DIM),
        new_s["diff_v"].reshape(n_b, nb, 1, DIFF_GROUPS, 2 * DIFF_HEAD_DIM),
        new_s["ssm"].reshape(n_c, nb, SSD_HEADS, SSD_HEAD_DIM, SSD_STATE),
        new_s["conv"],
    )
```

```python
import functools
import math

import jax
import jax.numpy as jnp
from jax import lax
from jax.experimental import pallas as pl
from jax.experimental.pallas import tpu as pltpu

F32 = jnp.float32
BF16 = jnp.bfloat16

RMS_EPS = 1e-6
ROPE_THETA = 10000.0
NEG_INF = -1e30
PAST_LEN = 8192
PAGE_SIZE = 128

MLA_HEADS = 8
MLA_NOPE = 128
MLA_ROPE = 64
MLA_KV_LORA = 256
MLA_V = 128
MLA_QK = 384
MLA_SCALE = (MLA_NOPE + MLA_ROPE) ** -0.5

DIFF_HEADS = 8
DIFF_GROUPS = 4
DIFF_HEAD_DIM = 64
DIFF_SCALE = DIFF_HEAD_DIM ** -0.5

SSD_D_INNER = 2048
SSD_HEADS = 32
SSD_HEAD_DIM = 64
SSD_GROUPS = 4
SSD_STATE = 128
SSD_CONV = 4
SSD_CONV_DIM = SSD_D_INNER + 2 * SSD_GROUPS * SSD_STATE
SSD_CHUNK = 128

XA_HEADS = 4
XA_HEAD_DIM = 256
XA_SCALE = XA_HEAD_DIM ** -0.5

LOG2E = math.log2(math.e)
SOFTMAX_ROWS = 32
LANES = 128
VMEM_LIMIT = 48 * 1024 * 1024

NT_DIMS = (((1,), (1,)), ((), ()))
TN_DIMS = (((0,), (0,)), ((), ()))


def _params(sem):
    return pltpu.CompilerParams(dimension_semantics=sem, vmem_limit_bytes=VMEM_LIMIT)


def _rms(x, g):
    return x * lax.rsqrt(jnp.mean(x * x, axis=-1, keepdims=True) + RMS_EPS) * g


def _silu(x):
    return x * jax.nn.sigmoid(x)


def _dot(a, b):
    return jnp.dot(a, b, preferred_element_type=F32)


def _dot_nt(a, b):
    return lax.dot_general(a, b, NT_DIMS, preferred_element_type=F32)


def _dot_exact(a, b, dims=(((1,), (0,)), ((), ()))):
    return lax.dot_general(a, b, dims, precision=lax.Precision.HIGHEST, preferred_element_type=F32)


def _linear_kernel(*refs, has_norm, has_res):
    refs = list(refs)
    x_ref = refs.pop(0)
    g_ref = refs.pop(0) if has_norm else None
    w_ref = refs.pop(0)
    r_ref = refs.pop(0) if has_res else None
    x = x_ref[...]
    if has_norm:
        x = _rms(x.astype(F32), g_ref[...])
    h = x.astype(BF16)
    start = 0
    for o_ref in refs:
        width = o_ref.shape[1]
        tn = _pick_tn(width)
        for j in range(width // tn):
            cols = slice(start + j * tn, start + (j + 1) * tn)
            acc = _dot(h, w_ref[:, cols])
            if has_res:
                acc = acc + r_ref[:, cols]
            o_ref[:, j * tn:(j + 1) * tn] = acc.astype(o_ref.dtype)
        start += width


def _pick_tn(n):
    for tn in (512, 256, 128):
        if n % tn == 0:
            return tn
    return n


def _linear(x, w, gain=None, res=None, out_dtype=F32, x_col=0, splits=None, name="linear"):
    m = x.shape[0]
    k, n = w.shape
    widths = (n,) if splits is None else tuple(splits)
    tm = min(m, 1024 if n <= 2048 else 512)
    in_specs = [pl.BlockSpec((tm, k), lambda i: (i, x_col))]
    args = [x]
    if gain is not None:
        in_specs.append(pl.BlockSpec((1, k), lambda i: (0, 0)))
        args.append(gain.reshape(1, k).astype(F32))
    in_specs.append(pl.BlockSpec((k, n), lambda i: (0, 0), pipeline_mode=pl.Buffered(1)))
    args.append(w)
    if res is not None:
        in_specs.append(pl.BlockSpec((tm, n), lambda i: (i, 0)))
        args.append(res)
    outs = pl.pallas_call(
        functools.partial(_linear_kernel, has_norm=gain is not None, has_res=res is not None),
        out_shape=tuple(jax.ShapeDtypeStruct((m, wd), out_dtype) for wd in widths),
        grid=(m // tm,),
        in_specs=in_specs,
        out_specs=tuple(pl.BlockSpec((tm, wd), lambda i: (i, 0)) for wd in widths),
        compiler_params=_params(("parallel",)),
        name=name,
    )(*args)
    return outs[0] if splits is None else outs


def _ffn_kernel(x_ref, g_ref, wg_ref, wu_ref, wd_ref, *rest, chunk):
    o_ref = rest[-1]
    x = x_ref[...]
    h = _rms(x, g_ref[...]).astype(BF16)
    acc = x
    for c in range(wg_ref.shape[1] // chunk):
        cols = slice(c * chunk, (c + 1) * chunk)
        a = _silu(_dot(h, wg_ref[:, cols])) * _dot(h, wu_ref[:, cols])
        acc = acc + _dot(a.astype(BF16), wd_ref[cols, :])
    if len(rest) == 2:
        acc = _rms(acc, rest[0][...])
    o_ref[...] = acc


def _ffn(x, gain, wg, wu, wd, final_gain=None):
    m, d = x.shape
    hid = wg.shape[1]
    tm = min(m, 1024)
    resident = lambda shape: pl.BlockSpec(shape, lambda i: (0, 0), pipeline_mode=pl.Buffered(1))
    vec = pl.BlockSpec((1, d), lambda i: (0, 0))
    extra = [] if final_gain is None else [final_gain.reshape(1, d)]
    return pl.pallas_call(
        functools.partial(_ffn_kernel, chunk=256),
        out_shape=jax.ShapeDtypeStruct((m, d), F32),
        grid=(m // tm,),
        in_specs=[pl.BlockSpec((tm, d), lambda i: (i, 0)), vec, resident((d, hid)), resident((d, hid)),
                  resident((hid, d))] + [vec] * len(extra),
        out_specs=pl.BlockSpec((tm, d), lambda i: (i, 0)),
        compiler_params=_params(("parallel",)),
        name="ffn",
    )(x, gain.reshape(1, d), wg, wu, wd, *extra)


def _rope(x, cos, sin):
    n = x.shape[1]
    reps = n // LANES
    if reps > 1:
        cos = jnp.concatenate([cos] * reps, axis=1)
        sin = jnp.concatenate([sin] * reps, axis=1)
    lane = lax.broadcasted_iota(jnp.int32, x.shape, 1)
    first_half = (lane % 64) < 32
    partner = jnp.where(first_half, pltpu.roll(x, n - 32, 1), pltpu.roll(x, 32, 1))
    return x * cos + partner * sin


def _rope_tables(pos):
    inv = ROPE_THETA ** (-jnp.arange(32, dtype=F32) * 2.0 / 64)
    ang = pos.astype(F32)[:, None] * inv[None, :]
    c, s = jnp.cos(ang), jnp.sin(ang)
    return jnp.concatenate([c, c, c, c], axis=1), jnp.concatenate([-s, s, -s, s], axis=1)


def _mla_prep_kernel(q_ref, kv_ref, cos_ref, sin_ref, wuk_ref, kvg_ref,
                     qh_ref, kh_ref, ckv_ref, kpe_ref, *v_ref, absorb):
    cos, sin = cos_ref[...], sin_ref[...]
    q = q_ref[...] * (MLA_SCALE * LOG2E)
    q_pe = _rope(q[:, 1024:], cos, sin).astype(BF16)
    q_nope = q[:, :1024].astype(BF16)
    kv = kv_ref[...]
    ckv = _rms(kv[:, :256], kvg_ref[...])
    kpe = _rope(kv[:, 256:], cos, sin)
    ckv_ref[...] = ckv
    kpe_ref[...] = kpe[:, :64]
    ckv_b, kpe_b = ckv.astype(BF16), kpe.astype(BF16)
    if absorb:
        kh_ref[:, 0:256] = ckv_b
        kh_ref[:, 256:384] = kpe_b
    else:
        v_ref[0][...] = ckv_b
    for h in range(MLA_HEADS):
        head = slice(h * 128, (h + 1) * 128)
        if absorb:
            qh_ref[h, :, 0:256] = _dot(q_nope[:, head], wuk_ref[h]).astype(BF16)
            qh_ref[h, :, 256:384] = q_pe[:, head]
        else:
            qh_ref[h, :, 0:128] = q_nope[:, head]
            qh_ref[h, :, 128:256] = q_pe[:, head]
            kh_ref[h, :, 0:128] = _dot(ckv_b, wuk_ref[h]).astype(BF16)
            kh_ref[h, :, 128:256] = kpe_b


def _mla_prep(q, qkv_a, cos, sin, wuk, kv_gain, absorb):
    m = q.shape[0]
    tm = min(m, 512)
    nt = cos.shape[0] // tm
    row = lambda i: (i, 0)
    per_head = lambda i: (0, i, 0)
    if absorb:
        head_shapes = [jax.ShapeDtypeStruct((MLA_HEADS, m, MLA_QK), BF16), jax.ShapeDtypeStruct((m, MLA_QK), BF16)]
        head_specs = [pl.BlockSpec((MLA_HEADS, tm, MLA_QK), per_head), pl.BlockSpec((tm, MLA_QK), row)]
        extra_shapes, extra_specs = [], []
    else:
        head_shapes = [jax.ShapeDtypeStruct((MLA_HEADS, m, 256), BF16)] * 2
        head_specs = [pl.BlockSpec((MLA_HEADS, tm, 256), per_head)] * 2
        extra_shapes = [jax.ShapeDtypeStruct((m, MLA_KV_LORA), BF16)]
        extra_specs = [pl.BlockSpec((tm, MLA_KV_LORA), row)]
    return pl.pallas_call(
        functools.partial(_mla_prep_kernel, absorb=absorb),
        out_shape=tuple(head_shapes + [jax.ShapeDtypeStruct((m, MLA_KV_LORA), F32),
                                       jax.ShapeDtypeStruct((m, MLA_ROPE), F32)] + extra_shapes),
        grid=(m // tm,),
        in_specs=[
            pl.BlockSpec((tm, 2048), row),
            pl.BlockSpec((tm, MLA_QK), lambda i: (i, 1)),
            pl.BlockSpec((tm, LANES), lambda i: (i % nt, 0)),
            pl.BlockSpec((tm, LANES), lambda i: (i % nt, 0)),
            pl.BlockSpec((MLA_HEADS,) + wuk.shape[1:], lambda i: (0, 0, 0)),
            pl.BlockSpec((1, 256), lambda i: (0, 0)),
        ],
        out_specs=tuple(head_specs + [pl.BlockSpec((tm, MLA_KV_LORA), row), pl.BlockSpec((tm, MLA_ROPE), row)]
                        + extra_specs),
        compiler_params=_params(("parallel",)),
        name="mla_prep",
    )(q, qkv_a, cos, sin, wuk, kv_gain.reshape(1, 256))


def _tree(op, xs):
    xs = list(xs)
    while len(xs) > 1:
        xs = [op(xs[i], xs[i + 1]) if i + 1 < len(xs) else xs[i] for i in range(0, len(xs), 2)]
    return xs[0]


def _online_softmax(s, row0, m_scr, l_scr, acc_scr, masked):
    rows_n, tk = s.shape
    ch = SOFTMAX_ROWS
    acc_tiles = acc_scr.shape[1] // LANES
    out = []
    for c in range(rows_n // ch):
        rows = pl.ds(row0 + c * ch, ch)
        first_row, last_row = c * ch, (c + 1) * ch - 1
        tiles = []
        for t in range(tk // LANES):
            lo = t * LANES
            if masked and lo > last_row:
                continue
            tile = s[first_row:last_row + 1, lo:lo + LANES]
            if masked and lo + LANES - 1 > first_row:
                qpos = first_row + lax.broadcasted_iota(jnp.int32, (ch, LANES), 0)
                kpos = lo + lax.broadcasted_iota(jnp.int32, (ch, LANES), 1)
                tile = jnp.where(kpos <= qpos, tile, NEG_INF)
            tiles.append(tile)
        m_prev = m_scr[rows, :]
        m_new = jnp.maximum(m_prev, jnp.max(_tree(jnp.maximum, tiles), axis=1, keepdims=True))
        alpha = jnp.exp2(m_prev - m_new)
        ps = [jnp.exp2(t - m_new) for t in tiles]
        if l_scr is not None:
            l_scr[rows, :] = alpha * l_scr[rows, :] + jnp.sum(_tree(jnp.add, ps), axis=1, keepdims=True)
        m_scr[rows, :] = m_new
        acc_scr[rows, :] = acc_scr[rows, :] * jnp.concatenate([alpha] * acc_tiles, axis=1)
        ps += [jnp.zeros((ch, LANES), F32)] * (tk // LANES - len(ps))
        out.append(jnp.concatenate(ps, axis=1).astype(BF16))
    return jnp.concatenate(out, axis=0)


def _flash_mla_kernel(q_ref, k_ref, v_ref, o_ref, m_scr, l_scr, acc_scr, *, tile):
    qi, ki = pl.program_id(1), pl.program_id(2)

    @pl.when(ki == 0)
    def _():
        m_scr[...] = jnp.full_like(m_scr, NEG_INF)
        l_scr[...] = jnp.zeros_like(l_scr)
        acc_scr[...] = jnp.zeros_like(acc_scr)

    def all_heads(masked):
        def head(h, carry):
            row0 = pl.multiple_of(h * tile, tile)
            p = _online_softmax(_dot_nt(q_ref[h], k_ref[h]), row0, m_scr, l_scr, acc_scr, masked)
            acc_scr[pl.ds(row0, tile), :] += _dot(p, v_ref[...])
            return carry
        lax.fori_loop(0, MLA_HEADS, head, 0, unroll=4)

    @pl.when(ki < qi)
    def _():
        all_heads(False)

    @pl.when(ki == qi)
    def _():
        all_heads(True)
        for h in range(MLA_HEADS):
            rows = slice(h * tile, (h + 1) * tile)
            l = l_scr[rows, :]
            o = acc_scr[rows, :] / jnp.concatenate([l, l], axis=1)
            o_ref[:, h * 256:(h + 1) * 256] = o.astype(o_ref.dtype)


def _flash_mla(q_heads, k_heads, v, batch, seq, tile=512):
    nt = seq // tile
    return pl.pallas_call(
        functools.partial(_flash_mla_kernel, tile=tile),
        out_shape=jax.ShapeDtypeStruct((batch * seq, MLA_HEADS * MLA_KV_LORA), BF16),
        grid=(batch, nt, nt),
        in_specs=[
            pl.BlockSpec((MLA_HEADS, tile, 256), lambda b, qi, ki: (0, b * nt + qi, 0)),
            pl.BlockSpec((MLA_HEADS, tile, 256), lambda b, qi, ki: (0, b * nt + jnp.minimum(ki, qi), 0)),
            pl.BlockSpec((tile, MLA_KV_LORA), lambda b, qi, ki: (b * nt + jnp.minimum(ki, qi), 0)),
        ],
        out_specs=pl.BlockSpec((tile, MLA_HEADS * MLA_KV_LORA), lambda b, qi, ki: (b * nt + qi, 0)),
        scratch_shapes=[
            pltpu.VMEM((MLA_HEADS * tile, LANES), F32),
            pltpu.VMEM((MLA_HEADS * tile, LANES), F32),
            pltpu.VMEM((MLA_HEADS * tile, MLA_KV_LORA), F32),
        ],
        compiler_params=_params(("parallel", "parallel", "arbitrary")),
        name="flash_mla",
    )(q_heads, k_heads, v)


def _mla_out_kernel(o_ref, wuv_ref, wo_ref, x_ref, y_ref):
    o = o_ref[...].astype(BF16)
    heads = [_dot(o[:, h * 256:(h + 1) * 256], wuv_ref[h]).astype(BF16) for h in range(MLA_HEADS)]
    y_ref[...] = x_ref[...] + _dot(jnp.concatenate(heads, axis=1), wo_ref[...])


def _mla_out(o_lat, wuv, wo, x):
    m, d = x.shape
    tm = min(m, 512)
    return pl.pallas_call(
        _mla_out_kernel,
        out_shape=jax.ShapeDtypeStruct((m, d), F32),
        grid=(m // tm,),
        in_specs=[
            pl.BlockSpec((tm, 2048), lambda i: (i, 0)),
            pl.BlockSpec((MLA_HEADS, 256, 128), lambda i: (0, 0, 0)),
            pl.BlockSpec((1024, d), lambda i: (0, 0)),
            pl.BlockSpec((tm, d), lambda i: (i, 0)),
        ],
        out_specs=pl.BlockSpec((tm, d), lambda i: (i, 0)),
        compiler_params=_params(("parallel",)),
        name="mla_out",
    )(o_lat, wuv, wo, x)


def _decode_mla_kernel(pt_ref, q_ref, kn_ref, ckv_hbm, kpe_hbm, o_ref,
                       cbuf, pbuf, kb_scr, sem, *, layer, n_pages):
    b = pl.program_id(0)
    slot = b % 2

    def copies(tok, sl, p):
        page = pt_ref[tok * n_pages + p]
        return (pltpu.make_async_copy(ckv_hbm.at[layer, page], cbuf.at[sl, p], sem.at[0, sl]),
                pltpu.make_async_copy(kpe_hbm.at[layer, page], pbuf.at[sl, p], sem.at[1, sl]))

    def issue(tok, sl):
        def body(p, carry):
            for cp in copies(tok, sl, p):
                cp.start()
            return carry
        lax.fori_loop(0, n_pages, body, 0)

    def wait_all(tok, sl):
        def body(p, carry):
            for cp in copies(tok, sl, p):
                cp.wait()
            return carry
        lax.fori_loop(0, n_pages, body, 0)

    last = pl.num_programs(0) - 1

    @pl.when(b == 0)
    def _():
        issue(0, 0)

    wait_all(b, slot)

    nxt = jnp.minimum(b + 1, last)
    for p in range(n_pages):
        for cp in copies(nxt, 1 - slot, p):
            cp.start()

    n_keys = n_pages * PAGE_SIZE
    kb_scr[...] = cbuf[slot].reshape(n_keys, MLA_KV_LORA).astype(BF16)

    q = q_ref[0]
    kn = kn_ref[0].astype(F32)
    s_self = jnp.sum(q.astype(F32) * kn, axis=-1, keepdims=True)
    q_pe = q[:, 256:320]
    kpe_t = jnp.concatenate([pbuf[slot, p].astype(BF16) for p in range(n_pages)], axis=1)
    s = _dot_nt(q[:, :256], kb_scr[...]) + _dot(q_pe, kpe_t)
    m = jnp.maximum(jnp.max(s, axis=-1, keepdims=True), s_self)
    p = jnp.exp2(s - m)
    p_self = jnp.exp2(s_self - m)
    l = jnp.sum(p, axis=-1, keepdims=True) + p_self
    acc = _dot(p.astype(BF16), kb_scr[...]) + p_self * kn[:, :256]
    o_ref[0] = acc / l

    @pl.when(b == last)
    def _():
        wait_all(last, 1 - slot)


def _decode_mla(page_table, q_tok, k_new, cache_ckv, cache_kpe, layer):
    nb, n_pages = page_table.shape
    n_keys = n_pages * PAGE_SIZE
    return pl.pallas_call(
        functools.partial(_decode_mla_kernel, layer=layer, n_pages=n_pages),
        out_shape=jax.ShapeDtypeStruct((nb, MLA_HEADS, MLA_KV_LORA), F32),
        grid_spec=pltpu.PrefetchScalarGridSpec(
            num_scalar_prefetch=1,
            grid=(nb,),
            in_specs=[
                pl.BlockSpec((1, MLA_HEADS, MLA_QK), lambda b, pt: (b, 0, 0)),
                pl.BlockSpec((1, 1, MLA_QK), lambda b, pt: (b, 0, 0)),
                pl.BlockSpec(memory_space=pl.ANY),
                pl.BlockSpec(memory_space=pl.ANY),
            ],
            out_specs=pl.BlockSpec((1, MLA_HEADS, MLA_KV_LORA), lambda b, pt: (b, 0, 0)),
            scratch_shapes=[
                pltpu.VMEM((2, n_pages, PAGE_SIZE, MLA_KV_LORA), F32),
                pltpu.VMEM((2, n_pages, MLA_ROPE, PAGE_SIZE), F32),
                pltpu.VMEM((n_keys, MLA_KV_LORA), BF16),
                pltpu.SemaphoreType.DMA((2, 2)),
            ],
        ),
        compiler_params=_params(("arbitrary",)),
        name="decode_mla",
    )(page_table.reshape(-1), q_tok, k_new.reshape(nb, 1, MLA_QK), cache_ckv, cache_kpe)


def _diff_prep_kernel(qkv_ref, cos_ref, sin_ref, q_ref, k32_ref, kb_ref, v32_ref, vb_ref, *, k_transposed):
    qkv = qkv_ref[...]
    qk = _rope(qkv[:, :1536], cos_ref[...], sin_ref[...])
    q_ref[...] = (qk[:, :1024] * (DIFF_SCALE * LOG2E)).astype(BF16)
    k = qk[:, 1024:]
    v = qkv[:, 1536:]
    if k_transposed:
        k32_ref[0] = k.T
    else:
        k32_ref[...] = k
    kb_ref[...] = k.astype(BF16)
    v32_ref[...] = v
    vb_ref[...] = v.astype(BF16)


def _diff_prep(qkv, cos, sin, batch=None):
    m = qkv.shape[0]
    tm = min(m, 512)
    nt = cos.shape[0] // tm
    row = lambda i: (i, 0)
    if batch is None:
        k32_shape, k32_spec = jax.ShapeDtypeStruct((m, 512), F32), pl.BlockSpec((tm, 512), row)
    else:
        k32_shape = jax.ShapeDtypeStruct((batch, 512, m // batch), F32)
        k32_spec = pl.BlockSpec((1, 512, tm), lambda i: (i // nt, 0, i % nt))
    return pl.pallas_call(
        functools.partial(_diff_prep_kernel, k_transposed=batch is not None),
        out_shape=(
            jax.ShapeDtypeStruct((m, 1024), BF16),
            k32_shape,
            jax.ShapeDtypeStruct((m, 512), BF16),
            jax.ShapeDtypeStruct((m, 512), F32),
            jax.ShapeDtypeStruct((m, 512), BF16),
        ),
        grid=(m // tm,),
        in_specs=[
            pl.BlockSpec((tm, 2048), row),
            pl.BlockSpec((tm, LANES), lambda i: (i % nt, 0)),
            pl.BlockSpec((tm, LANES), lambda i: (i % nt, 0)),
        ],
        out_specs=(
            pl.BlockSpec((tm, 1024), row),
            k32_spec,
            pl.BlockSpec((tm, 512), row),
            pl.BlockSpec((tm, 512), row),
            pl.BlockSpec((tm, 512), row),
        ),
        compiler_params=_params(("parallel",)),
        name="diff_prep",
    )(qkv, cos, sin)


def _diff_lambda(lq1_ref, lk1_ref, lq2_ref, lk2_ref, lam_init):
    e1 = jnp.exp(jnp.sum(lq1_ref[...] * lk1_ref[...], axis=-1, keepdims=True))
    e2 = jnp.exp(jnp.sum(lq2_ref[...] * lk2_ref[...], axis=-1, keepdims=True))
    return e1 - e2 + lam_init


def _flash_diff_kernel(lq1_ref, lk1_ref, lq2_ref, lk2_ref, sub_ref, q_ref, k_ref, v_ref, o_ref,
                       m_scr, acc_scr, *, tile, lam_init):
    qi, ki = pl.program_id(2), pl.program_id(3)

    @pl.when(ki == 0)
    def _():
        m_scr[...] = jnp.full_like(m_scr, NEG_INF)
        acc_scr[...] = jnp.zeros_like(acc_scr)

    def all_rows(masked):
        q = q_ref[...]
        k = k_ref[...]
        v_ext = jnp.concatenate([v_ref[...], jnp.ones((tile, LANES), BF16)], axis=1)
        comp0 = lax.broadcasted_iota(jnp.int32, (tile, LANES), 1) < DIFF_HEAD_DIM
        zero = jnp.zeros((tile, LANES), BF16)
        for r in range(2):
            head = q[:, r * LANES:(r + 1) * LANES]
            for c in range(2):
                qc = jnp.where(comp0, head, zero) if c == 0 else jnp.where(comp0, zero, head)
                row0 = (2 * r + c) * tile
                p = _online_softmax(_dot_nt(qc, k), row0, m_scr, None, acc_scr, masked)
                acc_scr[row0:row0 + tile, :] += _dot(p, v_ext)

    @pl.when(ki < qi)
    def _():
        all_rows(False)

    @pl.when(ki == qi)
    def _():
        all_rows(True)
        lam = _diff_lambda(lq1_ref, lk1_ref, lq2_ref, lk2_ref, lam_init)
        g = sub_ref[...]
        for r in range(2):
            o = []
            for c in range(2):
                rows = slice((2 * r + c) * tile, (2 * r + c + 1) * tile)
                o.append(acc_scr[rows, :LANES] / acc_scr[rows, LANES:])
            d = _rms(o[0] - lam * o[1], g) * (1.0 - lam_init)
            o_ref[:, r * LANES:(r + 1) * LANES] = d.astype(o_ref.dtype)


def _flash_diff(q, k, v, lams, subln, lam_init, batch, seq, tile=512):
    nt = seq // tile

    def kv_map(b, g, qi, ki):
        return (b * nt + jnp.minimum(ki, qi), g)

    vec = pl.BlockSpec((1, DIFF_HEAD_DIM), lambda b, g, qi, ki: (0, 0))
    return pl.pallas_call(
        functools.partial(_flash_diff_kernel, tile=tile, lam_init=lam_init),
        out_shape=jax.ShapeDtypeStruct((batch * seq, 1024), BF16),
        grid=(batch, DIFF_GROUPS, nt, nt),
        in_specs=[
            vec, vec, vec, vec,
            pl.BlockSpec((1, LANES), lambda b, g, qi, ki: (0, 0)),
            pl.BlockSpec((tile, 256), lambda b, g, qi, ki: (b * nt + qi, g)),
            pl.BlockSpec((tile, LANES), kv_map),
            pl.BlockSpec((tile, LANES), kv_map),
        ],
        out_specs=pl.BlockSpec((tile, 256), lambda b, g, qi, ki: (b * nt + qi, g)),
        scratch_shapes=[
            pltpu.VMEM((4 * tile, LANES), F32),
            pltpu.VMEM((4 * tile, 2 * LANES), F32),
        ],
        compiler_params=_params(("parallel", "parallel", "parallel", "arbitrary")),
        name="flash_diff",
    )(*[a.reshape(1, DIFF_HEAD_DIM) for a in lams], subln.reshape(1, LANES), q, k, v)


def _decode_diff_kernel(pt_ref, lq1_ref, lk1_ref, lq2_ref, lk2_ref, sub_ref, q_ref, kn_ref, vn_ref,
                        k_hbm, v_hbm, o_ref, kbuf, vbuf, sem, m_scr, l_scr, acc_scr,
                        *, layer, n_pages, unit, lam_init):
    b, u = pl.program_id(0), pl.program_id(1)
    n_units = n_pages // unit
    step = b * n_units + u
    slot = step % 2

    def copies(st, sl, p):
        tok, un = st // n_units, st % n_units
        page = pt_ref[tok * n_pages + un * unit + p]
        return (pltpu.make_async_copy(k_hbm.at[layer, page], kbuf.at[sl, p], sem.at[0, sl]),
                pltpu.make_async_copy(v_hbm.at[layer, page], vbuf.at[sl, p], sem.at[1, sl]))

    def issue(st, sl):
        def body(p, carry):
            for cp in copies(st, sl, p):
                cp.start()
            return carry
        lax.fori_loop(0, unit, body, 0)

    @pl.when(step == 0)
    def _():
        issue(0, 0)

    @pl.when(step + 1 < pl.num_programs(0) * n_units)
    def _():
        issue(step + 1, 1 - slot)

    def wait_body(p, carry):
        for cp in copies(step, slot, p):
            cp.wait()
        return carry
    lax.fori_loop(0, unit, wait_body, 0)

    q = q_ref[0]

    @pl.when(u == 0)
    def _():
        m_scr[...] = jnp.sum(q.astype(F32) * kn_ref[0].astype(F32), axis=-1, keepdims=True)
        l_scr[...] = jnp.ones_like(l_scr)
        acc_scr[...] = vn_ref[0]

    kt = jnp.concatenate([kbuf[slot, pg].astype(BF16) for pg in range(unit)], axis=1)
    s = _dot(q, kt)
    m_old = m_scr[...]
    m_new = jnp.maximum(m_old, jnp.max(s, axis=-1, keepdims=True))
    alpha = jnp.exp2(m_old - m_new)
    p = jnp.exp2(s - m_new)
    l_scr[...] = alpha * l_scr[...] + jnp.sum(p, axis=-1, keepdims=True)
    m_scr[...] = m_new
    pb = p.astype(BF16)
    halves = []
    for gp in range(DIFF_GROUPS // 2):
        out = jnp.zeros((16, 2 * LANES), F32)
        for pp in range(unit // 2):
            stacked = []
            for pg in (2 * pp, 2 * pp + 1):
                side = [vbuf[slot, pg, pl.ds(2 * gp + i, PAGE_SIZE, stride=DIFF_GROUPS), :] for i in range(2)]
                stacked.append(jnp.concatenate(side, axis=1).astype(BF16))
            out = out + _dot(pb[:, 2 * pp * PAGE_SIZE:(2 * pp + 2) * PAGE_SIZE], jnp.concatenate(stacked, axis=0))
        halves += [out[:, :LANES], out[:, LANES:]]
    row_g = lax.broadcasted_iota(jnp.int32, (16, LANES), 0) % DIFF_GROUPS
    pv = jnp.where(row_g == 0, halves[0], jnp.where(row_g == 1, halves[1],
                                                    jnp.where(row_g == 2, halves[2], halves[3])))
    acc_scr[...] = alpha * acc_scr[...] + pv

    @pl.when(u == n_units - 1)
    def _():
        lam = _diff_lambda(lq1_ref, lk1_ref, lq2_ref, lk2_ref, lam_init)
        on = acc_scr[...] / l_scr[...]
        d = on[0:8] - lam * on[8:16]
        o_ref[0] = _rms(d, sub_ref[...]) * (1.0 - lam_init)


def _decode_diff(page_table, q_rows, k_new, v_rows, cache_kt, cache_v, lams, subln, lam_init, layer, unit=16):
    nb, n_pages = page_table.shape
    vec = pl.BlockSpec((1, DIFF_HEAD_DIM), lambda b, u, pt: (0, 0))
    return pl.pallas_call(
        functools.partial(_decode_diff_kernel, layer=layer, n_pages=n_pages, unit=unit, lam_init=lam_init),
        out_shape=jax.ShapeDtypeStruct((nb, 8, LANES), F32),
        grid_spec=pltpu.PrefetchScalarGridSpec(
            num_scalar_prefetch=1,
            grid=(nb, n_pages // unit),
            in_specs=[
                vec, vec, vec, vec,
                pl.BlockSpec((1, LANES), lambda b, u, pt: (0, 0)),
                pl.BlockSpec((1, 16, 512), lambda b, u, pt: (b, 0, 0)),
                pl.BlockSpec((1, 1, 512), lambda b, u, pt: (b, 0, 0)),
                pl.BlockSpec((1, 16, LANES), lambda b, u, pt: (b, 0, 0)),
                pl.BlockSpec(memory_space=pl.ANY),
                pl.BlockSpec(memory_space=pl.ANY),
            ],
            out_specs=pl.BlockSpec((1, 8, LANES), lambda b, u, pt: (b, 0, 0)),
            scratch_shapes=[
                pltpu.VMEM((2, unit, 512, PAGE_SIZE), F32),
                pltpu.VMEM((2, unit, 512, LANES), F32),
                pltpu.SemaphoreType.DMA((2, 2)),
                pltpu.VMEM((16, 1), F32),
                pltpu.VMEM((16, 1), F32),
                pltpu.VMEM((16, LANES), F32),
            ],
        ),
        compiler_params=_params(("arbitrary", "arbitrary")),
        name="decode_diff",
    )(page_table.reshape(-1), *[a.reshape(1, DIFF_HEAD_DIM) for a in lams], subln.reshape(1, LANES),
      q_rows, k_new.reshape(nb, 1, 512), v_rows, cache_kt, cache_v)


def _ssd_gate_norm(y, xs, z, d_ref, nw_ref):
    yg = (y + d_ref[...] * xs) * _silu(z)
    w = SSD_D_INNER // SSD_GROUPS
    parts = []
    for g in range(SSD_GROUPS):
        part = yg[:, g * w:(g + 1) * w]
        parts.append(part * lax.rsqrt(jnp.mean(part * part, axis=-1, keepdims=True) + RMS_EPS))
    return jnp.concatenate(parts, axis=1) * nw_ref[...]


def _ssd_prompt_kernel(z_ref, xbc_ref, dt_ref, cw_ref, cb_ref, dtb_ref, a_ref, e_ref, d_ref, nw_ref,
                       yn_ref, st_ref, buf_ref, xp_scr, s_scr, *, chunk):
    c = pl.program_id(1)

    @pl.when(c == 0)
    def _():
        xp_scr[0:8, :] = jnp.zeros((8, SSD_CONV_DIM), F32)
        s_scr[...] = jnp.zeros_like(s_scr)

    xp_scr[8:8 + chunk, :] = xbc_ref[...]
    conv = cb_ref[...] + sum(xp_scr[5 + k:5 + k + chunk, :] * cw_ref[k:k + 1, :] for k in range(SSD_CONV))
    u = _silu(conv)
    xs = u[:, :SSD_D_INNER]
    bm = u[:, SSD_D_INNER:SSD_D_INNER + 512]
    cm = u[:, SSD_D_INNER + 512:]

    dt = jax.nn.softplus(dt_ref[...] + dtb_ref[...])
    a = dt * a_ref[...]
    row = lax.broadcasted_iota(jnp.int32, (chunk, chunk), 0)
    col = lax.broadcasted_iota(jnp.int32, (chunk, chunk), 1)
    causal = col <= row
    tri = jnp.where(causal, 1.0, 0.0).astype(BF16)
    a_cs = _dot(jnp.concatenate([tri, tri, tri], axis=1),
                jnp.concatenate(_split_bf16x3(a), axis=0).astype(BF16))
    a_cs_t = a_cs.T
    a_last = a_cs[chunk - 1:chunk, :]
    cols = jnp.concatenate([dt, jnp.exp(a_cs), jnp.exp(a_last - a_cs)], axis=0)
    head_lane = lax.broadcasted_iota(jnp.int32, cols.shape, 1) < SSD_HEADS
    hi, mid, lo = _split_bf16x3(jnp.where(head_lane, cols, 0.0))
    packed = hi + pltpu.roll(mid, SSD_HEADS, 1) + pltpu.roll(lo, 2 * SSD_HEADS, 1)
    ex = _dot(packed.astype(BF16), e_ref[...])
    dtx, ecs, dend = ex[:chunk], ex[chunk:2 * chunk], ex[2 * chunk:]
    xdt = xs * dtx
    xdt_b = xdt.astype(BF16)
    xdtd_b = (xdt * dend).astype(BF16)
    etot = ecs[chunk - 1:chunk, :]
    low_head = lax.broadcasted_iota(jnp.int32, (chunk, LANES), 1) < SSD_HEAD_DIM

    ys = []
    for g in range(SSD_GROUPS):
        bg = bm[:, g * 128:(g + 1) * 128]
        cg = cm[:, g * 128:(g + 1) * 128].astype(BF16)
        cb = _dot_nt(cg, bg.astype(BF16))
        bg_t = bg.T.astype(BF16)
        for pi in range(4):
            pair = g * 4 + pi
            sl = slice(pair * LANES, (pair + 1) * LANES)
            ms = []
            for h in (2 * pair, 2 * pair + 1):
                seg = a_cs[:, h:h + 1] - a_cs_t[h:h + 1, :]
                ms.append((cb * jnp.exp(jnp.where(causal, seg, NEG_INF))).astype(BF16))
            y_diag = jnp.where(low_head, _dot(ms[0], xdt_b[:, sl]), _dot(ms[1], xdt_b[:, sl]))
            st = s_scr[:, sl]
            y_off = _dot(cg, st.astype(BF16)) * ecs[:, sl]
            s_scr[:, sl] = etot[:, sl] * st + _dot(bg_t, xdtd_b[:, sl])
            ys.append(y_diag + y_off)

    y = jnp.concatenate(ys, axis=1)
    yn_ref[...] = _ssd_gate_norm(y, xs, z_ref[...], d_ref, nw_ref).astype(yn_ref.dtype)
    xp_scr[0:8, :] = xp_scr[chunk:chunk + 8, :]

    @pl.when(c == pl.num_programs(1) - 1)
    def _():
        st_ref[0] = s_scr[...].T
        buf_ref[0] = xp_scr[chunk + 5:chunk + 8, :]


def _ssd_prompt(z, xbc, dt, cw, cb, dtb, a_neg, expand, d_exp, nw, batch, seq, chunk=SSD_CHUNK):
    nc = seq // chunk
    row = lambda b, c: (b * nc + c, 0)
    const = lambda b, c: (0, 0)
    return pl.pallas_call(
        functools.partial(_ssd_prompt_kernel, chunk=chunk),
        out_shape=(
            jax.ShapeDtypeStruct((batch * seq, SSD_D_INNER), BF16),
            jax.ShapeDtypeStruct((batch, SSD_D_INNER, SSD_STATE), F32),
            jax.ShapeDtypeStruct((batch, SSD_CONV - 1, SSD_CONV_DIM), F32),
        ),
        grid=(batch, nc),
        in_specs=[
            pl.BlockSpec((chunk, SSD_D_INNER), row),
            pl.BlockSpec((chunk, SSD_CONV_DIM), row),
            pl.BlockSpec((chunk, LANES), row),
            pl.BlockSpec((SSD_CONV, SSD_CONV_DIM), const),
            pl.BlockSpec((1, SSD_CONV_DIM), const),
            pl.BlockSpec((1, LANES), const),
            pl.BlockSpec((1, LANES), const),
            pl.BlockSpec((LANES, SSD_D_INNER), const),
            pl.BlockSpec((1, SSD_D_INNER), const),
            pl.BlockSpec((1, SSD_D_INNER), const),
        ],
        out_specs=(
            pl.BlockSpec((chunk, SSD_D_INNER), row),
            pl.BlockSpec((1, SSD_D_INNER, SSD_STATE), lambda b, c: (b, 0, 0)),
            pl.BlockSpec((1, SSD_CONV - 1, SSD_CONV_DIM), lambda b, c: (b, 0, 0)),
        ),
        scratch_shapes=[
            pltpu.VMEM((chunk + 8, SSD_CONV_DIM), F32),
            pltpu.VMEM((SSD_STATE, SSD_D_INNER), F32),
        ],
        compiler_params=_params(("parallel", "arbitrary")),
        name="ssd_prompt",
    )(z, xbc, dt, cw, cb, dtb, a_neg, expand, d_exp, nw)


def _ssd_decode_pre_kernel(xbc_ref, dt_ref, buf_ref, cw_ref, cb_ref, dtb_ref, a_ref, e_ref,
                           xs_ref, xdt_ref, dec_ref, b_ref, c_ref, nbuf_ref):
    xbc = xbc_ref[...]
    conv = cb_ref[...] + xbc * cw_ref[3:4, :]
    for k in range(SSD_CONV - 1):
        conv = conv + buf_ref[k] * cw_ref[k:k + 1, :]
    u = _silu(conv)
    xs = u[:, :SSD_D_INNER]
    dt = jax.nn.softplus(dt_ref[...] + dtb_ref[...])
    ex = _dot_exact(jnp.concatenate([dt, dt * a_ref[...]], axis=0), e_ref[...])
    t = xbc.shape[0]
    xs_ref[...] = xs
    xdt_ref[...] = xs * ex[:t]
    dec_ref[...] = jnp.exp(ex[t:])
    b_ref[...] = u[:, SSD_D_INNER:SSD_D_INNER + 512]
    c_ref[...] = u[:, SSD_D_INNER + 512:]
    nbuf_ref[0] = buf_ref[1]
    nbuf_ref[1] = buf_ref[2]
    nbuf_ref[2] = xbc


def _ssd_decode_pre(xbc, dt, buf, cw, cb, dtb, a_neg, expand):
    t = xbc.shape[0]
    wide = jax.ShapeDtypeStruct((t, SSD_D_INNER), F32)
    grp = jax.ShapeDtypeStruct((t, 512), F32)
    return pl.pallas_call(
        _ssd_decode_pre_kernel,
        out_shape=(wide, wide, wide, grp, grp, jax.ShapeDtypeStruct((SSD_CONV - 1, t, SSD_CONV_DIM), F32)),
        compiler_params=pltpu.CompilerParams(vmem_limit_bytes=VMEM_LIMIT),
        name="ssd_decode_pre",
    )(xbc, dt, buf, cw, cb, dtb, a_neg, expand)


def _split_bf16x3(x):
    hi = x.astype(BF16).astype(F32)
    mid = (x - hi).astype(BF16).astype(F32)
    lo = (x - hi - mid).astype(BF16).astype(F32)
    return hi, mid, lo


def _ssd_decode_state_kernel(xdt_ref, dec_ref, b_ref, c_ref, st_ref, y_ref, nst_ref):
    xdt = xdt_ref[0]
    lane_g = lax.broadcasted_iota(jnp.int32, (8, SSD_D_INNER), 1) // 512
    row = lax.broadcasted_iota(jnp.int32, (8, SSD_D_INNER), 0)
    lhs = jnp.where(row == lane_g, xdt, 0.0) + jnp.where(row == 4, dec_ref[0], 0.0)
    brow = lax.broadcasted_iota(jnp.int32, (8, LANES), 0)
    bmat = jnp.zeros((8, LANES), F32)
    cmat = jnp.zeros((8, LANES), F32)
    for g in range(SSD_GROUPS):
        bmat = jnp.where(brow == g, b_ref[0][:, g * 128:(g + 1) * 128], bmat)
        cmat = jnp.where(brow == g, c_ref[0][:, g * 128:(g + 1) * 128], cmat)
    rhs = jnp.concatenate([bmat, (brow == 4).astype(F32)], axis=1)
    lh, lm, ll = _split_bf16x3(lhs)
    rh, rm, rl = _split_bf16x3(rhs)
    lhs6 = jnp.concatenate([lh, lh, lm, lm, lh, ll], axis=0).astype(BF16)
    rhs6 = jnp.concatenate([rh, rm, rh, rm, rl, rh], axis=0).astype(BF16)
    both = lax.dot_general(lhs6, rhs6, TN_DIMS, preferred_element_type=F32)
    new = st_ref[0] * both[:, LANES:] + both[:, :LANES]
    nst_ref[0] = new
    yg = _dot_nt(cmat.astype(BF16), new.astype(BF16))
    y_ref[0] = jnp.sum(jnp.where(row == lane_g, yg, 0.0), axis=0, keepdims=True)


def _ssd_decode_state(xdt, dec, bm, cm, state):
    t = xdt.shape[0]
    tok3 = lambda w: pl.BlockSpec((1, 1, w), lambda i: (i, 0, 0))
    st_spec = pl.BlockSpec((1, SSD_D_INNER, SSD_STATE), lambda i: (i, 0, 0))
    return pl.pallas_call(
        _ssd_decode_state_kernel,
        out_shape=(jax.ShapeDtypeStruct((t, 1, SSD_D_INNER), F32),
                   jax.ShapeDtypeStruct((t, SSD_D_INNER, SSD_STATE), F32)),
        grid=(t,),
        in_specs=[tok3(SSD_D_INNER), tok3(SSD_D_INNER), tok3(512), tok3(512), st_spec],
        out_specs=(tok3(SSD_D_INNER), st_spec),
        compiler_params=_params(("parallel",)),
        name="ssd_decode_state",
    )(xdt.reshape(t, 1, -1), dec.reshape(t, 1, -1), bm.reshape(t, 1, -1), cm.reshape(t, 1, -1), state)


def _ssd_decode_post_kernel(y_ref, xs_ref, z_ref, d_ref, nw_ref, o_ref):
    o_ref[...] = _ssd_gate_norm(y_ref[...], xs_ref[...], z_ref[...], d_ref, nw_ref).astype(o_ref.dtype)


def _ssd_decode_post(y, xs, z, d_exp, nw):
    return pl.pallas_call(
        _ssd_decode_post_kernel,
        out_shape=jax.ShapeDtypeStruct(y.shape, BF16),
        compiler_params=pltpu.CompilerParams(vmem_limit_bytes=VMEM_LIMIT),
        name="ssd_decode_post",
    )(y, xs, z, d_exp, nw)


def _xa_prompt_kernel(x_ref, g_ref, wq_ref, k_ref, v_ref, wo_ref, o_ref):
    x = x_ref[...]
    q = _dot(_rms(x, g_ref[...]).astype(BF16), wq_ref[...]).astype(BF16)
    k = k_ref[0].astype(BF16)
    v = v_ref[0].astype(BF16)
    heads = []
    for h in range(XA_HEADS):
        sl = slice(h * XA_HEAD_DIM, (h + 1) * XA_HEAD_DIM)
        s = _dot_nt(q[:, sl], k[:, sl]) * XA_SCALE
        p = jnp.exp(s - jnp.max(s, axis=-1, keepdims=True))
        o = _dot(p.astype(BF16), v[:, sl]) / jnp.sum(p, axis=-1, keepdims=True)
        heads.append(o.astype(BF16))
    o_ref[...] = x + _dot(jnp.concatenate(heads, axis=1), wo_ref[...])


def _xa_prompt(x, gain, wq, mem_k, mem_v, wo, batch, seq):
    m, d = x.shape
    tm = 512
    nt = seq // tm
    mem = mem_k.shape[1]
    return pl.pallas_call(
        _xa_prompt_kernel,
        out_shape=jax.ShapeDtypeStruct((m, d), F32),
        grid=(m // tm,),
        in_specs=[
            pl.BlockSpec((tm, d), lambda i: (i, 0)),
            pl.BlockSpec((1, d), lambda i: (0, 0)),
            pl.BlockSpec((d, d), lambda i: (0, 0)),
            pl.BlockSpec((1, mem, d), lambda i: (i // nt, 0, 0)),
            pl.BlockSpec((1, mem, d), lambda i: (i // nt, 0, 0)),
            pl.BlockSpec((d, d), lambda i: (0, 0)),
        ],
        out_specs=pl.BlockSpec((tm, d), lambda i: (i, 0)),
        compiler_params=_params(("parallel",)),
        name="xa_prompt",
    )(x, gain.reshape(1, d), wq, mem_k, mem_v, wo)


def _xa_decode_kernel(q_ref, k_ref, v_ref, o_ref, *, tokens):
    n = k_ref.shape[2]
    rows = 8 * tokens
    lane = lax.broadcasted_iota(jnp.int32, (rows, n), 1)
    own = lane % 8 == lax.broadcasted_iota(jnp.int32, (rows, n), 0) % 8
    first_half = lane % 8 < XA_HEADS
    strides = [8 << i for i in range((n // 8).bit_length() - 1)]

    def over_tokens(op, x):
        for sh in strides:
            x = op(x, pltpu.roll(x, sh, 1))
        return x

    g = jnp.concatenate([_dot_nt(q_ref[0, t].astype(BF16), k_ref[0, t].astype(BF16)) for t in range(tokens)],
                        axis=0)
    part = jnp.sum(jnp.where(own, g, 0.0).reshape(tokens, 8, n), axis=1, keepdims=True)
    part = jnp.broadcast_to(part, (tokens, 8, n)).reshape(rows, n)
    s = (part + pltpu.roll(part, n - XA_HEADS, 1)) * XA_SCALE
    p = jnp.exp(s - over_tokens(jnp.maximum, s))
    p = p / over_tokens(jnp.add, p)
    p = jnp.where(own, jnp.where(first_half, p, pltpu.roll(p, XA_HEADS, 1)), 0.0)
    for t in range(tokens):
        o_ref[0, t] = _dot(p[8 * t:8 * t + 8].astype(BF16), v_ref[0, t].astype(BF16))


def _xa_decode(q, mem_k, mem_v, layer, tokens=4):
    t, d = q.shape
    rows = mem_k.shape[2]
    half = XA_HEAD_DIM // 2
    q8 = jnp.transpose(q.reshape(t, XA_HEADS, 2, half), (0, 2, 1, 3)).reshape(t // tokens, tokens, 8, half)
    kv_spec = pl.BlockSpec((1, tokens, rows, half), lambda i: (layer, i, 0, 0))
    tok_spec = pl.BlockSpec((1, tokens, 8, half), lambda i: (i, 0, 0, 0))
    o8 = pl.pallas_call(
        functools.partial(_xa_decode_kernel, tokens=tokens),
        out_shape=jax.ShapeDtypeStruct((t // tokens, tokens, 8, half), F32),
        grid=(t // tokens,),
        in_specs=[tok_spec, kv_spec, kv_spec],
        out_specs=tok_spec,
        compiler_params=_params(("parallel",)),
        name="xa_decode",
    )(q8, mem_k, mem_v)
    return jnp.transpose(o8.reshape(t, 2, XA_HEADS, half), (0, 2, 1, 3)).reshape(t, d)


def kernel(x_prompt, x_sample, cache_mla_ckv, cache_mla_kpe, cache_diff_k, cache_diff_v, state_ssm, state_conv, cache_mem_k, cache_mem_v, page_table, mem_prompt, norm_mix, norm_xa, norm_ffn, norm_final, xa_mem_norm, xa_wq, xa_wk, xa_wv, xa_wo, ffn_w_gate, ffn_w_up, ffn_w_down, mla_wq_a, mla_q_norm, mla_wq_b, mla_wkv_a, mla_kv_norm, mla_w_uk, mla_w_uv, mla_wo, diff_wq, diff_wk, diff_wv, diff_lambda_q1, diff_lambda_k1, diff_lambda_q2, diff_lambda_k2, diff_subln, diff_wo, ssd_w_in, ssd_conv_w, ssd_conv_b, ssd_dt_bias, ssd_A_log, ssd_D, ssd_norm, ssd_w_out):
    batch, seq, d = x_prompt.shape
    nb = x_sample.shape[0]
    depth = norm_mix.shape[0]
    n_mix = 3
    bf = lambda w: w.astype(BF16)

    n_a = mla_wq_a.shape[0]
    wkv_a = jnp.pad(mla_wkv_a, ((0, 0), (0, 0), (0, MLA_QK - mla_wkv_a.shape[2])))
    w_qkv_a = bf(jnp.concatenate([mla_wq_a, wkv_a], axis=2))
    wq_b = mla_wq_b.reshape(n_a, -1, MLA_HEADS, MLA_NOPE + MLA_ROPE)
    wq_b = bf(jnp.concatenate([
        wq_b[..., :MLA_NOPE].reshape(n_a, -1, MLA_HEADS * MLA_NOPE),
        jnp.pad(wq_b[..., MLA_NOPE:], ((0, 0), (0, 0), (0, 0), (0, 64))).reshape(n_a, -1, MLA_HEADS * 128),
    ], axis=2))
    w_uk = bf(jnp.transpose(mla_w_uk, (0, 2, 3, 1)))
    w_uk_t = bf(jnp.transpose(mla_w_uk, (0, 2, 1, 3)))
    w_uv = bf(jnp.transpose(mla_w_uv, (0, 2, 1, 3)))
    w_mla_o = bf(mla_wo)
    w_diff_qkv = bf(jnp.concatenate([diff_wq, diff_wk, diff_wv], axis=2))
    w_diff_o = bf(diff_wo)
    w_ssd_in = bf(jnp.pad(ssd_w_in, ((0, 0), (0, 0), (0, LANES - SSD_HEADS))))
    w_ssd_out = bf(ssd_w_out)
    pad_heads = lambda a: jnp.pad(a, ((0, 0), (0, LANES - SSD_HEADS)))
    ssd_dtb = pad_heads(ssd_dt_bias)
    ssd_a = pad_heads(-jnp.exp(ssd_A_log.astype(F32)))
    ssd_d_exp = jnp.repeat(ssd_D, SSD_HEAD_DIM, axis=1)
    lane_head = jnp.arange(SSD_D_INNER)[None, :] // SSD_HEAD_DIM
    expand = (jnp.arange(LANES)[:, None] == lane_head).astype(F32)
    expand3 = ((jnp.arange(LANES)[:, None] % SSD_HEADS == lane_head)
               & (jnp.arange(LANES)[:, None] < 3 * SSD_HEADS)).astype(BF16)
    w_xa_q, w_xa_o = bf(xa_wq), bf(xa_wo)
    w_xa_kv = bf(jnp.concatenate([xa_wk, xa_wv], axis=2))
    w_gate, w_up, w_down = bf(ffn_w_gate), bf(ffn_w_up), bf(ffn_w_down)

    def mem_view(c):
        mt = c.shape[2]
        c = c.reshape(depth, nb, mt, XA_HEADS, 2, XA_HEAD_DIM // 2)
        return jnp.transpose(c, (0, 1, 2, 4, 3, 5)).reshape(depth, nb, mt * 8, XA_HEAD_DIM // 2)

    mem_k, mem_v = mem_view(cache_mem_k), mem_view(cache_mem_v)
    n_pool = cache_diff_k.shape[1]
    pool_kpe_t = jnp.swapaxes(cache_mla_kpe, 2, 3)
    pool_kt = jnp.transpose(cache_diff_k, (0, 1, 3, 4, 5, 2)).reshape(-1, n_pool, 512, PAGE_SIZE)
    pool_v = cache_diff_v.reshape(-1, n_pool, PAGE_SIZE * DIFF_GROUPS, 2 * DIFF_HEAD_DIM)
    mem_flat = mem_prompt.reshape(-1, d)

    def run(x, pos_tab, prompt):
        m = x.shape[0]
        cos, sin = _rope_tables(pos_tab)
        new = {k: [] for k in ("mla_ckv", "mla_kpe", "diff_k", "diff_v", "ssm", "conv", "mem_k", "mem_v")}
        for i in range(depth):
            kind, j = i % n_mix, i // n_mix
            g_mix = norm_mix[i]
            if kind == 0:
                qkv_a = _linear(x, w_qkv_a[j], gain=g_mix, name="mla_qkv_a")
                q = _linear(qkv_a, wq_b[j], gain=mla_q_norm[j], name="mla_q_b")
                if prompt:
                    q_heads, k_heads, ckv, kpe, v_lat = _mla_prep(q, qkv_a, cos, sin, w_uk_t[j], mla_kv_norm[j], False)
                    o_lat = _flash_mla(q_heads, k_heads, v_lat, batch, seq)
                else:
                    qcat, kcat, ckv, kpe = _mla_prep(q, qkv_a, cos, sin, w_uk[j], mla_kv_norm[j], True)
                    o_lat = _decode_mla(page_table, jnp.transpose(qcat, (1, 0, 2)), kcat,
                                        cache_mla_ckv, pool_kpe_t, j).reshape(m, -1)
                x = _mla_out(o_lat, w_uv[j], w_mla_o[j], x)
                new["mla_ckv"].append(ckv)
                new["mla_kpe"].append(kpe)
            elif kind == 1:
                lam_init = 0.8 - 0.6 * math.exp(-0.3 * i)
                lams = (diff_lambda_q1[j], diff_lambda_k1[j], diff_lambda_q2[j], diff_lambda_k2[j])
                qkv = _linear(x, w_diff_qkv[j], gain=g_mix, name="diff_qkv")
                q_b, k32, k_b, v32, v_b = _diff_prep(qkv, cos, sin, batch if prompt else None)
                if prompt:
                    k32 = jnp.transpose(k32.reshape(batch, DIFF_GROUPS, 2, DIFF_HEAD_DIM, seq), (0, 4, 1, 2, 3))
                if prompt:
                    o = _flash_diff(q_b, k_b, v_b, lams, diff_subln[j], lam_init, batch, seq)
                else:
                    q5 = q_b.reshape(m, DIFF_GROUPS, 2, 2, DIFF_HEAD_DIM)
                    eye_g = jnp.eye(DIFF_GROUPS, dtype=BF16)
                    eye_c = jnp.eye(2, dtype=BF16)
                    q_rows = jnp.einsum("tgrcd,gG,cC->tcrgGCd", q5, eye_g, eye_c).reshape(m, 16, 512)
                    v_rows = jnp.tile(v_b.astype(F32).reshape(m, DIFF_GROUPS, LANES), (1, 4, 1))
                    o = _decode_diff(page_table, q_rows, k_b, v_rows, pool_kt, pool_v, lams, diff_subln[j],
                                     lam_init, j)
                    o = jnp.transpose(o.reshape(m, 2, DIFF_GROUPS, LANES), (0, 2, 1, 3)).reshape(m, -1)
                x = _linear(o, w_diff_o[j], res=x, name="diff_o")
                new["diff_k"].append(k32)
                new["diff_v"].append(v32)
            else:
                z, xbc, dt = _linear(x, w_ssd_in[j], gain=g_mix, splits=(SSD_D_INNER, SSD_CONV_DIM, LANES),
                                     name="ssd_in")
                cw, cb = ssd_conv_w[j], ssd_conv_b[j].reshape(1, -1)
                dtb, a_neg = ssd_dtb[j].reshape(1, -1), ssd_a[j].reshape(1, -1)
                d_exp, nw = ssd_d_exp[j].reshape(1, -1), ssd_norm[j].reshape(1, -1)
                if prompt:
                    yn, st, buf = _ssd_prompt(z, xbc, dt, cw, cb, dtb, a_neg, expand3, d_exp, nw, batch, seq)
                else:
                    xs, xdt, dec, bm, cm, nbuf = _ssd_decode_pre(
                        xbc, dt, jnp.transpose(state_conv[j], (1, 0, 2)), cw, cb, dtb, a_neg, expand)
                    y, st = _ssd_decode_state(xdt, dec, bm, cm, state_ssm[j].reshape(m, SSD_D_INNER, SSD_STATE))
                    yn = _ssd_decode_post(y.reshape(m, -1), xs, z, d_exp, nw)
                    buf = jnp.transpose(nbuf, (1, 0, 2))
                x = _linear(yn, w_ssd_out[j], res=x, name="ssd_out")
                new["ssm"].append(st.reshape(-1, SSD_HEADS, SSD_HEAD_DIM, SSD_STATE))
                new["conv"].append(buf)
            if prompt:
                mk, mv = _linear(mem_flat, w_xa_kv[i], gain=xa_mem_norm[i], splits=(d, d), name="xa_mem_kv")
                new["mem_k"].append(mk)
                new["mem_v"].append(mv)
                x = _xa_prompt(x, norm_xa[i], w_xa_q[i], mk.reshape(batch, -1, d), mv.reshape(batch, -1, d),
                               w_xa_o[i], batch, seq)
            else:
                q = _linear(x, w_xa_q[i], gain=norm_xa[i], name="xa_q")
                o = _xa_decode(q, mem_k, mem_v, i)
                x = _linear(o, w_xa_o[i], res=x, name="xa_o")
            x = _ffn(x, norm_ffn[i], w_gate[i], w_up[i], w_down[i], norm_final if i == depth - 1 else None)
        stack = lambda v: v[0][None] if len(v) == 1 else jnp.stack(v)
        return x, {k: stack(v) for k, v in new.items() if v}

    pos_p = jnp.arange(seq, dtype=jnp.int32)
    pos_s = jnp.full((nb,), PAST_LEN, jnp.int32)
    y_p, new_p = run(x_prompt.reshape(batch * seq, d), pos_p, True)
    y_s, new_s = run(x_sample.reshape(nb, d), pos_s, False)

    n_b = new_p["diff_k"].shape[0]
    n_c = new_p["ssm"].shape[0]
    mem_t = mem_prompt.shape[1]
    return (
        y_p.reshape(batch, seq, d),
        y_s.reshape(nb, 1, d),
        new_p["mla_ckv"].reshape(n_a, batch, seq, MLA_KV_LORA),
        new_p["mla_kpe"].reshape(n_a, batch, seq, MLA_ROPE),
        new_p["diff_k"].reshape(n_b, batch, seq, DIFF_GROUPS, 2, DIFF_HEAD_DIM),
        new_p["diff_v"].reshape(n_b, batch, seq, DIFF_GROUPS, 2 * DIFF_HEAD_DIM),
        new_p["ssm"].reshape(n_c, batch, SSD_HEADS, SSD_HEAD_DIM, SSD_STATE),
        new_p["conv"],
        new_p["mem_k"].reshape(depth, batch, mem_t, XA_HEADS, XA_HEAD_DIM),
        new_p["mem_v"].reshape(depth, batch, mem_t, XA_HEADS, XA_HEAD_DIM),
        new_s["mla_ckv"].reshape(n_a, nb, 1, MLA_KV_LORA),
        new_s["mla_kpe"].reshape(n_a, nb, 1, MLA_ROPE),
        new_s["diff_k"].reshape(n_b, nb, 1, DIFF_GROUPS, 2, DIFF_HEAD_DIM),
        new_s["diff_v"].reshape(n_b, nb, 1, DIFF_GROUPS, 2 * DIFF_HEAD_DIM),
        new_s["ssm"].reshape(n_c, nb, SSD_HEADS, SSD_HEAD_DIM, SSD_STATE),
        new_s["conv"],
    )
```

```python
import functools
import math

import jax
import jax.numpy as jnp
from jax import lax
from jax.experimental import pallas as pl
from jax.experimental.pallas import tpu as pltpu

F32 = jnp.float32
BF16 = jnp.bfloat16

RMS_EPS = 1e-6
ROPE_THETA = 10000.0
NEG_INF = -1e30
PAST_LEN = 8192
PAGE_SIZE = 128

MLA_HEADS = 8
MLA_NOPE = 128
MLA_ROPE = 64
MLA_KV_LORA = 256
MLA_V = 128
MLA_QK = 384
MLA_SCALE = (MLA_NOPE + MLA_ROPE) ** -0.5

DIFF_HEADS = 8
DIFF_GROUPS = 4
DIFF_HEAD_DIM = 64
DIFF_SCALE = DIFF_HEAD_DIM ** -0.5

SSD_D_INNER = 2048
SSD_HEADS = 32
SSD_HEAD_DIM = 64
SSD_GROUPS = 4
SSD_STATE = 128
SSD_CONV = 4
SSD_CONV_DIM = SSD_D_INNER + 2 * SSD_GROUPS * SSD_STATE
SSD_CHUNK = 128

XA_HEADS = 4
XA_HEAD_DIM = 256
XA_SCALE = XA_HEAD_DIM ** -0.5

LOG2E = math.log2(math.e)
SOFTMAX_ROWS = 32
LANES = 128
VMEM_LIMIT = 48 * 1024 * 1024

NT_DIMS = (((1,), (1,)), ((), ()))
TN_DIMS = (((0,), (0,)), ((), ()))


def _params(sem):
    return pltpu.CompilerParams(dimension_semantics=sem, vmem_limit_bytes=VMEM_LIMIT)


def _rms(x, g):
    return x * lax.rsqrt(jnp.mean(x * x, axis=-1, keepdims=True) + RMS_EPS) * g


def _silu(x):
    return x * jax.nn.sigmoid(x)


def _dot(a, b):
    return jnp.dot(a, b, preferred_element_type=F32)


def _dot_nt(a, b):
    return lax.dot_general(a, b, NT_DIMS, preferred_element_type=F32)


def _dot_exact(a, b, dims=(((1,), (0,)), ((), ()))):
    return lax.dot_general(a, b, dims, precision=lax.Precision.HIGHEST, preferred_element_type=F32)


def _linear_kernel(*refs, has_norm, has_res):
    refs = list(refs)
    x_ref = refs.pop(0)
    g_ref = refs.pop(0) if has_norm else None
    w_ref = refs.pop(0)
    r_ref = refs.pop(0) if has_res else None
    x = x_ref[...]
    if has_norm:
        x = _rms(x.astype(F32), g_ref[...])
    h = x.astype(BF16)
    start = 0
    for o_ref in refs:
        width = o_ref.shape[1]
        tn = _pick_tn(width)
        for j in range(width // tn):
            cols = slice(start + j * tn, start + (j + 1) * tn)
            acc = _dot(h, w_ref[:, cols])
            if has_res:
                acc = acc + r_ref[:, cols]
            o_ref[:, j * tn:(j + 1) * tn] = acc.astype(o_ref.dtype)
        start += width


def _pick_tn(n):
    for tn in (512, 256, 128):
        if n % tn == 0:
            return tn
    return n


def _linear(x, w, gain=None, res=None, out_dtype=F32, x_col=0, splits=None, name="linear"):
    m = x.shape[0]
    k, n = w.shape
    widths = (n,) if splits is None else tuple(splits)
    tm = min(m, 1024 if n <= 2048 else 512)
    in_specs = [pl.BlockSpec((tm, k), lambda i: (i, x_col))]
    args = [x]
    if gain is not None:
        in_specs.append(pl.BlockSpec((1, k), lambda i: (0, 0)))
        args.append(gain.reshape(1, k).astype(F32))
    in_specs.append(pl.BlockSpec((k, n), lambda i: (0, 0), pipeline_mode=pl.Buffered(1)))
    args.append(w)
    if res is not None:
        in_specs.append(pl.BlockSpec((tm, n), lambda i: (i, 0)))
        args.append(res)
    outs = pl.pallas_call(
        functools.partial(_linear_kernel, has_norm=gain is not None, has_res=res is not None),
        out_shape=tuple(jax.ShapeDtypeStruct((m, wd), out_dtype) for wd in widths),
        grid=(m // tm,),
        in_specs=in_specs,
        out_specs=tuple(pl.BlockSpec((tm, wd), lambda i: (i, 0)) for wd in widths),
        compiler_params=_params(("parallel",)),
        name=name,
    )(*args)
    return outs[0] if splits is None else outs


def _ffn_kernel(x_ref, g_ref, wg_ref, wu_ref, wd_ref, *rest, chunk):
    o_ref = rest[-1]
    x = x_ref[...]
    h = _rms(x, g_ref[...]).astype(BF16)
    acc = x
    for c in range(wg_ref.shape[1] // chunk):
        cols = slice(c * chunk, (c + 1) * chunk)
        a = _silu(_dot(h, wg_ref[:, cols])) * _dot(h, wu_ref[:, cols])
        acc = acc + _dot(a.astype(BF16), wd_ref[cols, :])
    if len(rest) == 2:
        acc = _rms(acc, rest[0][...])
    o_ref[...] = acc


def _ffn(x, gain, wg, wu, wd, final_gain=None):
    m, d = x.shape
    hid = wg.shape[1]
    tm = min(m, 1024)
    resident = lambda shape: pl.BlockSpec(shape, lambda i: (0, 0), pipeline_mode=pl.Buffered(1))
    vec = pl.BlockSpec((1, d), lambda i: (0, 0))
    extra = [] if final_gain is None else [final_gain.reshape(1, d)]
    return pl.pallas_call(
        functools.partial(_ffn_kernel, chunk=256),
        out_shape=jax.ShapeDtypeStruct((m, d), F32),
        grid=(m // tm,),
        in_specs=[pl.BlockSpec((tm, d), lambda i: (i, 0)), vec, resident((d, hid)), resident((d, hid)),
                  resident((hid, d))] + [vec] * len(extra),
        out_specs=pl.BlockSpec((tm, d), lambda i: (i, 0)),
        compiler_params=_params(("parallel",)),
        name="ffn",
    )(x, gain.reshape(1, d), wg, wu, wd, *extra)


def _rope(x, cos, sin):
    n = x.shape[1]
    reps = n // LANES
    if reps > 1:
        cos = jnp.concatenate([cos] * reps, axis=1)
        sin = jnp.concatenate([sin] * reps, axis=1)
    lane = lax.broadcasted_iota(jnp.int32, x.shape, 1)
    first_half = (lane % 64) < 32
    partner = jnp.where(first_half, pltpu.roll(x, n - 32, 1), pltpu.roll(x, 32, 1))
    return x * cos + partner * sin


def _rope_tables(pos):
    inv = ROPE_THETA ** (-jnp.arange(32, dtype=F32) * 2.0 / 64)
    ang = pos.astype(F32)[:, None] * inv[None, :]
    c, s = jnp.cos(ang), jnp.sin(ang)
    return jnp.concatenate([c, c, c, c], axis=1), jnp.concatenate([-s, s, -s, s], axis=1)


def _mla_prep_kernel(q_ref, kv_ref, cos_ref, sin_ref, wuk_ref, kvg_ref,
                     qh_ref, kh_ref, ckv_ref, kpe_ref, *v_ref, absorb):
    cos, sin = cos_ref[...], sin_ref[...]
    q = q_ref[...] * (MLA_SCALE * LOG2E)
    q_pe = _rope(q[:, 1024:], cos, sin).astype(BF16)
    q_nope = q[:, :1024].astype(BF16)
    kv = kv_ref[...]
    ckv = _rms(kv[:, :256], kvg_ref[...])
    kpe = _rope(kv[:, 256:], cos, sin)
    ckv_ref[...] = ckv
    kpe_ref[...] = kpe[:, :64]
    ckv_b, kpe_b = ckv.astype(BF16), kpe.astype(BF16)
    if absorb:
        kh_ref[:, 0:256] = ckv_b
        kh_ref[:, 256:384] = kpe_b
    else:
        v_ref[0][...] = ckv_b
    for h in range(MLA_HEADS):
        head = slice(h * 128, (h + 1) * 128)
        if absorb:
            qh_ref[h, :, 0:256] = _dot(q_nope[:, head], wuk_ref[h]).astype(BF16)
            qh_ref[h, :, 256:384] = q_pe[:, head]
        else:
            qh_ref[h, :, 0:128] = q_nope[:, head]
            qh_ref[h, :, 128:256] = q_pe[:, head]
            kh_ref[h, :, 0:128] = _dot(ckv_b, wuk_ref[h]).astype(BF16)
            kh_ref[h, :, 128:256] = kpe_b


def _mla_prep(q, qkv_a, cos, sin, wuk, kv_gain, absorb):
    m = q.shape[0]
    tm = min(m, 512)
    nt = cos.shape[0] // tm
    row = lambda i: (i, 0)
    per_head = lambda i: (0, i, 0)
    if absorb:
        head_shapes = [jax.ShapeDtypeStruct((MLA_HEADS, m, MLA_QK), BF16), jax.ShapeDtypeStruct((m, MLA_QK), BF16)]
        head_specs = [pl.BlockSpec((MLA_HEADS, tm, MLA_QK), per_head), pl.BlockSpec((tm, MLA_QK), row)]
        extra_shapes, extra_specs = [], []
    else:
        head_shapes = [jax.ShapeDtypeStruct((MLA_HEADS, m, 256), BF16)] * 2
        head_specs = [pl.BlockSpec((MLA_HEADS, tm, 256), per_head)] * 2
        extra_shapes = [jax.ShapeDtypeStruct((m, MLA_KV_LORA), BF16)]
        extra_specs = [pl.BlockSpec((tm, MLA_KV_LORA), row)]
    return pl.pallas_call(
        functools.partial(_mla_prep_kernel, absorb=absorb),
        out_shape=tuple(head_shapes + [jax.ShapeDtypeStruct((m, MLA_KV_LORA), F32),
                                       jax.ShapeDtypeStruct((m, MLA_ROPE), F32)] + extra_shapes),
        grid=(m // tm,),
        in_specs=[
            pl.BlockSpec((tm, 2048), row),
            pl.BlockSpec((tm, MLA_QK), lambda i: (i, 1)),
            pl.BlockSpec((tm, LANES), lambda i: (i % nt, 0)),
            pl.BlockSpec((tm, LANES), lambda i: (i % nt, 0)),
            pl.BlockSpec((MLA_HEADS,) + wuk.shape[1:], lambda i: (0, 0, 0)),
            pl.BlockSpec((1, 256), lambda i: (0, 0)),
        ],
        out_specs=tuple(head_specs + [pl.BlockSpec((tm, MLA_KV_LORA), row), pl.BlockSpec((tm, MLA_ROPE), row)]
                        + extra_specs),
        compiler_params=_params(("parallel",)),
        name="mla_prep",
    )(q, qkv_a, cos, sin, wuk, kv_gain.reshape(1, 256))


def _tree(op, xs):
    xs = list(xs)
    while len(xs) > 1:
        xs = [op(xs[i], xs[i + 1]) if i + 1 < len(xs) else xs[i] for i in range(0, len(xs), 2)]
    return xs[0]


def _online_softmax(s, row0, m_scr, l_scr, acc_scr, masked):
    rows_n, tk = s.shape
    ch = SOFTMAX_ROWS
    acc_tiles = acc_scr.shape[1] // LANES
    out = []
    for c in range(rows_n // ch):
        rows = pl.ds(row0 + c * ch, ch)
        first_row, last_row = c * ch, (c + 1) * ch - 1
        tiles = []
        for t in range(tk // LANES):
            lo = t * LANES
            if masked and lo > last_row:
                continue
            tile = s[first_row:last_row + 1, lo:lo + LANES]
            if masked and lo + LANES - 1 > first_row:
                qpos = first_row + lax.broadcasted_iota(jnp.int32, (ch, LANES), 0)
                kpos = lo + lax.broadcasted_iota(jnp.int32, (ch, LANES), 1)
                tile = jnp.where(kpos <= qpos, tile, NEG_INF)
            tiles.append(tile)
        m_prev = m_scr[rows, :]
        m_new = jnp.maximum(m_prev, jnp.max(_tree(jnp.maximum, tiles), axis=1, keepdims=True))
        alpha = jnp.exp2(m_prev - m_new)
        ps = [jnp.exp2(t - m_new) for t in tiles]
        if l_scr is not None:
            l_scr[rows, :] = alpha * l_scr[rows, :] + jnp.sum(_tree(jnp.add, ps), axis=1, keepdims=True)
        m_scr[rows, :] = m_new
        acc_scr[rows, :] = acc_scr[rows, :] * jnp.concatenate([alpha] * acc_tiles, axis=1)
        ps += [jnp.zeros((ch, LANES), F32)] * (tk // LANES - len(ps))
        out.append(jnp.concatenate(ps, axis=1).astype(BF16))
    return jnp.concatenate(out, axis=0)


def _causal_tiles(nt):
    pairs = [(qi, ki) for qi in range(nt) for ki in range(qi + 1)]
    return jnp.array([p[0] for p in pairs], jnp.int32), jnp.array([p[1] for p in pairs], jnp.int32)


def _flash_mla_kernel(qi_ref, ki_ref, q_ref, k_ref, v_ref, o_ref, m_scr, l_scr, acc_scr, *, tile):
    qi, ki = qi_ref[pl.program_id(1)], ki_ref[pl.program_id(1)]

    @pl.when(ki == 0)
    def _():
        m_scr[...] = jnp.full_like(m_scr, NEG_INF)
        l_scr[...] = jnp.zeros_like(l_scr)
        acc_scr[...] = jnp.zeros_like(acc_scr)

    def all_heads(masked):
        def head(h, carry):
            row0 = pl.multiple_of(h * tile, tile)
            p = _online_softmax(_dot_nt(q_ref[h], k_ref[h]), row0, m_scr, l_scr, acc_scr, masked)
            acc_scr[pl.ds(row0, tile), :] += _dot(p, v_ref[...])
            return carry
        lax.fori_loop(0, MLA_HEADS, head, 0, unroll=4)

    @pl.when(ki < qi)
    def _():
        all_heads(False)

    @pl.when(ki == qi)
    def _():
        all_heads(True)
        for h in range(MLA_HEADS):
            rows = slice(h * tile, (h + 1) * tile)
            l = l_scr[rows, :]
            o = acc_scr[rows, :] / jnp.concatenate([l, l], axis=1)
            o_ref[:, h * 256:(h + 1) * 256] = o.astype(o_ref.dtype)


def _flash_mla(q_heads, k_heads, v, batch, seq, tile=512):
    nt = seq // tile
    qi_tab, ki_tab = _causal_tiles(nt)
    return pl.pallas_call(
        functools.partial(_flash_mla_kernel, tile=tile),
        out_shape=jax.ShapeDtypeStruct((batch * seq, MLA_HEADS * MLA_KV_LORA), BF16),
        grid_spec=pltpu.PrefetchScalarGridSpec(
            num_scalar_prefetch=2,
            grid=(batch, qi_tab.shape[0]),
            in_specs=[
                pl.BlockSpec((MLA_HEADS, tile, 256), lambda b, t, qt, kt: (0, b * nt + qt[t], 0)),
                pl.BlockSpec((MLA_HEADS, tile, 256), lambda b, t, qt, kt: (0, b * nt + kt[t], 0)),
                pl.BlockSpec((tile, MLA_KV_LORA), lambda b, t, qt, kt: (b * nt + kt[t], 0)),
            ],
            out_specs=pl.BlockSpec((tile, MLA_HEADS * MLA_KV_LORA), lambda b, t, qt, kt: (b * nt + qt[t], 0)),
            scratch_shapes=[
                pltpu.VMEM((MLA_HEADS * tile, LANES), F32),
                pltpu.VMEM((MLA_HEADS * tile, LANES), F32),
                pltpu.VMEM((MLA_HEADS * tile, MLA_KV_LORA), F32),
            ],
        ),
        compiler_params=_params(("parallel", "arbitrary")),
        name="flash_mla",
    )(qi_tab, ki_tab, q_heads, k_heads, v)


def _mla_out_kernel(o_ref, wuv_ref, wo_ref, x_ref, y_ref):
    o = o_ref[...].astype(BF16)
    heads = [_dot(o[:, h * 256:(h + 1) * 256], wuv_ref[h]).astype(BF16) for h in range(MLA_HEADS)]
    y_ref[...] = x_ref[...] + _dot(jnp.concatenate(heads, axis=1), wo_ref[...])


def _mla_out(o_lat, wuv, wo, x):
    m, d = x.shape
    tm = min(m, 512)
    return pl.pallas_call(
        _mla_out_kernel,
        out_shape=jax.ShapeDtypeStruct((m, d), F32),
        grid=(m // tm,),
        in_specs=[
            pl.BlockSpec((tm, 2048), lambda i: (i, 0)),
            pl.BlockSpec((MLA_HEADS, 256, 128), lambda i: (0, 0, 0)),
            pl.BlockSpec((1024, d), lambda i: (0, 0)),
            pl.BlockSpec((tm, d), lambda i: (i, 0)),
        ],
        out_specs=pl.BlockSpec((tm, d), lambda i: (i, 0)),
        compiler_params=_params(("parallel",)),
        name="mla_out",
    )(o_lat, wuv, wo, x)


def _decode_mla_kernel(pt_ref, q_ref, kn_ref, ckv_hbm, kpe_hbm, o_ref,
                       cbuf, pbuf, kb_scr, sem, *, layer, n_pages):
    b = pl.program_id(0)
    slot = b % 2

    def copies(tok, sl, p):
        page = pt_ref[tok * n_pages + p]
        return (pltpu.make_async_copy(ckv_hbm.at[layer, page], cbuf.at[sl, p], sem.at[0, sl]),
                pltpu.make_async_copy(kpe_hbm.at[layer, page], pbuf.at[sl, p], sem.at[1, sl]))

    def issue(tok, sl):
        def body(p, carry):
            for cp in copies(tok, sl, p):
                cp.start()
            return carry
        lax.fori_loop(0, n_pages, body, 0)

    def wait_all(tok, sl):
        def body(p, carry):
            for cp in copies(tok, sl, p):
                cp.wait()
            return carry
        lax.fori_loop(0, n_pages, body, 0)

    last = pl.num_programs(0) - 1

    @pl.when(b == 0)
    def _():
        issue(0, 0)

    wait_all(b, slot)

    nxt = jnp.minimum(b + 1, last)
    for p in range(n_pages):
        for cp in copies(nxt, 1 - slot, p):
            cp.start()

    n_keys = n_pages * PAGE_SIZE
    kb_scr[...] = cbuf[slot].reshape(n_keys, MLA_KV_LORA).astype(BF16)

    q = q_ref[0]
    kn = kn_ref[0].astype(F32)
    s_self = jnp.sum(q.astype(F32) * kn, axis=-1, keepdims=True)
    q_pe = q[:, 256:320]
    kpe_t = jnp.concatenate([pbuf[slot, p].astype(BF16) for p in range(n_pages)], axis=1)
    s = _dot_nt(q[:, :256], kb_scr[...]) + _dot(q_pe, kpe_t)
    m = jnp.maximum(jnp.max(s, axis=-1, keepdims=True), s_self)
    p = jnp.exp2(s - m)
    p_self = jnp.exp2(s_self - m)
    l = jnp.sum(p, axis=-1, keepdims=True) + p_self
    acc = _dot(p.astype(BF16), kb_scr[...]) + p_self * kn[:, :256]
    o_ref[0] = acc / l

    @pl.when(b == last)
    def _():
        wait_all(last, 1 - slot)


def _decode_mla(page_table, q_tok, k_new, cache_ckv, cache_kpe, layer):
    nb, n_pages = page_table.shape
    n_keys = n_pages * PAGE_SIZE
    return pl.pallas_call(
        functools.partial(_decode_mla_kernel, layer=layer, n_pages=n_pages),
        out_shape=jax.ShapeDtypeStruct((nb, MLA_HEADS, MLA_KV_LORA), F32),
        grid_spec=pltpu.PrefetchScalarGridSpec(
            num_scalar_prefetch=1,
            grid=(nb,),
            in_specs=[
                pl.BlockSpec((1, MLA_HEADS, MLA_QK), lambda b, pt: (b, 0, 0)),
                pl.BlockSpec((1, 1, MLA_QK), lambda b, pt: (b, 0, 0)),
                pl.BlockSpec(memory_space=pl.ANY),
                pl.BlockSpec(memory_space=pl.ANY),
            ],
            out_specs=pl.BlockSpec((1, MLA_HEADS, MLA_KV_LORA), lambda b, pt: (b, 0, 0)),
            scratch_shapes=[
                pltpu.VMEM((2, n_pages, PAGE_SIZE, MLA_KV_LORA), F32),
                pltpu.VMEM((2, n_pages, MLA_ROPE, PAGE_SIZE), F32),
                pltpu.VMEM((n_keys, MLA_KV_LORA), BF16),
                pltpu.SemaphoreType.DMA((2, 2)),
            ],
        ),
        compiler_params=_params(("arbitrary",)),
        name="decode_mla",
    )(page_table.reshape(-1), q_tok, k_new.reshape(nb, 1, MLA_QK), cache_ckv, cache_kpe)


def _diff_prep_kernel(qkv_ref, cos_ref, sin_ref, q_ref, k32_ref, kb_ref, v32_ref, vb_ref, *, k_transposed):
    qkv = qkv_ref[...]
    qk = _rope(qkv[:, :1536], cos_ref[...], sin_ref[...])
    q_ref[...] = (qk[:, :1024] * (DIFF_SCALE * LOG2E)).astype(BF16)
    k = qk[:, 1024:]
    v = qkv[:, 1536:]
    if k_transposed:
        k32_ref[0] = k.T
    else:
        k32_ref[...] = k
    kb_ref[...] = k.astype(BF16)
    v32_ref[...] = v
    vb_ref[...] = v.astype(BF16)


def _diff_prep(qkv, cos, sin, batch=None):
    m = qkv.shape[0]
    tm = min(m, 512)
    nt = cos.shape[0] // tm
    row = lambda i: (i, 0)
    if batch is None:
        k32_shape, k32_spec = jax.ShapeDtypeStruct((m, 512), F32), pl.BlockSpec((tm, 512), row)
    else:
        k32_shape = jax.ShapeDtypeStruct((batch, 512, m // batch), F32)
        k32_spec = pl.BlockSpec((1, 512, tm), lambda i: (i // nt, 0, i % nt))
    return pl.pallas_call(
        functools.partial(_diff_prep_kernel, k_transposed=batch is not None),
        out_shape=(
            jax.ShapeDtypeStruct((m, 1024), BF16),
            k32_shape,
            jax.ShapeDtypeStruct((m, 512), BF16),
            jax.ShapeDtypeStruct((m, 512), F32),
            jax.ShapeDtypeStruct((m, 512), BF16),
        ),
        grid=(m // tm,),
        in_specs=[
            pl.BlockSpec((tm, 2048), row),
            pl.BlockSpec((tm, LANES), lambda i: (i % nt, 0)),
            pl.BlockSpec((tm, LANES), lambda i: (i % nt, 0)),
        ],
        out_specs=(
            pl.BlockSpec((tm, 1024), row),
            k32_spec,
            pl.BlockSpec((tm, 512), row),
            pl.BlockSpec((tm, 512), row),
            pl.BlockSpec((tm, 512), row),
        ),
        compiler_params=_params(("parallel",)),
        name="diff_prep",
    )(qkv, cos, sin)


def _diff_lambda(lq1_ref, lk1_ref, lq2_ref, lk2_ref, lam_init):
    e1 = jnp.exp(jnp.sum(lq1_ref[...] * lk1_ref[...], axis=-1, keepdims=True))
    e2 = jnp.exp(jnp.sum(lq2_ref[...] * lk2_ref[...], axis=-1, keepdims=True))
    return e1 - e2 + lam_init


def _flash_diff_kernel(qi_ref, ki_ref, lq1_ref, lk1_ref, lq2_ref, lk2_ref, sub_ref, q_ref, k_ref, v_ref, o_ref,
                       m_scr, acc_scr, *, tile, lam_init):
    qi, ki = qi_ref[pl.program_id(2)], ki_ref[pl.program_id(2)]

    @pl.when(ki == 0)
    def _():
        m_scr[...] = jnp.full_like(m_scr, NEG_INF)
        acc_scr[...] = jnp.zeros_like(acc_scr)

    def all_rows(masked):
        q = q_ref[...]
        k = k_ref[...]
        v_ext = jnp.concatenate([v_ref[...], jnp.ones((tile, LANES), BF16)], axis=1)
        comp0 = lax.broadcasted_iota(jnp.int32, (tile, LANES), 1) < DIFF_HEAD_DIM
        zero = jnp.zeros((tile, LANES), BF16)
        for r in range(2):
            head = q[:, r * LANES:(r + 1) * LANES]
            for c in range(2):
                qc = jnp.where(comp0, head, zero) if c == 0 else jnp.where(comp0, zero, head)
                row0 = (2 * r + c) * tile
                p = _online_softmax(_dot_nt(qc, k), row0, m_scr, None, acc_scr, masked)
                acc_scr[row0:row0 + tile, :] += _dot(p, v_ext)

    @pl.when(ki < qi)
    def _():
        all_rows(False)

    @pl.when(ki == qi)
    def _():
        all_rows(True)
        lam = _diff_lambda(lq1_ref, lk1_ref, lq2_ref, lk2_ref, lam_init)
        g = sub_ref[...]
        for r in range(2):
            o = []
            for c in range(2):
                rows = slice((2 * r + c) * tile, (2 * r + c + 1) * tile)
                o.append(acc_scr[rows, :LANES] / acc_scr[rows, LANES:])
            d = _rms(o[0] - lam * o[1], g) * (1.0 - lam_init)
            o_ref[:, r * LANES:(r + 1) * LANES] = d.astype(o_ref.dtype)


def _flash_diff(q, k, v, lams, subln, lam_init, batch, seq, tile=512):
    nt = seq // tile
    qi_tab, ki_tab = _causal_tiles(nt)
    q_map = lambda b, g, t, qt, kt: (b * nt + qt[t], g)
    kv_map = lambda b, g, t, qt, kt: (b * nt + kt[t], g)
    vec = pl.BlockSpec((1, DIFF_HEAD_DIM), lambda b, g, t, qt, kt: (0, 0))
    return pl.pallas_call(
        functools.partial(_flash_diff_kernel, tile=tile, lam_init=lam_init),
        out_shape=jax.ShapeDtypeStruct((batch * seq, 1024), BF16),
        grid_spec=pltpu.PrefetchScalarGridSpec(
            num_scalar_prefetch=2,
            grid=(batch, DIFF_GROUPS, qi_tab.shape[0]),
            in_specs=[
                vec, vec, vec, vec,
                pl.BlockSpec((1, LANES), lambda b, g, t, qt, kt: (0, 0)),
                pl.BlockSpec((tile, 256), q_map),
                pl.BlockSpec((tile, LANES), kv_map),
                pl.BlockSpec((tile, LANES), kv_map),
            ],
            out_specs=pl.BlockSpec((tile, 256), q_map),
            scratch_shapes=[
                pltpu.VMEM((4 * tile, LANES), F32),
                pltpu.VMEM((4 * tile, 2 * LANES), F32),
            ],
        ),
        compiler_params=_params(("parallel", "parallel", "arbitrary")),
        name="flash_diff",
    )(qi_tab, ki_tab, *[a.reshape(1, DIFF_HEAD_DIM) for a in lams], subln.reshape(1, LANES), q, k, v)


def _decode_diff_kernel(pt_ref, lq1_ref, lk1_ref, lq2_ref, lk2_ref, sub_ref, q_ref, kn_ref, vn_ref,
                        k_hbm, v_hbm, o_ref, kbuf, vbuf, sem, m_scr, l_scr, acc_scr,
                        *, layer, n_pages, unit, lam_init):
    b, u = pl.program_id(0), pl.program_id(1)
    n_units = n_pages // unit
    step = b * n_units + u
    slot = step % 2

    def copies(st, sl, p):
        tok, un = st // n_units, st % n_units
        page = pt_ref[tok * n_pages + un * unit + p]
        return (pltpu.make_async_copy(k_hbm.at[layer, page], kbuf.at[sl, p], sem.at[0, sl]),
                pltpu.make_async_copy(v_hbm.at[layer, page], vbuf.at[sl, p], sem.at[1, sl]))

    def issue(st, sl):
        def body(p, carry):
            for cp in copies(st, sl, p):
                cp.start()
            return carry
        lax.fori_loop(0, unit, body, 0)

    @pl.when(step == 0)
    def _():
        issue(0, 0)

    @pl.when(step + 1 < pl.num_programs(0) * n_units)
    def _():
        issue(step + 1, 1 - slot)

    def wait_body(p, carry):
        for cp in copies(step, slot, p):
            cp.wait()
        return carry
    lax.fori_loop(0, unit, wait_body, 0)

    q = q_ref[0]

    @pl.when(u == 0)
    def _():
        m_scr[...] = jnp.sum(q.astype(F32) * kn_ref[0].astype(F32), axis=-1, keepdims=True)
        l_scr[...] = jnp.ones_like(l_scr)
        acc_scr[...] = vn_ref[0]

    kt = jnp.concatenate([kbuf[slot, pg].astype(BF16) for pg in range(unit)], axis=1)
    s = _dot(q, kt)
    m_old = m_scr[...]
    m_new = jnp.maximum(m_old, jnp.max(s, axis=-1, keepdims=True))
    alpha = jnp.exp2(m_old - m_new)
    p = jnp.exp2(s - m_new)
    l_scr[...] = alpha * l_scr[...] + jnp.sum(p, axis=-1, keepdims=True)
    m_scr[...] = m_new
    pb = p.astype(BF16)
    halves = []
    for gp in range(DIFF_GROUPS // 2):
        out = jnp.zeros((16, 2 * LANES), F32)
        for pp in range(unit // 2):
            stacked = []
            for pg in (2 * pp, 2 * pp + 1):
                side = [vbuf[slot, pg, pl.ds(2 * gp + i, PAGE_SIZE, stride=DIFF_GROUPS), :] for i in range(2)]
                stacked.append(jnp.concatenate(side, axis=1).astype(BF16))
            out = out + _dot(pb[:, 2 * pp * PAGE_SIZE:(2 * pp + 2) * PAGE_SIZE], jnp.concatenate(stacked, axis=0))
        halves += [out[:, :LANES], out[:, LANES:]]
    row_g = lax.broadcasted_iota(jnp.int32, (16, LANES), 0) % DIFF_GROUPS
    pv = jnp.where(row_g == 0, halves[0], jnp.where(row_g == 1, halves[1],
                                                    jnp.where(row_g == 2, halves[2], halves[3])))
    acc_scr[...] = alpha * acc_scr[...] + pv

    @pl.when(u == n_units - 1)
    def _():
        lam = _diff_lambda(lq1_ref, lk1_ref, lq2_ref, lk2_ref, lam_init)
        on = acc_scr[...] / l_scr[...]
        d = on[0:8] - lam * on[8:16]
        o_ref[0] = _rms(d, sub_ref[...]) * (1.0 - lam_init)


def _decode_diff(page_table, q_rows, k_new, v_rows, cache_kt, cache_v, lams, subln, lam_init, layer, unit=16):
    nb, n_pages = page_table.shape
    vec = pl.BlockSpec((1, DIFF_HEAD_DIM), lambda b, u, pt: (0, 0))
    return pl.pallas_call(
        functools.partial(_decode_diff_kernel, layer=layer, n_pages=n_pages, unit=unit, lam_init=lam_init),
        out_shape=jax.ShapeDtypeStruct((nb, 8, LANES), F32),
        grid_spec=pltpu.PrefetchScalarGridSpec(
            num_scalar_prefetch=1,
            grid=(nb, n_pages // unit),
            in_specs=[
                vec, vec, vec, vec,
                pl.BlockSpec((1, LANES), lambda b, u, pt: (0, 0)),
                pl.BlockSpec((1, 16, 512), lambda b, u, pt: (b, 0, 0)),
                pl.BlockSpec((1, 1, 512), lambda b, u, pt: (b, 0, 0)),
                pl.BlockSpec((1, 16, LANES), lambda b, u, pt: (b, 0, 0)),
                pl.BlockSpec(memory_space=pl.ANY),
                pl.BlockSpec(memory_space=pl.ANY),
            ],
            out_specs=pl.BlockSpec((1, 8, LANES), lambda b, u, pt: (b, 0, 0)),
            scratch_shapes=[
                pltpu.VMEM((2, unit, 512, PAGE_SIZE), F32),
                pltpu.VMEM((2, unit, 512, LANES), F32),
                pltpu.SemaphoreType.DMA((2, 2)),
                pltpu.VMEM((16, 1), F32),
                pltpu.VMEM((16, 1), F32),
                pltpu.VMEM((16, LANES), F32),
            ],
        ),
        compiler_params=_params(("arbitrary", "arbitrary")),
        name="decode_diff",
    )(page_table.reshape(-1), *[a.reshape(1, DIFF_HEAD_DIM) for a in lams], subln.reshape(1, LANES),
      q_rows, k_new.reshape(nb, 1, 512), v_rows, cache_kt, cache_v)


def _ssd_gate_norm(y, xs, z, d_ref, nw_ref):
    yg = (y + d_ref[...] * xs) * _silu(z)
    w = SSD_D_INNER // SSD_GROUPS
    parts = []
    for g in range(SSD_GROUPS):
        part = yg[:, g * w:(g + 1) * w]
        parts.append(part * lax.rsqrt(jnp.mean(part * part, axis=-1, keepdims=True) + RMS_EPS))
    return jnp.concatenate(parts, axis=1) * nw_ref[...]


def _ssd_prompt_kernel(z_ref, xbc_ref, dt_ref, cw_ref, cb_ref, dtb_ref, a_ref, e_ref, d_ref, nw_ref,
                       yn_ref, st_ref, buf_ref, xp_scr, s_scr, *, chunk):
    c = pl.program_id(1)

    @pl.when(c == 0)
    def _():
        xp_scr[0:8, :] = jnp.zeros((8, SSD_CONV_DIM), F32)
        s_scr[...] = jnp.zeros_like(s_scr)

    xp_scr[8:8 + chunk, :] = xbc_ref[...]
    conv = cb_ref[...] + sum(xp_scr[5 + k:5 + k + chunk, :] * cw_ref[k:k + 1, :] for k in range(SSD_CONV))
    u = _silu(conv)
    xs = u[:, :SSD_D_INNER]
    bm = u[:, SSD_D_INNER:SSD_D_INNER + 512]
    cm = u[:, SSD_D_INNER + 512:]

    dt = jax.nn.softplus(dt_ref[...] + dtb_ref[...])
    a = dt * a_ref[...]
    row = lax.broadcasted_iota(jnp.int32, (chunk, chunk), 0)
    col = lax.broadcasted_iota(jnp.int32, (chunk, chunk), 1)
    causal = col <= row
    tri = jnp.where(causal, 1.0, 0.0).astype(BF16)
    a_cs = _dot(jnp.concatenate([tri, tri, tri], axis=1),
                jnp.concatenate(_split_bf16x3(a), axis=0).astype(BF16))
    a_cs_t = a_cs.T
    a_last = a_cs[chunk - 1:chunk, :]
    cols = jnp.concatenate([dt, jnp.exp(a_cs), jnp.exp(a_last - a_cs)], axis=0)
    head_lane = lax.broadcasted_iota(jnp.int32, cols.shape, 1) < SSD_HEADS
    hi, mid, lo = _split_bf16x3(jnp.where(head_lane, cols, 0.0))
    packed = hi + pltpu.roll(mid, SSD_HEADS, 1) + pltpu.roll(lo, 2 * SSD_HEADS, 1)
    ex = _dot(packed.astype(BF16), e_ref[...])
    dtx, ecs, dend = ex[:chunk], ex[chunk:2 * chunk], ex[2 * chunk:]
    xdt = xs * dtx
    xdt_b = xdt.astype(BF16)
    xdtd_b = (xdt * dend).astype(BF16)
    etot = ecs[chunk - 1:chunk, :]
    low_head = lax.broadcasted_iota(jnp.int32, (chunk, LANES), 1) < SSD_HEAD_DIM

    ys = []
    for g in range(SSD_GROUPS):
        bg = bm[:, g * 128:(g + 1) * 128]
        cg = cm[:, g * 128:(g + 1) * 128].astype(BF16)
        cb = _dot_nt(cg, bg.astype(BF16))
        bg_t = bg.T.astype(BF16)
        for pi in range(4):
            pair = g * 4 + pi
            sl = slice(pair * LANES, (pair + 1) * LANES)
            ms = []
            for h in (2 * pair, 2 * pair + 1):
                seg = a_cs[:, h:h + 1] - a_cs_t[h:h + 1, :]
                ms.append((cb * jnp.exp(jnp.where(causal, seg, NEG_INF))).astype(BF16))
            y_diag = jnp.where(low_head, _dot(ms[0], xdt_b[:, sl]), _dot(ms[1], xdt_b[:, sl]))
            st = s_scr[:, sl]
            y_off = _dot(cg, st.astype(BF16)) * ecs[:, sl]
            s_scr[:, sl] = etot[:, sl] * st + _dot(bg_t, xdtd_b[:, sl])
            ys.append(y_diag + y_off)

    y = jnp.concatenate(ys, axis=1)
    yn_ref[...] = _ssd_gate_norm(y, xs, z_ref[...], d_ref, nw_ref).astype(yn_ref.dtype)
    xp_scr[0:8, :] = xp_scr[chunk:chunk + 8, :]

    @pl.when(c == pl.num_programs(1) - 1)
    def _():
        st_ref[0] = s_scr[...].T
        buf_ref[0] = xp_scr[chunk + 5:chunk + 8, :]


def _ssd_prompt(z, xbc, dt, cw, cb, dtb, a_neg, expand, d_exp, nw, batch, seq, chunk=SSD_CHUNK):
    nc = seq // chunk
    row = lambda b, c: (b * nc + c, 0)
    const = lambda b, c: (0, 0)
    return pl.pallas_call(
        functools.partial(_ssd_prompt_kernel, chunk=chunk),
        out_shape=(
            jax.ShapeDtypeStruct((batch * seq, SSD_D_INNER), BF16),
            jax.ShapeDtypeStruct((batch, SSD_D_INNER, SSD_STATE), F32),
            jax.ShapeDtypeStruct((batch, SSD_CONV - 1, SSD_CONV_DIM), F32),
        ),
        grid=(batch, nc),
        in_specs=[
            pl.BlockSpec((chunk, SSD_D_INNER), row),
            pl.BlockSpec((chunk, SSD_CONV_DIM), row),
            pl.BlockSpec((chunk, LANES), row),
            pl.BlockSpec((SSD_CONV, SSD_CONV_DIM), const),
            pl.BlockSpec((1, SSD_CONV_DIM), const),
            pl.BlockSpec((1, LANES), const),
            pl.BlockSpec((1, LANES), const),
            pl.BlockSpec((LANES, SSD_D_INNER), const),
            pl.BlockSpec((1, SSD_D_INNER), const),
            pl.BlockSpec((1, SSD_D_INNER), const),
        ],
        out_specs=(
            pl.BlockSpec((chunk, SSD_D_INNER), row),
            pl.BlockSpec((1, SSD_D_INNER, SSD_STATE), lambda b, c: (b, 0, 0)),
            pl.BlockSpec((1, SSD_CONV - 1, SSD_CONV_DIM), lambda b, c: (b, 0, 0)),
        ),
        scratch_shapes=[
            pltpu.VMEM((chunk + 8, SSD_CONV_DIM), F32),
            pltpu.VMEM((SSD_STATE, SSD_D_INNER), F32),
        ],
        compiler_params=_params(("parallel", "arbitrary")),
        name="ssd_prompt",
    )(z, xbc, dt, cw, cb, dtb, a_neg, expand, d_exp, nw)


def _ssd_decode_pre_kernel(xbc_ref, dt_ref, buf_ref, cw_ref, cb_ref, dtb_ref, a_ref, e_ref,
                           xs_ref, xdt_ref, dec_ref, b_ref, c_ref, nbuf_ref):
    xbc = xbc_ref[...]
    conv = cb_ref[...] + xbc * cw_ref[3:4, :]
    for k in range(SSD_CONV - 1):
        conv = conv + buf_ref[k] * cw_ref[k:k + 1, :]
    u = _silu(conv)
    xs = u[:, :SSD_D_INNER]
    dt = jax.nn.softplus(dt_ref[...] + dtb_ref[...])
    ex = _dot_exact(jnp.concatenate([dt, dt * a_ref[...]], axis=0), e_ref[...])
    t = xbc.shape[0]
    xs_ref[...] = xs
    xdt_ref[...] = xs * ex[:t]
    dec_ref[...] = jnp.exp(ex[t:])
    b_ref[...] = u[:, SSD_D_INNER:SSD_D_INNER + 512]
    c_ref[...] = u[:, SSD_D_INNER + 512:]
    nbuf_ref[0] = buf_ref[1]
    nbuf_ref[1] = buf_ref[2]
    nbuf_ref[2] = xbc


def _ssd_decode_pre(xbc, dt, buf, cw, cb, dtb, a_neg, expand):
    t = xbc.shape[0]
    wide = jax.ShapeDtypeStruct((t, SSD_D_INNER), F32)
    grp = jax.ShapeDtypeStruct((t, 512), F32)
    return pl.pallas_call(
        _ssd_decode_pre_kernel,
        out_shape=(wide, wide, wide, grp, grp, jax.ShapeDtypeStruct((SSD_CONV - 1, t, SSD_CONV_DIM), F32)),
        compiler_params=pltpu.CompilerParams(vmem_limit_bytes=VMEM_LIMIT),
        name="ssd_decode_pre",
    )(xbc, dt, buf, cw, cb, dtb, a_neg, expand)


def _split_bf16x3(x):
    hi = x.astype(BF16).astype(F32)
    mid = (x - hi).astype(BF16).astype(F32)
    lo = (x - hi - mid).astype(BF16).astype(F32)
    return hi, mid, lo


def _ssd_decode_state_kernel(xdt_ref, dec_ref, b_ref, c_ref, st_ref, y_ref, nst_ref):
    xdt = xdt_ref[0]
    lane_g = lax.broadcasted_iota(jnp.int32, (8, SSD_D_INNER), 1) // 512
    row = lax.broadcasted_iota(jnp.int32, (8, SSD_D_INNER), 0)
    lhs = jnp.where(row == lane_g, xdt, 0.0) + jnp.where(row == 4, dec_ref[0], 0.0)
    brow = lax.broadcasted_iota(jnp.int32, (8, LANES), 0)
    bmat = jnp.zeros((8, LANES), F32)
    cmat = jnp.zeros((8, LANES), F32)
    for g in range(SSD_GROUPS):
        bmat = jnp.where(brow == g, b_ref[0][:, g * 128:(g + 1) * 128], bmat)
        cmat = jnp.where(brow == g, c_ref[0][:, g * 128:(g + 1) * 128], cmat)
    rhs = jnp.concatenate([bmat, (brow == 4).astype(F32)], axis=1)
    lh, lm, ll = _split_bf16x3(lhs)
    rh, rm, rl = _split_bf16x3(rhs)
    lhs6 = jnp.concatenate([lh, lh, lm, lm, lh, ll], axis=0).astype(BF16)
    rhs6 = jnp.concatenate([rh, rm, rh, rm, rl, rh], axis=0).astype(BF16)
    both = lax.dot_general(lhs6, rhs6, TN_DIMS, preferred_element_type=F32)
    new = st_ref[0] * both[:, LANES:] + both[:, :LANES]
    nst_ref[0] = new
    yg = _dot_nt(cmat.astype(BF16), new.astype(BF16))
    y_ref[0] = jnp.sum(jnp.where(row == lane_g, yg, 0.0), axis=0, keepdims=True)


def _ssd_decode_state(xdt, dec, bm, cm, state):
    t = xdt.shape[0]
    tok3 = lambda w: pl.BlockSpec((1, 1, w), lambda i: (i, 0, 0))
    st_spec = pl.BlockSpec((1, SSD_D_INNER, SSD_STATE), lambda i: (i, 0, 0))
    return pl.pallas_call(
        _ssd_decode_state_kernel,
        out_shape=(jax.ShapeDtypeStruct((t, 1, SSD_D_INNER), F32),
                   jax.ShapeDtypeStruct((t, SSD_D_INNER, SSD_STATE), F32)),
        grid=(t,),
        in_specs=[tok3(SSD_D_INNER), tok3(SSD_D_INNER), tok3(512), tok3(512), st_spec],
        out_specs=(tok3(SSD_D_INNER), st_spec),
        compiler_params=_params(("parallel",)),
        name="ssd_decode_state",
    )(xdt.reshape(t, 1, -1), dec.reshape(t, 1, -1), bm.reshape(t, 1, -1), cm.reshape(t, 1, -1), state)


def _ssd_decode_post_kernel(y_ref, xs_ref, z_ref, d_ref, nw_ref, o_ref):
    o_ref[...] = _ssd_gate_norm(y_ref[...], xs_ref[...], z_ref[...], d_ref, nw_ref).astype(o_ref.dtype)


def _ssd_decode_post(y, xs, z, d_exp, nw):
    return pl.pallas_call(
        _ssd_decode_post_kernel,
        out_shape=jax.ShapeDtypeStruct(y.shape, BF16),
        compiler_params=pltpu.CompilerParams(vmem_limit_bytes=VMEM_LIMIT),
        name="ssd_decode_post",
    )(y, xs, z, d_exp, nw)


def _xa_prompt_kernel(x_ref, g_ref, wq_ref, k_ref, v_ref, wo_ref, o_ref):
    x = x_ref[...]
    q = _dot(_rms(x, g_ref[...]).astype(BF16), wq_ref[...]).astype(BF16)
    k = k_ref[0].astype(BF16)
    v = v_ref[0].astype(BF16)
    heads = []
    for h in range(XA_HEADS):
        sl = slice(h * XA_HEAD_DIM, (h + 1) * XA_HEAD_DIM)
        s = _dot_nt(q[:, sl], k[:, sl]) * XA_SCALE
        p = jnp.exp(s - jnp.max(s, axis=-1, keepdims=True))
        o = _dot(p.astype(BF16), v[:, sl]) / jnp.sum(p, axis=-1, keepdims=True)
        heads.append(o.astype(BF16))
    o_ref[...] = x + _dot(jnp.concatenate(heads, axis=1), wo_ref[...])


def _xa_prompt(x, gain, wq, mem_k, mem_v, wo, batch, seq):
    m, d = x.shape
    tm = 512
    nt = seq // tm
    mem = mem_k.shape[1]
    return pl.pallas_call(
        _xa_prompt_kernel,
        out_shape=jax.ShapeDtypeStruct((m, d), F32),
        grid=(m // tm,),
        in_specs=[
            pl.BlockSpec((tm, d), lambda i: (i, 0)),
            pl.BlockSpec((1, d), lambda i: (0, 0)),
            pl.BlockSpec((d, d), lambda i: (0, 0)),
            pl.BlockSpec((1, mem, d), lambda i: (i // nt, 0, 0)),
            pl.BlockSpec((1, mem, d), lambda i: (i // nt, 0, 0)),
            pl.BlockSpec((d, d), lambda i: (0, 0)),
        ],
        out_specs=pl.BlockSpec((tm, d), lambda i: (i, 0)),
        compiler_params=_params(("parallel",)),
        name="xa_prompt",
    )(x, gain.reshape(1, d), wq, mem_k, mem_v, wo)


def _xa_decode_kernel(q_ref, k_ref, v_ref, o_ref, *, tokens):
    n = k_ref.shape[2]
    rows = 8 * tokens
    lane = lax.broadcasted_iota(jnp.int32, (rows, n), 1)
    own = lane % 8 == lax.broadcasted_iota(jnp.int32, (rows, n), 0) % 8
    first_half = lane % 8 < XA_HEADS
    strides = [8 << i for i in range((n // 8).bit_length() - 1)]

    def over_tokens(op, x):
        for sh in strides:
            x = op(x, pltpu.roll(x, sh, 1))
        return x

    g = jnp.concatenate([_dot_nt(q_ref[0, t].astype(BF16), k_ref[0, t].astype(BF16)) for t in range(tokens)],
                        axis=0)
    part = jnp.sum(jnp.where(own, g, 0.0).reshape(tokens, 8, n), axis=1, keepdims=True)
    part = jnp.broadcast_to(part, (tokens, 8, n)).reshape(rows, n)
    s = (part + pltpu.roll(part, n - XA_HEADS, 1)) * XA_SCALE
    p = jnp.exp(s - over_tokens(jnp.maximum, s))
    p = p / over_tokens(jnp.add, p)
    p = jnp.where(own, jnp.where(first_half, p, pltpu.roll(p, XA_HEADS, 1)), 0.0)
    for t in range(tokens):
        o_ref[0, t] = _dot(p[8 * t:8 * t + 8].astype(BF16), v_ref[0, t].astype(BF16))


def _xa_decode(q, mem_k, mem_v, layer, tokens=4):
    t, d = q.shape
    rows = mem_k.shape[2]
    half = XA_HEAD_DIM // 2
    q8 = jnp.transpose(q.reshape(t, XA_HEADS, 2, half), (0, 2, 1, 3)).reshape(t // tokens, tokens, 8, half)
    kv_spec = pl.BlockSpec((1, tokens, rows, half), lambda i: (layer, i, 0, 0))
    tok_spec = pl.BlockSpec((1, tokens, 8, half), lambda i: (i, 0, 0, 0))
    o8 = pl.pallas_call(
        functools.partial(_xa_decode_kernel, tokens=tokens),
        out_shape=jax.ShapeDtypeStruct((t // tokens, tokens, 8, half), F32),
        grid=(t // tokens,),
        in_specs=[tok_spec, kv_spec, kv_spec],
        out_specs=tok_spec,
        compiler_params=_params(("parallel",)),
        name="xa_decode",
    )(q8, mem_k, mem_v)
    return jnp.transpose(o8.reshape(t, 2, XA_HEADS, half), (0, 2, 1, 3)).reshape(t, d)


def kernel(x_prompt, x_sample, cache_mla_ckv, cache_mla_kpe, cache_diff_k, cache_diff_v, state_ssm, state_conv, cache_mem_k, cache_mem_v, page_table, mem_prompt, norm_mix, norm_xa, norm_ffn, norm_final, xa_mem_norm, xa_wq, xa_wk, xa_wv, xa_wo, ffn_w_gate, ffn_w_up, ffn_w_down, mla_wq_a, mla_q_norm, mla_wq_b, mla_wkv_a, mla_kv_norm, mla_w_uk, mla_w_uv, mla_wo, diff_wq, diff_wk, diff_wv, diff_lambda_q1, diff_lambda_k1, diff_lambda_q2, diff_lambda_k2, diff_subln, diff_wo, ssd_w_in, ssd_conv_w, ssd_conv_b, ssd_dt_bias, ssd_A_log, ssd_D, ssd_norm, ssd_w_out):
    batch, seq, d = x_prompt.shape
    nb = x_sample.shape[0]
    depth = norm_mix.shape[0]
    n_mix = 3
    bf = lambda w: w.astype(BF16)

    n_a = mla_wq_a.shape[0]
    wkv_a = jnp.pad(mla_wkv_a, ((0, 0), (0, 0), (0, MLA_QK - mla_wkv_a.shape[2])))
    w_qkv_a = bf(jnp.concatenate([mla_wq_a, wkv_a], axis=2))
    wq_b = mla_wq_b.reshape(n_a, -1, MLA_HEADS, MLA_NOPE + MLA_ROPE)
    wq_b = bf(jnp.concatenate([
        wq_b[..., :MLA_NOPE].reshape(n_a, -1, MLA_HEADS * MLA_NOPE),
        jnp.pad(wq_b[..., MLA_NOPE:], ((0, 0), (0, 0), (0, 0), (0, 64))).reshape(n_a, -1, MLA_HEADS * 128),
    ], axis=2))
    w_uk = bf(jnp.transpose(mla_w_uk, (0, 2, 3, 1)))
    w_uk_t = bf(jnp.transpose(mla_w_uk, (0, 2, 1, 3)))
    w_uv = bf(jnp.transpose(mla_w_uv, (0, 2, 1, 3)))
    w_mla_o = bf(mla_wo)
    w_diff_qkv = bf(jnp.concatenate([diff_wq, diff_wk, diff_wv], axis=2))
    w_diff_o = bf(diff_wo)
    w_ssd_in = bf(jnp.pad(ssd_w_in, ((0, 0), (0, 0), (0, LANES - SSD_HEADS))))
    w_ssd_out = bf(ssd_w_out)
    pad_heads = lambda a: jnp.pad(a, ((0, 0), (0, LANES - SSD_HEADS)))
    ssd_dtb = pad_heads(ssd_dt_bias)
    ssd_a = pad_heads(-jnp.exp(ssd_A_log.astype(F32)))
    ssd_d_exp = jnp.repeat(ssd_D, SSD_HEAD_DIM, axis=1)
    lane_head = jnp.arange(SSD_D_INNER)[None, :] // SSD_HEAD_DIM
    expand = (jnp.arange(LANES)[:, None] == lane_head).astype(F32)
    expand3 = ((jnp.arange(LANES)[:, None] % SSD_HEADS == lane_head)
               & (jnp.arange(LANES)[:, None] < 3 * SSD_HEADS)).astype(BF16)
    w_xa_q, w_xa_o = bf(xa_wq), bf(xa_wo)
    w_xa_kv = bf(jnp.concatenate([xa_wk, xa_wv], axis=2))
    w_gate, w_up, w_down = bf(ffn_w_gate), bf(ffn_w_up), bf(ffn_w_down)

    def mem_view(c):
        mt = c.shape[2]
        c = c.reshape(depth, nb, mt, XA_HEADS, 2, XA_HEAD_DIM // 2)
        return jnp.transpose(c, (0, 1, 2, 4, 3, 5)).reshape(depth, nb, mt * 8, XA_HEAD_DIM // 2)

    mem_k, mem_v = mem_view(cache_mem_k), mem_view(cache_mem_v)
    n_pool = cache_diff_k.shape[1]
    pool_kpe_t = jnp.swapaxes(cache_mla_kpe, 2, 3)
    pool_kt = jnp.transpose(cache_diff_k, (0, 1, 3, 4, 5, 2)).reshape(-1, n_pool, 512, PAGE_SIZE)
    pool_v = cache_diff_v.reshape(-1, n_pool, PAGE_SIZE * DIFF_GROUPS, 2 * DIFF_HEAD_DIM)
    mem_flat = mem_prompt.reshape(-1, d)

    def run(x, pos_tab, prompt):
        m = x.shape[0]
        cos, sin = _rope_tables(pos_tab)
        new = {k: [] for k in ("mla_ckv", "mla_kpe", "diff_k", "diff_v", "ssm", "conv", "mem_k", "mem_v")}
        for i in range(depth):
            kind, j = i % n_mix, i // n_mix
            g_mix = norm_mix[i]
            if kind == 0:
                qkv_a = _linear(x, w_qkv_a[j], gain=g_mix, name="mla_qkv_a")
                q = _linear(qkv_a, wq_b[j], gain=mla_q_norm[j], name="mla_q_b")
                if prompt:
                    q_heads, k_heads, ckv, kpe, v_lat = _mla_prep(q, qkv_a, cos, sin, w_uk_t[j], mla_kv_norm[j], False)
                    o_lat = _flash_mla(q_heads, k_heads, v_lat, batch, seq)
                else:
                    qcat, kcat, ckv, kpe = _mla_prep(q, qkv_a, cos, sin, w_uk[j], mla_kv_norm[j], True)
                    o_lat = _decode_mla(page_table, jnp.transpose(qcat, (1, 0, 2)), kcat,
                                        cache_mla_ckv, pool_kpe_t, j).reshape(m, -1)
                x = _mla_out(o_lat, w_uv[j], w_mla_o[j], x)
                new["mla_ckv"].append(ckv)
                new["mla_kpe"].append(kpe)
            elif kind == 1:
                lam_init = 0.8 - 0.6 * math.exp(-0.3 * i)
                lams = (diff_lambda_q1[j], diff_lambda_k1[j], diff_lambda_q2[j], diff_lambda_k2[j])
                qkv = _linear(x, w_diff_qkv[j], gain=g_mix, name="diff_qkv")
                q_b, k32, k_b, v32, v_b = _diff_prep(qkv, cos, sin, batch if prompt else None)
                if prompt:
                    k32 = jnp.transpose(k32.reshape(batch, DIFF_GROUPS, 2, DIFF_HEAD_DIM, seq), (0, 4, 1, 2, 3))
                if prompt:
                    o = _flash_diff(q_b, k_b, v_b, lams, diff_subln[j], lam_init, batch, seq)
                else:
                    q5 = q_b.reshape(m, DIFF_GROUPS, 2, 2, DIFF_HEAD_DIM)
                    eye_g = jnp.eye(DIFF_GROUPS, dtype=BF16)
                    eye_c = jnp.eye(2, dtype=BF16)
                    q_rows = jnp.einsum("tgrcd,gG,cC->tcrgGCd", q5, eye_g, eye_c).reshape(m, 16, 512)
                    v_rows = jnp.tile(v_b.astype(F32).reshape(m, DIFF_GROUPS, LANES), (1, 4, 1))
                    o = _decode_diff(page_table, q_rows, k_b, v_rows, pool_kt, pool_v, lams, diff_subln[j],
                                     lam_init, j)
                    o = jnp.transpose(o.reshape(m, 2, DIFF_GROUPS, LANES), (0, 2, 1, 3)).reshape(m, -1)
                x = _linear(o, w_diff_o[j], res=x, name="diff_o")
                new["diff_k"].append(k32)
                new["diff_v"].append(v32)
            else:
                z, xbc, dt = _linear(x, w_ssd_in[j], gain=g_mix, splits=(SSD_D_INNER, SSD_CONV_DIM, LANES),
                                     name="ssd_in")
                cw, cb = ssd_conv_w[j], ssd_conv_b[j].reshape(1, -1)
                dtb, a_neg = ssd_dtb[j].reshape(1, -1), ssd_a[j].reshape(1, -1)
                d_exp, nw = ssd_d_exp[j].reshape(1, -1), ssd_norm[j].reshape(1, -1)
                if prompt:
                    yn, st, buf = _ssd_prompt(z, xbc, dt, cw, cb, dtb, a_neg, expand3, d_exp, nw, batch, seq)
                else:
                    xs, xdt, dec, bm, cm, nbuf = _ssd_decode_pre(
                        xbc, dt, jnp.transpose(state_conv[j], (1, 0, 2)), cw, cb, dtb, a_neg, expand)
                    y, st = _ssd_decode_state(xdt, dec, bm, cm, state_ssm[j].reshape(m, SSD_D_INNER, SSD_STATE))
                    yn = _ssd_decode_post(y.reshape(m, -1), xs, z, d_exp, nw)
                    buf = jnp.transpose(nbuf, (1, 0, 2))
                x = _linear(yn, w_ssd_out[j], res=x, name="ssd_out")
                new["ssm"].append(st.reshape(-1, SSD_HEADS, SSD_HEAD_DIM, SSD_STATE))
                new["conv"].append(buf)
            if prompt:
                mk, mv = _linear(mem_flat, w_xa_kv[i], gain=xa_mem_norm[i], splits=(d, d), name="xa_mem_kv")
                new["mem_k"].append(mk)
                new["mem_v"].append(mv)
                x = _xa_prompt(x, norm_xa[i], w_xa_q[i], mk.reshape(batch, -1, d), mv.reshape(batch, -1, d),
                               w_xa_o[i], batch, seq)
            else:
                q = _linear(x, w_xa_q[i], gain=norm_xa[i], name="xa_q")
                o = _xa_decode(q, mem_k, mem_v, i)
                x = _linear(o, w_xa_o[i], res=x, name="xa_o")
            x = _ffn(x, norm_ffn[i], w_gate[i], w_up[i], w_down[i], norm_final if i == depth - 1 else None)
        stack = lambda v: v[0][None] if len(v) == 1 else jnp.stack(v)
        return x, {k: stack(v) for k, v in new.items() if v}

    pos_p = jnp.arange(seq, dtype=jnp.int32)
    pos_s = jnp.full((nb,), PAST_LEN, jnp.int32)
    y_p, new_p = run(x_prompt.reshape(batch * seq, d), pos_p, True)
    y_s, new_s = run(x_sample.reshape(nb, d), pos_s, False)

    n_b = new_p["diff_k"].shape[0]
    n_c = new_p["ssm"].shape[0]
    mem_t = mem_prompt.shape[1]
    return (
        y_p.reshape(batch, seq, d),
        y_s.reshape(nb, 1, d),
        new_p["mla_ckv"].reshape(n_a, batch, seq, MLA_KV_LORA),
        new_p["mla_kpe"].reshape(n_a, batch, seq, MLA_ROPE),
        new_p["diff_k"].reshape(n_b, batch, seq, DIFF_GROUPS, 2, DIFF_HEAD_DIM),
        new_p["diff_v"].reshape(n_b, batch, seq, DIFF_GROUPS, 2 * DIFF_HEAD_DIM),
        new_p["ssm"].reshape(n_c, batch, SSD_HEADS, SSD_HEAD_DIM, SSD_STATE),
        new_p["conv"],
        new_p["mem_k"].reshape(depth, batch, mem_t, XA_HEADS, XA_HEAD_DIM),
        new_p["mem_v"].reshape(depth, batch, mem_t, XA_HEADS, XA_HEAD_DIM),
        new_s["mla_ckv"].reshape(n_a, nb, 1, MLA_KV_LORA),
        new_s["mla_kpe"].reshape(n_a, nb, 1, MLA_ROPE),
        new_s["diff_k"].reshape(n_b, nb, 1, DIFF_GROUPS, 2, DIFF_HEAD_DIM),
        new_s["diff_v"].reshape(n_b, nb, 1, DIFF_GROUPS, 2 * DIFF_HEAD_DIM),
        new_s["ssm"].reshape(n_c, nb, SSD_HEADS, SSD_HEAD_DIM, SSD_STATE),
        new_s["conv"],
    )
```
